```python
import math
import jax
import jax.numpy as jnp
from jax import lax
import numpy as np

D_MODEL = 1024
BATCH = 8
SEQ = 4096
DEPTH = 2

GRID_W = 64
CTX_LEN = 256
HEAD_DIM = 64
MIX_HEADS = D_MODEL // HEAD_DIM
A_Q_HEADS = MIX_HEADS // 2
A_KV_HEADS = A_Q_HEADS // 4
B_Q_HEADS = MIX_HEADS - A_Q_HEADS
B_KV_HEADS = B_Q_HEADS // 4
WINDOW = 128
BLOCK = 128
NA_ROWS = 8
NA_COLS = 16
NA_QCOLS = 16
NA_KCOLS = NA_QCOLS + NA_COLS
C_HEADS = D_MODEL // (2 * HEAD_DIM)
C_V_DIM = 2 * HEAD_DIM
D_FF = 4 * D_MODEL
ROPE_THETA = 10000.0
EPS = 1e-6

A_QW = A_Q_HEADS * HEAD_DIM
A_KVW = A_KV_HEADS * HEAD_DIM
B_QW = B_Q_HEADS * HEAD_DIM
B_KVW = B_KV_HEADS * HEAD_DIM
EVEN_WIDTHS = (A_QW, A_KVW, A_KVW, B_QW, B_KVW, B_KVW)
EVEN_IN = A_QW + 2 * A_KVW + B_QW + 2 * B_KVW
EVEN_OUT = A_QW + B_QW
C_QKW = C_HEADS * 2 * HEAD_DIM
C_VW = C_HEADS * C_V_DIM
ODD_IN = 2 * C_QKW + C_VW

kernel_name = 'hybrid_dit_window_natten_diffattn'


def _rms_norm(x, g):
    xf = x.astype(jnp.float32)
    y = xf * lax.rsqrt(jnp.mean(xf * xf, axis=-1, keepdims=True) + EPS)
    return (y * g.astype(jnp.float32)).astype(x.dtype)


def _modulate(x, g, shift, scale):
    return _rms_norm(x, g) * (1 + scale) + shift


def _rope_tables(n):
    t = jnp.arange(n, dtype=jnp.int32)
    row = (t // GRID_W).astype(jnp.float32)
    col = (t % GRID_W).astype(jnp.float32)
    quarter = HEAD_DIM // 4
    inv_freq = ROPE_THETA ** (-jnp.arange(quarter, dtype=jnp.float32) / quarter)
    ar = row[:, None] * inv_freq[None, :]
    ac = col[:, None] * inv_freq[None, :]
    ang = jnp.concatenate([ar, ar, ac, ac], axis=-1)
    return jnp.cos(ang), jnp.sin(ang)


def _apply_rope(x, cos, sin):
    xs = x.reshape(x.shape[:-1] + (2, 2, HEAD_DIM // 4))
    rot = jnp.stack([-xs[..., 1, :], xs[..., 0, :]], axis=-2).reshape(x.shape)
    return x * cos[:, None, :].astype(x.dtype) + rot * sin[:, None, :].astype(x.dtype)


def _ctx_attn(q, k, v, sink):
    b, n, hq, d = q.shape
    hkv = k.shape[2]
    g = hq // hkv
    qg = q.reshape(b, n, hkv, g, d)
    s = jnp.einsum('bqhgd,bkhd->bhgqk', qg, k).astype(jnp.float32) * d ** -0.5
    if sink is not None:
        sk = jnp.broadcast_to(sink.astype(jnp.float32).reshape(1, hkv, g, 1, 1), s.shape[:-1] + (1,))
        s = jnp.concatenate([s, sk], axis=-1)
    p = jax.nn.softmax(s, axis=-1)[..., :n].astype(v.dtype)
    return jnp.einsum('bhgqk,bkhd->bqhgd', p, v).reshape(b, n, hq * d)


def _window_attn(q, k, v, kc, vc, sink):
    b, n, hq, d = q.shape
    hkv = k.shape[2]
    g = hq // hkv
    nb = n // BLOCK
    qb = q.reshape(b, nb, BLOCK, hkv, g, d)
    pad = ((0, 0), (BLOCK, BLOCK), (0, 0), (0, 0))

    def band(t):
        tp = jnp.pad(t, pad).reshape(b, nb + 2, BLOCK, hkv, d)
        return jnp.concatenate([tp[:, :-2], tp[:, 1:-1], tp[:, 2:]], axis=2)

    kb, vb = band(k), band(v)
    scale = d ** -0.5
    s_lat = jnp.einsum('bnqhgd,bnkhd->bhgnqk', qb, kb).astype(jnp.float32) * scale
    blk = jnp.arange(nb)[:, None, None] * BLOCK
    qpos = blk + jnp.arange(BLOCK)[None, :, None]
    kpos = blk - BLOCK + jnp.arange(3 * BLOCK)[None, None, :]
    valid = (jnp.abs(qpos - kpos) <= WINDOW) & (kpos >= 0) & (kpos < n)
    s_lat = jnp.where(valid, s_lat, -jnp.inf)
    s_ctx = jnp.einsum('bnqhgd,bchd->bhgnqc', qb, kc).astype(jnp.float32) * scale
    s_sink = jnp.broadcast_to(sink.astype(jnp.float32).reshape(1, hkv, g, 1, 1, 1), s_ctx.shape[:-1] + (1,))
    p = jax.nn.softmax(jnp.concatenate([s_lat, s_ctx, s_sink], axis=-1), axis=-1).astype(v.dtype)
    nk, nc = 3 * BLOCK, kc.shape[1]
    out = (jnp.einsum('bhgnqk,bnkhd->bnqhgd', p[..., :nk], vb)
           + jnp.einsum('bhgnqc,bchd->bnqhgd', p[..., nk:nk + nc], vc))
    return out.reshape(b, n, hq * d)


def _neighbourhood_attn(q, k, v, kc, vc, rpb):
    b, n, hq, d = q.shape
    hkv = k.shape[2]
    g = hq // hkv
    rows = n // GRID_W
    kr = min(NA_ROWS, rows)
    ncb = GRID_W // NA_QCOLS
    nkey = kr * NA_KCOLS
    r = jnp.arange(rows)
    row_idx = jnp.clip(r - kr // 2, 0, rows - kr)[:, None] + jnp.arange(kr)[None, :]
    m = jnp.arange(ncb)
    col_idx = jnp.clip(m * NA_QCOLS - NA_COLS // 2, 0, GRID_W - NA_KCOLS)[:, None] + jnp.arange(NA_KCOLS)[None, :]
    qcol = m[:, None] * NA_QCOLS + jnp.arange(NA_QCOLS)[None, :]
    cstart = jnp.clip(qcol - NA_COLS // 2, 0, GRID_W - NA_COLS)
    kcol = col_idx[:, None, :]
    col_ok = (kcol >= cstart[..., None]) & (kcol < cstart[..., None] + NA_COLS)
    mask = jnp.broadcast_to(col_ok[:, :, None, :], (ncb, NA_QCOLS, kr, NA_KCOLS)).reshape(ncb, NA_QCOLS, nkey)

    def gather(t):
        tg = t.reshape(b, rows, GRID_W, hkv, d)
        tb = tg[:, row_idx[:, None, :, None], col_idx[None, :, None, :]]
        return tb.reshape(b, rows, ncb, nkey, hkv, d)

    kb, vb = gather(k), gather(v)
    roff = row_idx - r[:, None] + (NA_ROWS - 1)
    coff = jnp.clip(kcol - qcol[..., None], 1 - NA_COLS, NA_COLS - 1) + (NA_COLS - 1)
    bias = rpb[:, roff[:, :, None, None, None], coff[None, None]]
    bias = jnp.transpose(bias, (0, 1, 3, 4, 2, 5)).reshape(hkv, g, rows, ncb, NA_QCOLS, nkey).astype(jnp.float32)
    qb = q.reshape(b, rows, ncb, NA_QCOLS, hkv, g, d)
    scale = d ** -0.5
    s_nb = jnp.einsum('brmqhgd,brmkhd->bhgrmqk', qb, kb).astype(jnp.float32) * scale + bias
    s_nb = jnp.where(mask, s_nb, -jnp.inf)
    s_ctx = jnp.einsum('brmqhgd,bchd->bhgrmqc', qb, kc).astype(jnp.float32) * scale
    p = jax.nn.softmax(jnp.concatenate([s_nb, s_ctx], axis=-1), axis=-1).astype(v.dtype)
    out = (jnp.einsum('bhgrmqk,brmkhd->brmqhgd', p[..., :nkey], vb)
           + jnp.einsum('bhgrmqc,bchd->brmqhgd', p[..., nkey:], vc))
    return out.reshape(b, n, hq * d)


def _split_even(p):
    b, n, _ = p.shape
    idx = np.cumsum(EVEN_WIDTHS)[:-1].tolist()
    parts = jnp.split(p, idx, axis=-1)
    return [t.reshape(b, n, -1, HEAD_DIM) for t in parts]


def _even_mixer(hx, hc, w_in, w_out, sink, rpb, cos, sin, with_ctx):
    aq, ak, av, bq, bk, bv = _split_even(hx @ w_in)
    caq, cak, cav, cbq, cbk, cbv = _split_even(hc @ w_in)
    aq = _apply_rope(aq, cos, sin)
    ak = _apply_rope(ak, cos, sin)
    ya = _window_attn(aq, ak, av, cak, cav, sink)
    yb = _neighbourhood_attn(bq, bk, bv, cbk, cbv, rpb)
    y_x = jnp.concatenate([ya, yb], axis=-1) @ w_out
    if not with_ctx:
        return y_x, None
    y_c = jnp.concatenate([_ctx_attn(caq, cak, cav, sink), _ctx_attn(cbq, cbk, cbv, None)], axis=-1) @ w_out
    return y_x, y_c


def _split_odd(p):
    b, n, _ = p.shape
    q, k, v = jnp.split(p, [C_QKW, 2 * C_QKW], axis=-1)
    return (q.reshape(b, n, C_HEADS, 2, HEAD_DIM), k.reshape(b, n, C_HEADS, 2, HEAD_DIM),
            v.reshape(b, n, C_HEADS, C_V_DIM))


def _diff_core(q, k, v, lam, sub_g, lam_init):
    b, nq, h, _, d = q.shape
    s = jnp.einsum('bqhmd,bkhmd->bhmqk', q, k).astype(jnp.float32) * d ** -0.5
    p = jax.nn.softmax(s, axis=-1)
    a = (p[:, :, 0] - lam * p[:, :, 1]).astype(v.dtype)
    o = jnp.einsum('bhqk,bkhe->bqhe', a, v)
    o = _rms_norm(o, sub_g) * (1.0 - lam_init)
    return o.reshape(b, nq, h * v.shape[-1])


def _odd_mixer(hx, hc, w_in, w_out, lq1, lk1, lq2, lk2, sub_g, lam_init, cos, sin, with_ctx):
    q, k, v = _split_odd(hx @ w_in)
    cq, ck, cv = _split_odd(hc @ w_in)
    b, n = q.shape[:2]
    q = _apply_rope(q.reshape(b, n, 2 * C_HEADS, HEAD_DIM), cos, sin).reshape(q.shape)
    k = _apply_rope(k.reshape(b, n, 2 * C_HEADS, HEAD_DIM), cos, sin).reshape(k.shape)
    f32 = jnp.float32
    lam = (jnp.exp(jnp.sum(lq1.astype(f32) * lk1.astype(f32)))
           - jnp.exp(jnp.sum(lq2.astype(f32) * lk2.astype(f32))) + lam_init)
    kf = jnp.concatenate([k, ck], axis=1)
    vf = jnp.concatenate([v, cv], axis=1)
    nb = n // BLOCK
    qb = jnp.swapaxes(q.reshape(b, nb, BLOCK, C_HEADS, 2, HEAD_DIM), 0, 1)
    yx = lax.map(lambda qblk: _diff_core(qblk, kf, vf, lam, sub_g, lam_init), qb)
    yx = jnp.swapaxes(yx, 0, 1).reshape(b, n, C_VW) @ w_out
    if not with_ctx:
        return yx, None
    yc = _diff_core(cq, ck, cv, lam, sub_g, lam_init) @ w_out
    return yx, yc


def _mlp(h, w1, w2):
    a = jax.nn.relu(h @ w1)
    return (a * a) @ w2


def setup_inputs(seed: int = 0) -> dict:
    key = jax.random.key(seed)
    ks = jax.random.split(key, 22)
    n_even = (DEPTH + 1) // 2
    n_odd = DEPTH // 2

    def nrm(k, shape, s):
        return jax.random.normal(k, shape, jnp.float32) * s

    return {
        'x': nrm(ks[0], (BATCH, SEQ, D_MODEL), 1.0),
        'c': nrm(ks[1], (BATCH, D_MODEL), 1.0),
        'ctx': nrm(ks[2], (BATCH, CTX_LEN, D_MODEL), 1.0),
        'c_ctx': nrm(ks[3], (D_MODEL,), 1.0),
        'ada_w': nrm(ks[4], (DEPTH, D_MODEL, 6 * D_MODEL), 0.5 * D_MODEL ** -0.5),
        'ada_b': nrm(ks[5], (DEPTH, 6 * D_MODEL), 0.01),
        'norm1_g': 1.0 + nrm(ks[6], (DEPTH, D_MODEL), 0.05),
        'norm2_g': 1.0 + nrm(ks[7], (DEPTH, D_MODEL), 0.05),
        'even_w_in': nrm(ks[8], (n_even, D_MODEL, EVEN_IN), D_MODEL ** -0.5),
        'even_w_out': nrm(ks[9], (n_even, EVEN_OUT, D_MODEL), EVEN_OUT ** -0.5),
        'a_sink': nrm(ks[10], (n_even, A_Q_HEADS), 0.5),
        'b_rpb': nrm(ks[11], (n_even, B_Q_HEADS, 2 * NA_ROWS - 1, 2 * NA_COLS - 1), 0.1),
        'odd_w_in': nrm(ks[12], (n_odd, D_MODEL, ODD_IN), D_MODEL ** -0.5),
        'odd_w_out': nrm(ks[13], (n_odd, C_VW, D_MODEL), C_VW ** -0.5),
        'lam_q1': nrm(ks[14], (n_odd, HEAD_DIM), 0.1),
        'lam_k1': nrm(ks[15], (n_odd, HEAD_DIM), 0.1),
        'lam_q2': nrm(ks[16], (n_odd, HEAD_DIM), 0.1),
        'lam_k2': nrm(ks[17], (n_odd, HEAD_DIM), 0.1),
        'subln_g': 1.0 + nrm(ks[18], (n_odd, C_V_DIM), 0.05),
        'mlp_w1': nrm(ks[19], (DEPTH, D_MODEL, D_FF), D_MODEL ** -0.5),
        'mlp_w2': nrm(ks[20], (DEPTH, D_FF, D_MODEL), D_FF ** -0.5),
        'final_g': 1.0 + nrm(ks[21], (D_MODEL,), 0.05),
    }


def reference(x, c, ctx, c_ctx, ada_w, ada_b, norm1_g, norm2_g, even_w_in, even_w_out, a_sink, b_rpb,
              odd_w_in, odd_w_out, lam_q1, lam_k1, lam_q2, lam_k2, subln_g, mlp_w1, mlp_w2, final_g):
    n = x.shape[1]
    cos, sin = _rope_tables(n)
    hx, hc = x, ctx
    for i in range(DEPTH):
        with_ctx = i < DEPTH - 1
        j = i // 2
        mx = (jax.nn.silu(c) @ ada_w[i] + ada_b[i])[:, None, :]
        mc = jax.nn.silu(c_ctx) @ ada_w[i] + ada_b[i]
        sx1, cx1, gx1, sx2, cx2, gx2 = jnp.split(mx, 6, axis=-1)
        sc1, cc1, gc1, sc2, cc2, gc2 = jnp.split(mc, 6, axis=-1)
        ax = _modulate(hx, norm1_g[i], sx1, cx1)
        ac = _modulate(hc, norm1_g[i], sc1, cc1)
        if i % 2 == 0:
            yx, yc = _even_mixer(ax, ac, even_w_in[j], even_w_out[j], a_sink[j], b_rpb[j], cos, sin, with_ctx)
        else:
            lam_init = 0.8 - 0.6 * math.exp(-0.3 * i)
            yx, yc = _odd_mixer(ax, ac, odd_w_in[j], odd_w_out[j], lam_q1[j], lam_k1[j], lam_q2[j], lam_k2[j],
                                subln_g[j], lam_init, cos, sin, with_ctx)
        hx = hx + gx1 * yx
        hx = hx + gx2 * _mlp(_modulate(hx, norm2_g[i], sx2, cx2), mlp_w1[i], mlp_w2[i])
        if with_ctx:
            hc = hc + gc1 * yc
            hc = hc + gc2 * _mlp(_modulate(hc, norm2_g[i], sc2, cc2), mlp_w1[i], mlp_w2[i])
    return _rms_norm(hx, final_g)
```

```python
import functools
import math

import jax
import jax.numpy as jnp
from jax import lax
from jax.experimental import pallas as pl
from jax.experimental.pallas import tpu as pltpu

F32 = jnp.float32
BF16 = jnp.bfloat16

GRID_W = 64
HEAD_DIM = 64
WINDOW = 128
NA_ROWS = 8
NA_COLS = 16
ROPE_THETA = 10000.0
EPS = 1e-6
NEG = -1e30

V7X_VMEM_BYTES = 64 * 1024 * 1024
LANES = 128


def _vmem(nbytes):
    return pltpu.CompilerParams(vmem_limit_bytes=int(min(nbytes, V7X_VMEM_BYTES - 8 * 1024 * 1024)))


def _const_spec(shape):
    n = len(shape)
    return pl.BlockSpec(shape, lambda *_: (0,) * n, pipeline_mode=pl.Buffered(1))


def _dot(a, b):
    return jnp.dot(a, b, preferred_element_type=F32)


def _dot_nt(a, b):
    return lax.dot_general(a, b, (((1,), (1,)), ((), ())), preferred_element_type=F32)


def _ada_kernel(c_ref, w_ref, b_ref, o_ref):
    cc = c_ref[...]
    s = cc * jax.nn.sigmoid(cc)
    o_ref[0] = jnp.dot(s, w_ref[0], preferred_element_type=F32, precision=lax.Precision.HIGHEST) + b_ref[0]


def _ada_call(cc, ada_w, ada_b):
    depth, d, six_d = ada_w.shape
    tn = 1536
    return pl.pallas_call(
        _ada_kernel,
        grid=(depth, six_d // tn),
        in_specs=[
            pl.BlockSpec((16, d), lambda l, j: (0, 0)),
            pl.BlockSpec((1, d, tn), lambda l, j: (l, 0, j)),
            pl.BlockSpec((1, 1, tn), lambda l, j: (l, 0, j)),
        ],
        out_specs=pl.BlockSpec((1, 16, tn), lambda l, j: (l, 0, j)),
        out_shape=jax.ShapeDtypeStruct((depth, 16, six_d), F32),
        compiler_params=_vmem(32 * 1024 * 1024),
        name="ada_mod",
    )(cc, ada_w, ada_b.reshape(depth, 1, six_d))


def _modulated(x, g, shift, scale):
    ms = jnp.mean(x * x, axis=-1, keepdims=True)
    return (x * lax.rsqrt(ms + EPS) * g) * (1.0 + scale) + shift


def _proj_kernel(*refs, rope_chunks, n_chunk):
    if rope_chunks:
        h_ref, mod_ref, g_ref, w_ref, cos_ref, sa_ref, sb_ref, o_ref = refs
    else:
        h_ref, mod_ref, g_ref, w_ref, o_ref = refs
    a = _modulated(h_ref[0], g_ref[...], mod_ref[0, 0:1, :], mod_ref[0, 1:2, :]).astype(BF16)
    nout = o_ref.shape[-1]
    for c0 in range(0, nout, n_chunk):
        p = _dot(a, w_ref[:, c0:c0 + n_chunk])
        for j in range(n_chunk // LANES):
            lo = c0 + j * LANES
            xc = p[:, j * LANES:(j + 1) * LANES]
            if lo < rope_chunks * LANES:
                xc = (xc * cos_ref[...] + pltpu.roll(xc, LANES - 16, 1) * sa_ref[...]
                      + pltpu.roll(xc, 16, 1) * sb_ref[...])
            o_ref[0, :, lo:lo + LANES] = xc.astype(BF16)


def _project(h, mod9, mod_row, g, w, rope, tm, name):
    b, n, d = h.shape
    nout = w.shape[1]
    rope_chunks = 0 if rope is None else rope[0]
    n_chunk = 512
    if mod_row is None:
        mod_map = lambda bi, i: (bi, 0, 0)
    else:
        mod_map = lambda bi, i: (mod_row, 0, 0)
    in_specs = [
        pl.BlockSpec((1, tm, d), lambda bi, i: (bi, i, 0)),
        pl.BlockSpec((1, 6, d), mod_map),
        _const_spec((1, d)),
        _const_spec((d, nout)),
    ]
    args = [h, mod9, g.reshape(1, d), w]
    if rope_chunks:
        in_specs += [pl.BlockSpec((tm, LANES), lambda bi, i: (i, 0))] * 3
        args += list(rope[1:])
    return pl.pallas_call(
        functools.partial(_proj_kernel, rope_chunks=rope_chunks, n_chunk=n_chunk),
        grid=(b, n // tm),
        in_specs=in_specs,
        out_specs=pl.BlockSpec((1, tm, nout), lambda bi, i: (bi, i, 0)),
        out_shape=jax.ShapeDtypeStruct((b, n, nout), BF16),
        compiler_params=_vmem(48 * 1024 * 1024),
        name=name,
    )(*args)


def _softmax_parts(scores, extra_col=None):
    m = functools.reduce(jnp.maximum, [jnp.max(s, axis=-1, keepdims=True) for s in scores])
    if extra_col is not None:
        m = jnp.maximum(m, extra_col)
    ps = [jnp.exp(s - m) for s in scores]
    l = functools.reduce(jnp.add, [jnp.sum(p, axis=-1, keepdims=True) for p in ps])
    if extra_col is not None:
        l = l + jnp.exp(extra_col - m)
    return ps, l


def _group_sink_col(sink_ref, base, rows_per_head):
    rows = lax.broadcasted_iota(jnp.int32, (4 * rows_per_head, 1), 0) // rows_per_head
    col = jnp.full((4 * rows_per_head, 1), sink_ref[base + 3], F32)
    for g in range(3):
        col = jnp.where(rows == g, sink_ref[base + g], col)
    return col


def _win_kernel(sink_ref, q_ref, k_ref, v_ref, ck_ref, cv_ref, o_ref, *, blk):
    n = k_ref.shape[2]
    band = 3 * blk
    sink_col = _group_sink_col(sink_ref, pl.program_id(1) * 4, blk)
    rel = (lax.broadcasted_iota(jnp.int32, (4 * blk, band), 0) % blk
           - lax.broadcasted_iota(jnp.int32, (4 * blk, band), 1))
    ck = ck_ref[0, 0]
    cv = cv_ref[0, 0]

    def body(qb, carry):
        q0 = pl.multiple_of(qb * blk, blk)
        ws = pl.multiple_of(jnp.clip(q0 - blk, 0, n - band), blk)
        q = q_ref[0, :, pl.ds(q0, blk), :].reshape(4 * blk, HEAD_DIM)
        k = k_ref[0, 0, pl.ds(ws, band), :]
        v = v_ref[0, 0, pl.ds(ws, band), :]
        dist = rel + (q0 - ws)
        s = jnp.where(jnp.abs(dist) <= WINDOW, _dot_nt(q, k), NEG)
        sc = _dot_nt(q, ck)
        (p, pc), l = _softmax_parts([s, sc], sink_col)
        o = (_dot(p.astype(BF16), v) + _dot(pc.astype(BF16), cv)) / l
        o_ref[0, :, pl.ds(q0, blk), :] = o.reshape(4, blk, HEAD_DIM).astype(BF16)
        return carry

    lax.fori_loop(0, n // blk, body, 0)


def _window_attn(sink, q, k, v, ck, cv):
    b, hq, n, d = q.shape
    hkv = k.shape[1]
    c = ck.shape[2]
    return pl.pallas_call(
        functools.partial(_win_kernel, blk=WINDOW),
        grid=(b, hkv),
        in_specs=[
            pl.BlockSpec(memory_space=pltpu.SMEM),
            pl.BlockSpec((1, 4, n, d), lambda bi, h: (bi, h, 0, 0)),
            pl.BlockSpec((1, 1, n, d), lambda bi, h: (bi, h, 0, 0)),
            pl.BlockSpec((1, 1, n, d), lambda bi, h: (bi, h, 0, 0)),
            pl.BlockSpec((1, 1, c, d), lambda bi, h: (bi, h, 0, 0)),
            pl.BlockSpec((1, 1, c, d), lambda bi, h: (bi, h, 0, 0)),
        ],
        out_specs=pl.BlockSpec((1, 4, n, d), lambda bi, h: (bi, h, 0, 0)),
        out_shape=jax.ShapeDtypeStruct((b, hq, n, d), BF16),
        compiler_params=_vmem(40 * 1024 * 1024),
        name="window_attn",
    )(sink, q, k, v, ck, cv)


def _na_bias_kernel(rpb_ref, o_ref):
    h = pl.program_id(0)
    n_roff = 2 * NA_ROWS - 1
    n_coff = 2 * NA_COLS - 1
    cq = lax.broadcasted_iota(jnp.int32, (GRID_W, 2 * GRID_W), 0)
    lane = lax.broadcasted_iota(jnp.int32, (GRID_W, 2 * GRID_W), 1)
    ck = lane % GRID_W
    cstart = jnp.clip(cq - NA_COLS // 2, 0, GRID_W - NA_COLS)
    valid = (ck >= cstart) & (ck < cstart + NA_COLS)
    diff = ck - cq + (NA_COLS - 1)
    tiles = []
    for roff in range(n_roff):
        t = jnp.full((GRID_W, 2 * GRID_W), NEG, F32)
        for o in range(n_coff):
            t = jnp.where(valid & (diff == o), rpb_ref[(h * n_roff + roff) * n_coff + o], t)
        tiles.append(t)
    for ro in range(n_roff - 1):
        o_ref[0, ro] = jnp.where(lane < GRID_W, tiles[ro], tiles[ro + 1])


def _na_bias(rpb):
    hq, n_roff, n_coff = rpb.shape
    return pl.pallas_call(
        _na_bias_kernel,
        grid=(hq,),
        in_specs=[pl.BlockSpec(memory_space=pltpu.SMEM)],
        out_specs=pl.BlockSpec((1, n_roff - 1, GRID_W, 2 * GRID_W), lambda h: (h, 0, 0, 0)),
        out_shape=jax.ShapeDtypeStruct((hq, n_roff - 1, GRID_W, 2 * GRID_W), F32),
        name="na_bias",
    )(rpb.reshape(-1))


def _na_kernel(q_ref, k_ref, v_ref, ck_ref, cv_ref, bias_ref, o_ref):
    n = k_ref.shape[2]
    rows = n // GRID_W
    kr = min(NA_ROWS, rows)
    nkey = kr * GRID_W
    ck = ck_ref[0, 0]
    cv = cv_ref[0, 0]

    def body(r, carry):
        q0 = pl.multiple_of(r * GRID_W, GRID_W)
        rs = jnp.clip(r - kr // 2, 0, rows - kr)
        k0 = pl.multiple_of(rs * GRID_W, GRID_W)
        q = q_ref[0, :, pl.ds(q0, GRID_W), :].reshape(4 * GRID_W, HEAD_DIM)
        k = k_ref[0, 0, pl.ds(k0, nkey), :]
        v = v_ref[0, 0, pl.ds(k0, nkey), :]
        ro0 = rs - r + (NA_ROWS - 1)
        bias = jnp.concatenate(
            [jnp.concatenate([bias_ref[g, ro0 + 2 * i] for i in range(kr // 2)], axis=1) for g in range(4)], axis=0)
        s = _dot_nt(q, k) + bias
        sc = _dot_nt(q, ck)
        (p, pc), l = _softmax_parts([s, sc])
        o = (_dot(p.astype(BF16), v) + _dot(pc.astype(BF16), cv)) / l
        o_ref[0, :, pl.ds(q0, GRID_W), :] = o.reshape(4, GRID_W, HEAD_DIM).astype(BF16)
        return carry

    lax.fori_loop(0, rows, body, 0)


def _neighbourhood_attn(q, k, v, ck, cv, bias):
    b, hq, n, d = q.shape
    hkv = k.shape[1]
    c = ck.shape[2]
    return pl.pallas_call(
        _na_kernel,
        grid=(b, hkv),
        in_specs=[
            pl.BlockSpec((1, 4, n, d), lambda bi, h: (bi, h, 0, 0)),
            pl.BlockSpec((1, 1, n, d), lambda bi, h: (bi, h, 0, 0)),
            pl.BlockSpec((1, 1, n, d), lambda bi, h: (bi, h, 0, 0)),
            pl.BlockSpec((1, 1, c, d), lambda bi, h: (bi, h, 0, 0)),
            pl.BlockSpec((1, 1, c, d), lambda bi, h: (bi, h, 0, 0)),
            pl.BlockSpec((4,) + bias.shape[1:], lambda bi, h: (h, 0, 0, 0)),
        ],
        out_specs=pl.BlockSpec((1, 4, n, d), lambda bi, h: (bi, h, 0, 0)),
        out_shape=jax.ShapeDtypeStruct((b, hq, n, d), BF16),
        compiler_params=_vmem(40 * 1024 * 1024),
        name="neighbourhood_attn",
    )(q, k, v, ck, cv, bias)


def _ctx_kernel(sink_ref, q_ref, k_ref, v_ref, o_ref):
    c = k_ref.shape[2]
    sink_col = _group_sink_col(sink_ref, pl.program_id(1) * 4, c)
    q = q_ref[0].reshape(4 * c, HEAD_DIM)
    (p,), l = _softmax_parts([_dot_nt(q, k_ref[0, 0])], sink_col)
    o = _dot(p.astype(BF16), v_ref[0, 0]) / l
    o_ref[0] = o.reshape(4, c, HEAD_DIM).astype(BF16)


def _ctx_attn(sink, q, k, v):
    b, hq, c, d = q.shape
    hkv = k.shape[1]
    return pl.pallas_call(
        _ctx_kernel,
        grid=(b, hkv),
        in_specs=[
            pl.BlockSpec(memory_space=pltpu.SMEM),
            pl.BlockSpec((1, 4, c, d), lambda bi, h: (bi, h, 0, 0)),
            pl.BlockSpec((1, 1, c, d), lambda bi, h: (bi, h, 0, 0)),
            pl.BlockSpec((1, 1, c, d), lambda bi, h: (bi, h, 0, 0)),
        ],
        out_specs=pl.BlockSpec((1, 4, c, d), lambda bi, h: (bi, h, 0, 0)),
        out_shape=jax.ShapeDtypeStruct((b, hq, c, d), BF16),
        name="ctx_attn",
    )(sink, q, k, v)


def _diff_kernel(lq1_ref, lk1_ref, lq2_ref, lk2_ref, g_ref, qt_ref, kk_ref, vt_ref, o_ref,
                 qz_ref, m_ref, l_ref, acc1_ref, acc2_ref, *, lam_init):
    tq = qt_ref.shape[-1]
    n_kt, _, tk = vt_ref.shape[2:]
    qt = qt_ref[0, 0]
    zero = jnp.zeros((HEAD_DIM, tq), BF16)
    qz_ref[0:HEAD_DIM, 0:tq] = qt[0:HEAD_DIM]
    qz_ref[0:HEAD_DIM, tq:] = zero
    qz_ref[HEAD_DIM:, 0:tq] = zero
    qz_ref[HEAD_DIM:, tq:] = qt[HEAD_DIM:]
    m_ref[...] = jnp.full(m_ref.shape, NEG, F32)
    l_ref[...] = jnp.zeros(l_ref.shape, F32)
    acc1_ref[...] = jnp.zeros(acc1_ref.shape, F32)
    acc2_ref[...] = jnp.zeros(acc2_ref.shape, F32)

    def body(j, carry):
        k0 = pl.multiple_of(j * tk, tk)
        st = _dot(kk_ref[0, 0, pl.ds(k0, tk), :], qz_ref[...])
        m_old = m_ref[...]
        m_new = jnp.maximum(m_old, jnp.max(st, axis=0, keepdims=True))
        alpha = jnp.exp(m_old - m_new)
        p = jnp.exp(st - m_new)
        l_ref[...] = alpha * l_ref[...] + jnp.sum(p, axis=0, keepdims=True)
        m_ref[...] = m_new
        pb = p.astype(BF16)
        vt = vt_ref[0, 0, j]
        acc1_ref[...] = alpha[:, :tq] * acc1_ref[...] + _dot(vt, pb[:, :tq])
        acc2_ref[...] = alpha[:, tq:] * acc2_ref[...] + _dot(vt, pb[:, tq:])
        return carry

    lax.fori_loop(0, n_kt, body, 0)

    lam = (jnp.exp(jnp.sum(lq1_ref[...] * lk1_ref[...], axis=-1, keepdims=True))
           - jnp.exp(jnp.sum(lq2_ref[...] * lk2_ref[...], axis=-1, keepdims=True)) + lam_init)
    l = l_ref[...]
    o = acc1_ref[...] / l[:, :tq] - lam * (acc2_ref[...] / l[:, tq:])
    ms = jnp.mean(o * o, axis=0, keepdims=True)
    o_ref[0, 0] = ((o * lax.rsqrt(ms + EPS) * g_ref[...]) * (1.0 - lam_init)).astype(BF16)


def _diff_attn(lam_vecs, sub_g, qt, kk, vt, lam_init, tq):
    b, h, dv, n = qt.shape
    nk = kk.shape[2]
    n_kt, _, tk = vt.shape[2:]
    vec_spec = pl.BlockSpec((1, HEAD_DIM), lambda bi, hi, i: (0, 0))
    return pl.pallas_call(
        functools.partial(_diff_kernel, lam_init=lam_init),
        grid=(b, h, n // tq),
        in_specs=[vec_spec] * 4 + [
            pl.BlockSpec((dv, 1), lambda bi, hi, i: (0, 0)),
            pl.BlockSpec((1, 1, dv, tq), lambda bi, hi, i: (bi, hi, 0, i)),
            pl.BlockSpec((1, 1, nk, dv), lambda bi, hi, i: (bi, hi, 0, 0)),
            pl.BlockSpec((1, 1, n_kt, dv, tk), lambda bi, hi, i: (bi, hi, 0, 0, 0)),
        ],
        out_specs=pl.BlockSpec((1, 1, dv, tq), lambda bi, hi, i: (bi, hi, 0, i)),
        out_shape=jax.ShapeDtypeStruct((b, h, dv, n), BF16),
        scratch_shapes=[
            pltpu.VMEM((dv, 2 * tq), BF16),
            pltpu.VMEM((1, 2 * tq), F32),
            pltpu.VMEM((1, 2 * tq), F32),
            pltpu.VMEM((dv, tq), F32),
            pltpu.VMEM((dv, tq), F32),
        ],
        compiler_params=_vmem(32 * 1024 * 1024),
        name="diff_attn",
    )(*lam_vecs, sub_g.reshape(dv, 1), qt, kk, vt)


def _mlp_kernel(*refs, final, ff_chunk):
    if final:
        y_ref, h_ref, mod_ref, g_ref, wo_ref, w1_ref, w2_ref, fg_ref, o_ref = refs
    else:
        y_ref, h_ref, mod_ref, g_ref, wo_ref, w1_ref, w2_ref, o_ref = refs
    hx = h_ref[0] + mod_ref[0, 2:3, :] * _dot(y_ref[0], wo_ref[...])
    a = _modulated(hx, g_ref[...], mod_ref[0, 3:4, :], mod_ref[0, 4:5, :]).astype(BF16)
    acc = jnp.zeros(hx.shape, F32)
    for c0 in range(0, w1_ref.shape[1], ff_chunk):
        u = jnp.maximum(_dot(a, w1_ref[:, c0:c0 + ff_chunk]), 0.0)
        acc = acc + _dot((u * u).astype(BF16), w2_ref[c0:c0 + ff_chunk, :])
    out = hx + mod_ref[0, 5:6, :] * acc
    if final:
        ms = jnp.mean(out * out, axis=-1, keepdims=True)
        out = out * lax.rsqrt(ms + EPS) * fg_ref[...]
    o_ref[0] = out


def _out_mlp(y, h, mod9, mod_row, g2, wo, w1, w2, final_g, tm, name):
    b, n, d = h.shape
    dff = w1.shape[1]
    final = final_g is not None
    if mod_row is None:
        mod_map = lambda bi, i: (bi, 0, 0)
    else:
        mod_map = lambda bi, i: (mod_row, 0, 0)
    tile = pl.BlockSpec((1, tm, d), lambda bi, i: (bi, i, 0))
    in_specs = [tile, tile, pl.BlockSpec((1, 6, d), mod_map), _const_spec((1, d)),
                _const_spec((d, d)), _const_spec((d, dff)), _const_spec((dff, d))]
    args = [y, h, mod9, g2.reshape(1, d), wo, w1, w2]
    if final:
        in_specs.append(_const_spec((1, d)))
        args.append(final_g.reshape(1, d))
    return pl.pallas_call(
        functools.partial(_mlp_kernel, final=final, ff_chunk=1024),
        grid=(b, n // tm),
        in_specs=in_specs,
        out_specs=tile,
        out_shape=jax.ShapeDtypeStruct((b, n, d), F32),
        compiler_params=_vmem(56 * 1024 * 1024),
        name=name,
    )(*args)


def _rope_tables(n):
    t = jnp.arange(n, dtype=jnp.int32)
    row = (t // GRID_W).astype(F32)
    col = (t % GRID_W).astype(F32)
    quarter = HEAD_DIM // 4
    inv_freq = ROPE_THETA ** (-jnp.arange(quarter, dtype=F32) / quarter)
    ar = row[:, None] * inv_freq[None, :]
    ac = col[:, None] * inv_freq[None, :]
    ang = jnp.concatenate([ar, ar, ac, ac], axis=-1)
    cos, sin = jnp.cos(ang), jnp.sin(ang)
    even = (jnp.arange(HEAD_DIM) // quarter) % 2 == 0
    rep = LANES // HEAD_DIM
    cos = jnp.tile(cos, (1, rep))
    sin_a = jnp.tile(jnp.where(even, -sin, 0.0), (1, rep))
    sin_b = jnp.tile(jnp.where(even, 0.0, sin), (1, rep))
    return cos, sin_a, sin_b


def _heads(p, lo, hi, width=HEAD_DIM):
    b, n, _ = p.shape
    return p[:, :, lo:hi].reshape(b, n, (hi - lo) // width, width).transpose(0, 2, 1, 3)


def _unheads(y):
    b, h, n, d = y.shape
    return y.transpose(0, 2, 1, 3).reshape(b, n, h * d)


def kernel(x, c, ctx, c_ctx, ada_w, ada_b, norm1_g, norm2_g, even_w_in, even_w_out, a_sink, b_rpb,
           odd_w_in, odd_w_out, lam_q1, lam_k1, lam_q2, lam_k2, subln_g, mlp_w1, mlp_w2, final_g):
    b, n, d = x.shape
    n_ctx = ctx.shape[1]
    assert ada_w.shape[0] == 2 and d == 1024 and n % 512 == 0
    scale = HEAD_DIM ** -0.5
    a_qw, a_kvw = 8 * HEAD_DIM, 2 * HEAD_DIM
    b_q0 = a_qw + 2 * a_kvw

    cc = jnp.zeros((16, d), F32).at[:b].set(c).at[b].set(c_ctx)
    mod = _ada_call(cc, ada_w, ada_b)[:, :b + 1].reshape(2, b + 1, 6, d)
    rope = _rope_tables(n)

    cols0 = jnp.arange(even_w_in.shape[2])
    qcols0 = (cols0 < a_qw) | ((cols0 >= b_q0) & (cols0 < b_q0 + a_qw))
    w_in0 = (even_w_in[0] * jnp.where(qcols0, scale, 1.0)).astype(BF16)
    cols1 = jnp.arange(odd_w_in.shape[2])
    w_in1 = (odd_w_in[0] * jnp.where(cols1 < 1024, scale, 1.0)).astype(BF16)
    w_out0, w_out1 = even_w_out[0].astype(BF16), odd_w_out[0].astype(BF16)
    w1, w2 = mlp_w1.astype(BF16), mlp_w2.astype(BF16)

    px = _project(x, mod[0], None, norm1_g[0], w_in0, (5,) + rope, 512, "proj0_x")
    pc = _project(ctx, mod[0], b, norm1_g[0], w_in0, None, n_ctx, "proj0_ctx")
    ak0, av0, bq0, bk0, bv0 = a_qw, a_qw + a_kvw, b_q0, b_q0 + a_qw, b_q0 + a_qw + a_kvw
    aq, ak, av = _heads(px, 0, ak0), _heads(px, ak0, av0), _heads(px, av0, bq0)
    bq, bk, bv = _heads(px, bq0, bk0), _heads(px, bk0, bv0), _heads(px, bv0, bv0 + a_kvw)
    caq, cak, cav = _heads(pc, 0, ak0), _heads(pc, ak0, av0), _heads(pc, av0, bq0)
    cbq, cbk, cbv = _heads(pc, bq0, bk0), _heads(pc, bk0, bv0), _heads(pc, bv0, bv0 + a_kvw)
    sink = a_sink[0].astype(F32)
    ya = _window_attn(sink, aq, ak, av, cak, cav)
    yb = _neighbourhood_attn(bq, bk, bv, cbk, cbv, _na_bias(b_rpb[0].astype(F32)))
    yx = _unheads(jnp.concatenate([ya, yb], axis=1))
    sink_ab = jnp.concatenate([sink, jnp.full_like(sink, NEG)])
    yc = _unheads(_ctx_attn(sink_ab, jnp.concatenate([caq, cbq], axis=1), jnp.concatenate([cak, cbk], axis=1),
                            jnp.concatenate([cav, cbv], axis=1)))
    hx = _out_mlp(yx, x, mod[0], None, norm2_g[0], w_out0, w1[0], w2[0], None, 512, "mlp0_x")
    hc = _out_mlp(yc, ctx, mod[0], b, norm2_g[0], w_out0, w1[0], w2[0], None, n_ctx, "mlp0_ctx")

    lam_init = 0.8 - 0.6 * math.exp(-0.3 * 1)
    px = _project(hx, mod[1], None, norm1_g[1], w_in1, (16,) + rope, 512, "proj1_x")
    pc = _project(hc, mod[1], b, norm1_g[1], w_in1, None, n_ctx, "proj1_ctx")
    dv = 2 * HEAD_DIM
    tk = 256
    qt = px[:, :, :1024].reshape(b, n, 8, dv).transpose(0, 2, 3, 1)
    kk = jnp.concatenate([_heads(px, 1024, 2048, dv), _heads(pc, 1024, 2048, dv)], axis=2)
    vv = jnp.concatenate([px[:, :, 2048:], pc[:, :, 2048:]], axis=1)
    vt = vv.reshape(b, (n + n_ctx) // tk, tk, 8, dv).transpose(0, 3, 1, 4, 2)
    lam_vecs = [v[0].reshape(1, HEAD_DIM).astype(F32) for v in (lam_q1, lam_k1, lam_q2, lam_k2)]
    ot = _diff_attn(lam_vecs, subln_g[0].astype(F32), qt, kk, vt, lam_init, 256)
    yx = ot.transpose(0, 3, 1, 2).reshape(b, n, 8 * dv)
    return _out_mlp(yx, hx, mod[1], None, norm2_g[1], w_out1, w1[1], w2[1], final_g, 512, "mlp1_x")
```

```python
import functools
import math

import jax
import jax.numpy as jnp
from jax import lax
from jax.experimental import pallas as pl
from jax.experimental.pallas import tpu as pltpu

F32 = jnp.float32
BF16 = jnp.bfloat16

GRID_W = 64
HEAD_DIM = 64
WINDOW = 128
NA_ROWS = 8
NA_COLS = 16
ROPE_THETA = 10000.0
EPS = 1e-6
NEG = -1e30

V7X_VMEM_BYTES = 64 * 1024 * 1024
LANES = 128


def _vmem(nbytes):
    return pltpu.CompilerParams(vmem_limit_bytes=int(min(nbytes, V7X_VMEM_BYTES - 8 * 1024 * 1024)))


def _const_spec(shape):
    n = len(shape)
    return pl.BlockSpec(shape, lambda *_: (0,) * n, pipeline_mode=pl.Buffered(1))


def _dot(a, b):
    return jnp.dot(a, b, preferred_element_type=F32)


def _dot_nt(a, b):
    return lax.dot_general(a, b, (((1,), (1,)), ((), ())), preferred_element_type=F32)


def _ada_kernel(c_ref, w_ref, b_ref, o_ref):
    cc = c_ref[...]
    s = cc * jax.nn.sigmoid(cc)
    o_ref[0] = jnp.dot(s, w_ref[0], preferred_element_type=F32, precision=lax.Precision.HIGHEST) + b_ref[0]


def _ada_call(cc, ada_w, ada_b):
    depth, d, six_d = ada_w.shape
    tn = 1536
    return pl.pallas_call(
        _ada_kernel,
        grid=(depth, six_d // tn),
        in_specs=[
            pl.BlockSpec((16, d), lambda l, j: (0, 0)),
            pl.BlockSpec((1, d, tn), lambda l, j: (l, 0, j)),
            pl.BlockSpec((1, 1, tn), lambda l, j: (l, 0, j)),
        ],
        out_specs=pl.BlockSpec((1, 16, tn), lambda l, j: (l, 0, j)),
        out_shape=jax.ShapeDtypeStruct((depth, 16, six_d), F32),
        compiler_params=_vmem(32 * 1024 * 1024),
        name="ada_mod",
    )(cc, ada_w, ada_b.reshape(depth, 1, six_d))


def _modulated(x, g, shift, scale):
    ms = jnp.mean(x * x, axis=-1, keepdims=True)
    return (x * lax.rsqrt(ms + EPS) * g) * (1.0 + scale) + shift


def _proj_kernel(*refs, rope_chunks, n_chunk):
    if rope_chunks:
        h_ref, mod_ref, g_ref, w_ref, cos_ref, sa_ref, sb_ref, o_ref = refs
    else:
        h_ref, mod_ref, g_ref, w_ref, o_ref = refs
    a = _modulated(h_ref[0], g_ref[...], mod_ref[0, 0:1, :], mod_ref[0, 1:2, :]).astype(BF16)
    nout = o_ref.shape[-1]
    for c0 in range(0, nout, n_chunk):
        p = _dot(a, w_ref[:, c0:c0 + n_chunk])
        for j in range(n_chunk // LANES):
            lo = c0 + j * LANES
            xc = p[:, j * LANES:(j + 1) * LANES]
            if lo < rope_chunks * LANES:
                xc = (xc * cos_ref[...] + pltpu.roll(xc, LANES - 16, 1) * sa_ref[...]
                      + pltpu.roll(xc, 16, 1) * sb_ref[...])
            o_ref[0, :, lo:lo + LANES] = xc.astype(BF16)


def _project(h, mod9, mod_row, g, w, rope, tm, name):
    b, n, d = h.shape
    nout = w.shape[1]
    rope_chunks = 0 if rope is None else rope[0]
    n_chunk = 512
    if mod_row is None:
        mod_map = lambda bi, i: (bi, 0, 0)
    else:
        mod_map = lambda bi, i: (mod_row, 0, 0)
    in_specs = [
        pl.BlockSpec((1, tm, d), lambda bi, i: (bi, i, 0)),
        pl.BlockSpec((1, 6, d), mod_map),
        _const_spec((1, d)),
        _const_spec((d, nout)),
    ]
    args = [h, mod9, g.reshape(1, d), w]
    if rope_chunks:
        in_specs += [pl.BlockSpec((tm, LANES), lambda bi, i: (i, 0))] * 3
        args += list(rope[1:])
    return pl.pallas_call(
        functools.partial(_proj_kernel, rope_chunks=rope_chunks, n_chunk=n_chunk),
        grid=(b, n // tm),
        in_specs=in_specs,
        out_specs=pl.BlockSpec((1, tm, nout), lambda bi, i: (bi, i, 0)),
        out_shape=jax.ShapeDtypeStruct((b, n, nout), BF16),
        compiler_params=_vmem(48 * 1024 * 1024),
        name=name,
    )(*args)


def _softmax_parts(scores, extra_col=None):
    m = functools.reduce(jnp.maximum, [jnp.max(s, axis=-1, keepdims=True) for s in scores])
    if extra_col is not None:
        m = jnp.maximum(m, extra_col)
    ps = [jnp.exp(s - m) for s in scores]
    l = functools.reduce(jnp.add, [jnp.sum(p, axis=-1, keepdims=True) for p in ps])
    if extra_col is not None:
        l = l + jnp.exp(extra_col - m)
    return ps, l


def _group_sink_col(sink_ref, base, rows_per_head):
    rows = lax.broadcasted_iota(jnp.int32, (4 * rows_per_head, 1), 0) // rows_per_head
    col = jnp.full((4 * rows_per_head, 1), sink_ref[base + 3], F32)
    for g in range(3):
        col = jnp.where(rows == g, sink_ref[base + g], col)
    return col


def _win_kernel(sink_ref, q_ref, k_ref, v_ref, ck_ref, cv_ref, o_ref, *, blk):
    n = k_ref.shape[2]
    band = 3 * blk
    sink_col = _group_sink_col(sink_ref, pl.program_id(1) * 4, blk)
    rel = (lax.broadcasted_iota(jnp.int32, (4 * blk, band), 0) % blk
           - lax.broadcasted_iota(jnp.int32, (4 * blk, band), 1))
    ck = ck_ref[0, 0]
    cv = cv_ref[0, 0]

    def body(qb, carry):
        q0 = pl.multiple_of(qb * blk, blk)
        ws = pl.multiple_of(jnp.clip(q0 - blk, 0, n - band), blk)
        q = q_ref[0, :, pl.ds(q0, blk), :].reshape(4 * blk, HEAD_DIM)
        k = k_ref[0, 0, pl.ds(ws, band), :]
        v = v_ref[0, 0, pl.ds(ws, band), :]
        dist = rel + (q0 - ws)
        s = jnp.where(jnp.abs(dist) <= WINDOW, _dot_nt(q, k), NEG)
        sc = _dot_nt(q, ck)
        (p, pc), l = _softmax_parts([s, sc], sink_col)
        o = (_dot(p.astype(BF16), v) + _dot(pc.astype(BF16), cv)) / l
        o_ref[0, :, pl.ds(q0, blk), :] = o.reshape(4, blk, HEAD_DIM).astype(BF16)
        return carry

    lax.fori_loop(0, n // blk, body, 0)


def _window_attn(sink, q, k, v, ck, cv):
    b, hq, n, d = q.shape
    hkv = k.shape[1]
    c = ck.shape[2]
    return pl.pallas_call(
        functools.partial(_win_kernel, blk=WINDOW),
        grid=(b, hkv),
        in_specs=[
            pl.BlockSpec(memory_space=pltpu.SMEM),
            pl.BlockSpec((1, 4, n, d), lambda bi, h: (bi, h, 0, 0)),
            pl.BlockSpec((1, 1, n, d), lambda bi, h: (bi, h, 0, 0)),
            pl.BlockSpec((1, 1, n, d), lambda bi, h: (bi, h, 0, 0)),
            pl.BlockSpec((1, 1, c, d), lambda bi, h: (bi, h, 0, 0)),
            pl.BlockSpec((1, 1, c, d), lambda bi, h: (bi, h, 0, 0)),
        ],
        out_specs=pl.BlockSpec((1, 4, n, d), lambda bi, h: (bi, h, 0, 0)),
        out_shape=jax.ShapeDtypeStruct((b, hq, n, d), BF16),
        compiler_params=_vmem(40 * 1024 * 1024),
        name="window_attn",
    )(sink, q, k, v, ck, cv)


def _na_bias_kernel(rpb_ref, o_ref):
    h = pl.program_id(0)
    n_roff = 2 * NA_ROWS - 1
    n_coff = 2 * NA_COLS - 1
    cq = lax.broadcasted_iota(jnp.int32, (GRID_W, 2 * GRID_W), 0)
    lane = lax.broadcasted_iota(jnp.int32, (GRID_W, 2 * GRID_W), 1)
    ck = lane % GRID_W
    cstart = jnp.clip(cq - NA_COLS // 2, 0, GRID_W - NA_COLS)
    valid = (ck >= cstart) & (ck < cstart + NA_COLS)
    diff = ck - cq + (NA_COLS - 1)
    tiles = []
    for roff in range(n_roff):
        t = jnp.full((GRID_W, 2 * GRID_W), NEG, F32)
        for o in range(n_coff):
            t = jnp.where(valid & (diff == o), rpb_ref[(h * n_roff + roff) * n_coff + o], t)
        tiles.append(t)
    for ro in range(n_roff - 1):
        o_ref[0, ro] = jnp.where(lane < GRID_W, tiles[ro], tiles[ro + 1])


def _na_bias(rpb):
    hq, n_roff, n_coff = rpb.shape
    return pl.pallas_call(
        _na_bias_kernel,
        grid=(hq,),
        in_specs=[pl.BlockSpec(memory_space=pltpu.SMEM)],
        out_specs=pl.BlockSpec((1, n_roff - 1, GRID_W, 2 * GRID_W), lambda h: (h, 0, 0, 0)),
        out_shape=jax.ShapeDtypeStruct((hq, n_roff - 1, GRID_W, 2 * GRID_W), F32),
        name="na_bias",
    )(rpb.reshape(-1))


def _na_kernel(q_ref, k_ref, v_ref, ck_ref, cv_ref, bias_ref, o_ref):
    n = k_ref.shape[2]
    rows = n // GRID_W
    kr = min(NA_ROWS, rows)
    nkey = kr * GRID_W
    ck = ck_ref[0, 0]
    cv = cv_ref[0, 0]

    def body(r, carry):
        q0 = pl.multiple_of(r * GRID_W, GRID_W)
        rs = jnp.clip(r - kr // 2, 0, rows - kr)
        k0 = pl.multiple_of(rs * GRID_W, GRID_W)
        q = q_ref[0, :, pl.ds(q0, GRID_W), :].reshape(4 * GRID_W, HEAD_DIM)
        k = k_ref[0, 0, pl.ds(k0, nkey), :]
        v = v_ref[0, 0, pl.ds(k0, nkey), :]
        ro0 = rs - r + (NA_ROWS - 1)
        bias = jnp.concatenate(
            [jnp.concatenate([bias_ref[g, ro0 + 2 * i] for i in range(kr // 2)], axis=1) for g in range(4)], axis=0)
        s = _dot_nt(q, k) + bias
        sc = _dot_nt(q, ck)
        (p, pc), l = _softmax_parts([s, sc])
        o = (_dot(p.astype(BF16), v) + _dot(pc.astype(BF16), cv)) / l
        o_ref[0, :, pl.ds(q0, GRID_W), :] = o.reshape(4, GRID_W, HEAD_DIM).astype(BF16)
        return carry

    lax.fori_loop(0, rows, body, 0)


def _neighbourhood_attn(q, k, v, ck, cv, bias):
    b, hq, n, d = q.shape
    hkv = k.shape[1]
    c = ck.shape[2]
    return pl.pallas_call(
        _na_kernel,
        grid=(b, hkv),
        in_specs=[
            pl.BlockSpec((1, 4, n, d), lambda bi, h: (bi, h, 0, 0)),
            pl.BlockSpec((1, 1, n, d), lambda bi, h: (bi, h, 0, 0)),
            pl.BlockSpec((1, 1, n, d), lambda bi, h: (bi, h, 0, 0)),
            pl.BlockSpec((1, 1, c, d), lambda bi, h: (bi, h, 0, 0)),
            pl.BlockSpec((1, 1, c, d), lambda bi, h: (bi, h, 0, 0)),
            pl.BlockSpec((4,) + bias.shape[1:], lambda bi, h: (h, 0, 0, 0)),
        ],
        out_specs=pl.BlockSpec((1, 4, n, d), lambda bi, h: (bi, h, 0, 0)),
        out_shape=jax.ShapeDtypeStruct((b, hq, n, d), BF16),
        compiler_params=_vmem(40 * 1024 * 1024),
        name="neighbourhood_attn",
    )(q, k, v, ck, cv, bias)


def _ctx_kernel(sink_ref, q_ref, k_ref, v_ref, o_ref):
    c = k_ref.shape[2]
    sink_col = _group_sink_col(sink_ref, pl.program_id(1) * 4, c)
    q = q_ref[0].reshape(4 * c, HEAD_DIM)
    (p,), l = _softmax_parts([_dot_nt(q, k_ref[0, 0])], sink_col)
    o = _dot(p.astype(BF16), v_ref[0, 0]) / l
    o_ref[0] = o.reshape(4, c, HEAD_DIM).astype(BF16)


def _ctx_attn(sink, q, k, v):
    b, hq, c, d = q.shape
    hkv = k.shape[1]
    return pl.pallas_call(
        _ctx_kernel,
        grid=(b, hkv),
        in_specs=[
            pl.BlockSpec(memory_space=pltpu.SMEM),
            pl.BlockSpec((1, 4, c, d), lambda bi, h: (bi, h, 0, 0)),
            pl.BlockSpec((1, 1, c, d), lambda bi, h: (bi, h, 0, 0)),
            pl.BlockSpec((1, 1, c, d), lambda bi, h: (bi, h, 0, 0)),
        ],
        out_specs=pl.BlockSpec((1, 4, c, d), lambda bi, h: (bi, h, 0, 0)),
        out_shape=jax.ShapeDtypeStruct((b, hq, c, d), BF16),
        name="ctx_attn",
    )(sink, q, k, v)


def _diff_kernel(lq1_ref, lk1_ref, lq2_ref, lk2_ref, g_ref, qt_ref, kk_ref, vt_ref, o_ref,
                 qz_ref, kmax_ref, m_ref, l_ref, acc1_ref, acc2_ref, *, lam_init):
    tq = qt_ref.shape[-1]
    n_kt, _, tk = vt_ref.shape[2:]
    half = lax.broadcasted_iota(jnp.int32, (tk, 2 * HEAD_DIM), 1) < HEAD_DIM

    @pl.when(pl.program_id(2) == 0)
    def _():
        def kbody(j, carry):
            kf = kk_ref[0, 0, pl.ds(pl.multiple_of(j * tk, tk), tk), :].astype(F32)
            sq = kf * kf
            n1 = jnp.sum(jnp.where(half, sq, 0.0), axis=1, keepdims=True)
            n2 = jnp.sum(jnp.where(half, 0.0, sq), axis=1, keepdims=True)
            return (jnp.maximum(carry[0], jnp.max(n1, axis=0, keepdims=True)),
                    jnp.maximum(carry[1], jnp.max(n2, axis=0, keepdims=True)))

        k1, k2 = lax.fori_loop(0, n_kt, kbody, (jnp.zeros((1, 1), F32), jnp.zeros((1, 1), F32)))
        kmax_ref[:, :tq] = jnp.broadcast_to(jnp.sqrt(k1), (1, tq))
        kmax_ref[:, tq:] = jnp.broadcast_to(jnp.sqrt(k2), (1, tq))

    qt = qt_ref[0, 0]
    zero = jnp.zeros((HEAD_DIM, tq), BF16)
    qz_ref[0:HEAD_DIM, 0:tq] = qt[0:HEAD_DIM]
    qz_ref[0:HEAD_DIM, tq:] = zero
    qz_ref[HEAD_DIM:, 0:tq] = zero
    qz_ref[HEAD_DIM:, tq:] = qt[HEAD_DIM:]
    qz = qz_ref[...]
    qf = qt.astype(F32)
    qsq = qf * qf
    qn = jnp.concatenate([jnp.sum(qsq[:HEAD_DIM], axis=0, keepdims=True),
                          jnp.sum(qsq[HEAD_DIM:], axis=0, keepdims=True)], axis=1)
    shift = jnp.sqrt(qn) * kmax_ref[...] * (1.0 + 2.0 ** -10)

    l8 = jnp.zeros((8, 2 * tq), F32)
    acc1 = jnp.zeros(acc1_ref.shape, F32)
    acc2 = jnp.zeros(acc2_ref.shape, F32)
    ahead = 2
    scores = [_dot(kk_ref[0, 0, j * tk:(j + 1) * tk, :], qz) for j in range(ahead)]
    for j in range(n_kt):
        if j + ahead < n_kt:
            scores.append(_dot(kk_ref[0, 0, (j + ahead) * tk:(j + ahead + 1) * tk, :], qz))
        p = jnp.exp2(scores[j] - shift)
        scores[j] = None
        l8 = l8 + jnp.sum(p.reshape(tk // 8, 8, 2 * tq), axis=0)
        pb = p.astype(BF16)
        vt = vt_ref[0, 0, j]
        acc1 = acc1 + _dot(vt, pb[:, :tq])
        acc2 = acc2 + _dot(vt, pb[:, tq:])
    l_fast = jnp.sum(l8, axis=0, keepdims=True)
    l_ref[...] = l_fast
    acc1_ref[...] = acc1
    acc2_ref[...] = acc2

    @pl.when(jnp.min(l_fast) < 2.0 ** -80)
    def _():
        m_ref[...] = jnp.full(m_ref.shape, NEG, F32)
        l_ref[...] = jnp.zeros(l_ref.shape, F32)
        acc1_ref[...] = jnp.zeros(acc1_ref.shape, F32)
        acc2_ref[...] = jnp.zeros(acc2_ref.shape, F32)

        def body(j, carry):
            k0 = pl.multiple_of(j * tk, tk)
            st = _dot(kk_ref[0, 0, pl.ds(k0, tk), :], qz_ref[...])
            m_old = m_ref[...]
            m_new = jnp.maximum(m_old, jnp.max(st, axis=0, keepdims=True))
            alpha = jnp.exp2(m_old - m_new)
            p = jnp.exp2(st - m_new)
            l_ref[...] = alpha * l_ref[...] + jnp.sum(p, axis=0, keepdims=True)
            m_ref[...] = m_new
            pb = p.astype(BF16)
            vt = vt_ref[0, 0, j]
            acc1_ref[...] = alpha[:, :tq] * acc1_ref[...] + _dot(vt, pb[:, :tq])
            acc2_ref[...] = alpha[:, tq:] * acc2_ref[...] + _dot(vt, pb[:, tq:])
            return carry

        lax.fori_loop(0, n_kt, body, 0)

    lam = (jnp.exp(jnp.sum(lq1_ref[...] * lk1_ref[...], axis=-1, keepdims=True))
           - jnp.exp(jnp.sum(lq2_ref[...] * lk2_ref[...], axis=-1, keepdims=True)) + lam_init)
    l = l_ref[...]
    o = acc1_ref[...] / l[:, :tq] - lam * (acc2_ref[...] / l[:, tq:])
    ms = jnp.mean(o * o, axis=0, keepdims=True)
    o_ref[0, 0] = ((o * lax.rsqrt(ms + EPS) * g_ref[...]) * (1.0 - lam_init)).astype(BF16)


def _diff_attn(lam_vecs, sub_g, qt, kk, vt, lam_init, tq):
    b, h, dv, n = qt.shape
    nk = kk.shape[2]
    n_kt, _, tk = vt.shape[2:]
    vec_spec = pl.BlockSpec((1, HEAD_DIM), lambda bi, hi, i: (0, 0))
    return pl.pallas_call(
        functools.partial(_diff_kernel, lam_init=lam_init),
        grid=(b, h, n // tq),
        in_specs=[vec_spec] * 4 + [
            pl.BlockSpec((dv, 1), lambda bi, hi, i: (0, 0)),
            pl.BlockSpec((1, 1, dv, tq), lambda bi, hi, i: (bi, hi, 0, i)),
            pl.BlockSpec((1, 1, nk, dv), lambda bi, hi, i: (bi, hi, 0, 0)),
            pl.BlockSpec((1, 1, n_kt, dv, tk), lambda bi, hi, i: (bi, hi, 0, 0, 0)),
        ],
        out_specs=pl.BlockSpec((1, 1, dv, tq), lambda bi, hi, i: (bi, hi, 0, i)),
        out_shape=jax.ShapeDtypeStruct((b, h, dv, n), BF16),
        scratch_shapes=[
            pltpu.VMEM((dv, 2 * tq), BF16),
            pltpu.VMEM((1, 2 * tq), F32),
            pltpu.VMEM((1, 2 * tq), F32),
            pltpu.VMEM((1, 2 * tq), F32),
            pltpu.VMEM((dv, tq), F32),
            pltpu.VMEM((dv, tq), F32),
        ],
        compiler_params=pltpu.CompilerParams(
            dimension_semantics=("arbitrary", "arbitrary", "arbitrary"), vmem_limit_bytes=32 * 1024 * 1024),
        name="diff_attn",
    )(*lam_vecs, sub_g.reshape(dv, 1), qt, kk, vt)


def _mlp_kernel(*refs, final, ff_chunk):
    if final:
        y_ref, h_ref, mod_ref, g_ref, wo_ref, w1_ref, w2_ref, fg_ref, o_ref = refs
    else:
        y_ref, h_ref, mod_ref, g_ref, wo_ref, w1_ref, w2_ref, o_ref = refs
    hx = h_ref[0] + mod_ref[0, 2:3, :] * _dot(y_ref[0], wo_ref[...])
    a = _modulated(hx, g_ref[...], mod_ref[0, 3:4, :], mod_ref[0, 4:5, :]).astype(BF16)
    acc = jnp.zeros(hx.shape, F32)
    for c0 in range(0, w1_ref.shape[1], ff_chunk):
        u = jnp.maximum(_dot(a, w1_ref[:, c0:c0 + ff_chunk]), 0.0)
        acc = acc + _dot((u * u).astype(BF16), w2_ref[c0:c0 + ff_chunk, :])
    out = hx + mod_ref[0, 5:6, :] * acc
    if final:
        ms = jnp.mean(out * out, axis=-1, keepdims=True)
        out = out * lax.rsqrt(ms + EPS) * fg_ref[...]
    o_ref[0] = out


def _out_mlp(y, h, mod9, mod_row, g2, wo, w1, w2, final_g, tm, name):
    b, n, d = h.shape
    dff = w1.shape[1]
    final = final_g is not None
    if mod_row is None:
        mod_map = lambda bi, i: (bi, 0, 0)
    else:
        mod_map = lambda bi, i: (mod_row, 0, 0)
    tile = pl.BlockSpec((1, tm, d), lambda bi, i: (bi, i, 0))
    in_specs = [tile, tile, pl.BlockSpec((1, 6, d), mod_map), _const_spec((1, d)),
                _const_spec((d, d)), _const_spec((d, dff)), _const_spec((dff, d))]
    args = [y, h, mod9, g2.reshape(1, d), wo, w1, w2]
    if final:
        in_specs.append(_const_spec((1, d)))
        args.append(final_g.reshape(1, d))
    return pl.pallas_call(
        functools.partial(_mlp_kernel, final=final, ff_chunk=1024),
        grid=(b, n // tm),
        in_specs=in_specs,
        out_specs=tile,
        out_shape=jax.ShapeDtypeStruct((b, n, d), F32),
        compiler_params=_vmem(56 * 1024 * 1024),
        name=name,
    )(*args)


def _rope_tables(n):
    t = jnp.arange(n, dtype=jnp.int32)
    row = (t // GRID_W).astype(F32)
    col = (t % GRID_W).astype(F32)
    quarter = HEAD_DIM // 4
    inv_freq = ROPE_THETA ** (-jnp.arange(quarter, dtype=F32) / quarter)
    ar = row[:, None] * inv_freq[None, :]
    ac = col[:, None] * inv_freq[None, :]
    ang = jnp.concatenate([ar, ar, ac, ac], axis=-1)
    cos, sin = jnp.cos(ang), jnp.sin(ang)
    even = (jnp.arange(HEAD_DIM) // quarter) % 2 == 0
    rep = LANES // HEAD_DIM
    cos = jnp.tile(cos, (1, rep))
    sin_a = jnp.tile(jnp.where(even, -sin, 0.0), (1, rep))
    sin_b = jnp.tile(jnp.where(even, 0.0, sin), (1, rep))
    return cos, sin_a, sin_b


def _heads(p, lo, hi, width=HEAD_DIM):
    b, n, _ = p.shape
    return p[:, :, lo:hi].reshape(b, n, (hi - lo) // width, width).transpose(0, 2, 1, 3)


def _unheads(y):
    b, h, n, d = y.shape
    return y.transpose(0, 2, 1, 3).reshape(b, n, h * d)


def kernel(x, c, ctx, c_ctx, ada_w, ada_b, norm1_g, norm2_g, even_w_in, even_w_out, a_sink, b_rpb,
           odd_w_in, odd_w_out, lam_q1, lam_k1, lam_q2, lam_k2, subln_g, mlp_w1, mlp_w2, final_g):
    b, n, d = x.shape
    n_ctx = ctx.shape[1]
    assert ada_w.shape[0] == 2 and d == 1024 and n % 512 == 0
    scale = HEAD_DIM ** -0.5
    a_qw, a_kvw = 8 * HEAD_DIM, 2 * HEAD_DIM
    b_q0 = a_qw + 2 * a_kvw

    cc = jnp.zeros((16, d), F32).at[:b].set(c).at[b].set(c_ctx)
    mod = _ada_call(cc, ada_w, ada_b)[:, :b + 1].reshape(2, b + 1, 6, d)
    rope = _rope_tables(n)

    cols0 = jnp.arange(even_w_in.shape[2])
    qcols0 = (cols0 < a_qw) | ((cols0 >= b_q0) & (cols0 < b_q0 + a_qw))
    w_in0 = (even_w_in[0] * jnp.where(qcols0, scale, 1.0)).astype(BF16)
    cols1 = jnp.arange(odd_w_in.shape[2])
    w_in1 = (odd_w_in[0] * jnp.where(cols1 < 1024, scale * math.log2(math.e), 1.0)).astype(BF16)
    w_out0, w_out1 = even_w_out[0].astype(BF16), odd_w_out[0].astype(BF16)
    w1, w2 = mlp_w1.astype(BF16), mlp_w2.astype(BF16)

    px = _project(x, mod[0], None, norm1_g[0], w_in0, (5,) + rope, 512, "proj0_x")
    pc = _project(ctx, mod[0], b, norm1_g[0], w_in0, None, n_ctx, "proj0_ctx")
    ak0, av0, bq0, bk0, bv0 = a_qw, a_qw + a_kvw, b_q0, b_q0 + a_qw, b_q0 + a_qw + a_kvw
    aq, ak, av = _heads(px, 0, ak0), _heads(px, ak0, av0), _heads(px, av0, bq0)
    bq, bk, bv = _heads(px, bq0, bk0), _heads(px, bk0, bv0), _heads(px, bv0, bv0 + a_kvw)
    caq, cak, cav = _heads(pc, 0, ak0), _heads(pc, ak0, av0), _heads(pc, av0, bq0)
    cbq, cbk, cbv = _heads(pc, bq0, bk0), _heads(pc, bk0, bv0), _heads(pc, bv0, bv0 + a_kvw)
    sink = a_sink[0].astype(F32)
    ya = _window_attn(sink, aq, ak, av, cak, cav)
    yb = _neighbourhood_attn(bq, bk, bv, cbk, cbv, _na_bias(b_rpb[0].astype(F32)))
    yx = _unheads(jnp.concatenate([ya, yb], axis=1))
    sink_ab = jnp.concatenate([sink, jnp.full_like(sink, NEG)])
    yc = _unheads(_ctx_attn(sink_ab, jnp.concatenate([caq, cbq], axis=1), jnp.concatenate([cak, cbk], axis=1),
                            jnp.concatenate([cav, cbv], axis=1)))
    hx = _out_mlp(yx, x, mod[0], None, norm2_g[0], w_out0, w1[0], w2[0], None, 512, "mlp0_x")
    hc = _out_mlp(yc, ctx, mod[0], b, norm2_g[0], w_out0, w1[0], w2[0], None, n_ctx, "mlp0_ctx")

    lam_init = 0.8 - 0.6 * math.exp(-0.3 * 1)
    px = _project(hx, mod[1], None, norm1_g[1], w_in1, (16,) + rope, 512, "proj1_x")
    pc = _project(hc, mod[1], b, norm1_g[1], w_in1, None, n_ctx, "proj1_ctx")
    dv = 2 * HEAD_DIM
    tk = 256
    qt = px[:, :, :1024].reshape(b, n, 8, dv).transpose(0, 2, 3, 1)
    kk = jnp.concatenate([_heads(px, 1024, 2048, dv), _heads(pc, 1024, 2048, dv)], axis=2)
    vv = jnp.concatenate([px[:, :, 2048:], pc[:, :, 2048:]], axis=1)
    vt = vv.reshape(b, (n + n_ctx) // tk, tk, 8, dv).transpose(0, 3, 1, 4, 2)
    lam_vecs = [v[0].reshape(1, HEAD_DIM).astype(F32) for v in (lam_q1, lam_k1, lam_q2, lam_k2)]
    ot = _diff_attn(lam_vecs, subln_g[0].astype(F32), qt, kk, vt, lam_init, 256)
    yx = ot.transpose(0, 3, 1, 2).reshape(b, n, 8 * dv)
    return _out_mlp(yx, hx, mod[1], None, norm2_g[1], w_out1, w1[1], w2[1], final_g, 512, "mlp1_x")
```

```python
import functools
import math

import jax
import jax.numpy as jnp
from jax import lax
from jax.experimental import pallas as pl
from jax.experimental.pallas import tpu as pltpu

F32 = jnp.float32
BF16 = jnp.bfloat16

GRID_W = 64
HEAD_DIM = 64
WINDOW = 128
NA_ROWS = 8
NA_COLS = 16
ROPE_THETA = 10000.0
EPS = 1e-6
NEG = -1e30

V7X_VMEM_BYTES = 64 * 1024 * 1024
LANES = 128


def _vmem(nbytes):
    return pltpu.CompilerParams(vmem_limit_bytes=int(min(nbytes, V7X_VMEM_BYTES - 8 * 1024 * 1024)))


def _const_spec(shape):
    n = len(shape)
    return pl.BlockSpec(shape, lambda *_: (0,) * n, pipeline_mode=pl.Buffered(1))


def _dot(a, b):
    return jnp.dot(a, b, preferred_element_type=F32)


def _dot_nt(a, b):
    return lax.dot_general(a, b, (((1,), (1,)), ((), ())), preferred_element_type=F32)


def _ada_kernel(c_ref, w_ref, b_ref, o_ref):
    cc = c_ref[...]
    s = cc * jax.nn.sigmoid(cc)
    o_ref[0] = jnp.dot(s, w_ref[0], preferred_element_type=F32, precision=lax.Precision.HIGHEST) + b_ref[0]


def _ada_call(cc, ada_w, ada_b):
    depth, d, six_d = ada_w.shape
    tn = 1536
    return pl.pallas_call(
        _ada_kernel,
        grid=(depth, six_d // tn),
        in_specs=[
            pl.BlockSpec((16, d), lambda l, j: (0, 0)),
            pl.BlockSpec((1, d, tn), lambda l, j: (l, 0, j)),
            pl.BlockSpec((1, 1, tn), lambda l, j: (l, 0, j)),
        ],
        out_specs=pl.BlockSpec((1, 16, tn), lambda l, j: (l, 0, j)),
        out_shape=jax.ShapeDtypeStruct((depth, 16, six_d), F32),
        compiler_params=_vmem(32 * 1024 * 1024),
        name="ada_mod",
    )(cc, ada_w, ada_b.reshape(depth, 1, six_d))


def _modulated(x, g, shift, scale):
    ms = jnp.mean(x * x, axis=-1, keepdims=True)
    return (x * lax.rsqrt(ms + EPS) * g) * (1.0 + scale) + shift


def _proj_kernel(*refs, rope_chunks, n_chunk):
    if rope_chunks:
        h_ref, mod_ref, g_ref, w_ref, cos_ref, sa_ref, sb_ref, o_ref = refs
    else:
        h_ref, mod_ref, g_ref, w_ref, o_ref = refs
    a = _modulated(h_ref[0], g_ref[...], mod_ref[0, 0:1, :], mod_ref[0, 1:2, :]).astype(BF16)
    nout = o_ref.shape[-1]
    for c0 in range(0, nout, n_chunk):
        p = _dot(a, w_ref[:, c0:c0 + n_chunk])
        for j in range(n_chunk // LANES):
            lo = c0 + j * LANES
            xc = p[:, j * LANES:(j + 1) * LANES]
            if lo < rope_chunks * LANES:
                xc = (xc * cos_ref[...] + pltpu.roll(xc, LANES - 16, 1) * sa_ref[...]
                      + pltpu.roll(xc, 16, 1) * sb_ref[...])
            o_ref[0, :, lo:lo + LANES] = xc.astype(BF16)


def _project(h, mod9, mod_row, g, w, rope, tm, name):
    b, n, d = h.shape
    nout = w.shape[1]
    rope_chunks = 0 if rope is None else rope[0]
    n_chunk = 512
    if mod_row is None:
        mod_map = lambda bi, i: (bi, 0, 0)
    else:
        mod_map = lambda bi, i: (mod_row, 0, 0)
    in_specs = [
        pl.BlockSpec((1, tm, d), lambda bi, i: (bi, i, 0)),
        pl.BlockSpec((1, 6, d), mod_map),
        _const_spec((1, d)),
        _const_spec((d, nout)),
    ]
    args = [h, mod9, g.reshape(1, d), w]
    if rope_chunks:
        in_specs += [pl.BlockSpec((tm, LANES), lambda bi, i: (i, 0))] * 3
        args += list(rope[1:])
    return pl.pallas_call(
        functools.partial(_proj_kernel, rope_chunks=rope_chunks, n_chunk=n_chunk),
        grid=(b, n // tm),
        in_specs=in_specs,
        out_specs=pl.BlockSpec((1, tm, nout), lambda bi, i: (bi, i, 0)),
        out_shape=jax.ShapeDtypeStruct((b, n, nout), BF16),
        compiler_params=_vmem(48 * 1024 * 1024),
        name=name,
    )(*args)


def _rope_rows(x, cos_t, sin_t):
    q = HEAD_DIM // 4
    rot = jnp.concatenate([x[q:2 * q], x[0:q], x[3 * q:4 * q], x[2 * q:3 * q]], axis=0)
    return x * cos_t + rot * sin_t


def _proj1_kernel(*refs, with_q, tk):
    if with_q:
        (h_ref, mod_ref, g_ref, wk_ref, wt_ref, cos_ref, sa_ref, sb_ref, cost_ref, sint_ref,
         qt_ref, kk_ref, vt_ref) = refs
    else:
        h_ref, mod_ref, g_ref, wk_ref, wt_ref, kk_ref, vt_ref = refs
    a = _modulated(h_ref[0], g_ref[...], mod_ref[0, 0:1, :], mod_ref[0, 1:2, :]).astype(BF16)
    tm = a.shape[0]
    dv = 2 * HEAD_DIM
    n_heads = kk_ref.shape[1]
    chunk = 4 * dv
    for c0 in range(0, n_heads * dv, chunk):
        p = _dot(a, wk_ref[:, c0:c0 + chunk])
        for j in range(chunk // dv):
            xc = p[:, j * dv:(j + 1) * dv]
            if with_q:
                xc = (xc * cos_ref[...] + pltpu.roll(xc, LANES - 16, 1) * sa_ref[...]
                      + pltpu.roll(xc, 16, 1) * sb_ref[...])
            kk_ref[0, c0 // dv + j] = xc.astype(BF16)
    n_q = n_heads * dv if with_q else 0
    for f0 in range(0, wt_ref.shape[0], chunk):
        xt = _dot_nt(wt_ref[f0:f0 + chunk, :], a)
        for j in range(chunk // dv):
            row = f0 + j * dv
            hd = xt[j * dv:(j + 1) * dv]
            if row < n_q:
                hd = jnp.concatenate([_rope_rows(hd[m * HEAD_DIM:(m + 1) * HEAD_DIM], cost_ref[...], sint_ref[...])
                                      for m in range(2)], axis=0)
                qt_ref[0, row // dv] = hd.astype(BF16)
            else:
                for t in range(tm // tk):
                    vt_ref[0, (row - n_q) // dv, t] = hd[:, t * tk:(t + 1) * tk].astype(BF16)


def _project1(h, mod9, mod_row, g, wk, wt, rope, rope_t, tm, tk, name):
    b, n, d = h.shape
    dv = 2 * HEAD_DIM
    n_heads = wk.shape[1] // dv
    with_q = rope is not None
    if mod_row is None:
        mod_map = lambda bi, i: (bi, 0, 0)
    else:
        mod_map = lambda bi, i: (mod_row, 0, 0)
    in_specs = [
        pl.BlockSpec((1, tm, d), lambda bi, i: (bi, i, 0)),
        pl.BlockSpec((1, 6, d), mod_map),
        _const_spec((1, d)),
        _const_spec(wk.shape),
        _const_spec(wt.shape),
    ]
    args = [h, mod9, g.reshape(1, d), wk, wt]
    out_specs = [
        pl.BlockSpec((1, n_heads, tm, dv), lambda bi, i: (bi, 0, i, 0)),
        pl.BlockSpec((1, n_heads, tm // tk, dv, tk), lambda bi, i: (bi, 0, i, 0, 0)),
    ]
    out_shape = [
        jax.ShapeDtypeStruct((b, n_heads, n, dv), BF16),
        jax.ShapeDtypeStruct((b, n_heads, n // tk, dv, tk), BF16),
    ]
    if with_q:
        in_specs += [pl.BlockSpec((tm, LANES), lambda bi, i: (i, 0))] * 3
        in_specs += [pl.BlockSpec((HEAD_DIM, tm), lambda bi, i: (0, i))] * 2
        args += list(rope) + list(rope_t)
        out_specs.insert(0, pl.BlockSpec((1, n_heads, dv, tm), lambda bi, i: (bi, 0, 0, i)))
        out_shape.insert(0, jax.ShapeDtypeStruct((b, n_heads, dv, n), BF16))
    return pl.pallas_call(
        functools.partial(_proj1_kernel, with_q=with_q, tk=tk),
        grid=(b, n // tm),
        in_specs=in_specs,
        out_specs=out_specs,
        out_shape=out_shape,
        compiler_params=_vmem(48 * 1024 * 1024),
        name=name,
    )(*args)


def _softmax_parts(scores, extra_col=None):
    m = functools.reduce(jnp.maximum, [jnp.max(s, axis=-1, keepdims=True) for s in scores])
    if extra_col is not None:
        m = jnp.maximum(m, extra_col)
    ps = [jnp.exp(s - m) for s in scores]
    l = functools.reduce(jnp.add, [jnp.sum(p, axis=-1, keepdims=True) for p in ps])
    if extra_col is not None:
        l = l + jnp.exp(extra_col - m)
    return ps, l


def _group_sink_col(sink_ref, base, rows_per_head):
    rows = lax.broadcasted_iota(jnp.int32, (4 * rows_per_head, 1), 0) // rows_per_head
    col = jnp.full((4 * rows_per_head, 1), sink_ref[base + 3], F32)
    for g in range(3):
        col = jnp.where(rows == g, sink_ref[base + g], col)
    return col


def _win_kernel(sink_ref, q_ref, k_ref, v_ref, ck_ref, cv_ref, o_ref, *, blk):
    n = k_ref.shape[2]
    band = 3 * blk
    sink_col = _group_sink_col(sink_ref, pl.program_id(1) * 4, blk)
    rel = (lax.broadcasted_iota(jnp.int32, (4 * blk, band), 0) % blk
           - lax.broadcasted_iota(jnp.int32, (4 * blk, band), 1))
    ck = ck_ref[0, 0]
    cv = cv_ref[0, 0]

    def body(qb, carry):
        q0 = pl.multiple_of(qb * blk, blk)
        ws = pl.multiple_of(jnp.clip(q0 - blk, 0, n - band), blk)
        q = q_ref[0, :, pl.ds(q0, blk), :].reshape(4 * blk, HEAD_DIM)
        k = k_ref[0, 0, pl.ds(ws, band), :]
        v = v_ref[0, 0, pl.ds(ws, band), :]
        dist = rel + (q0 - ws)
        s = jnp.where(jnp.abs(dist) <= WINDOW, _dot_nt(q, k), NEG)
        sc = _dot_nt(q, ck)
        (p, pc), l = _softmax_parts([s, sc], sink_col)
        o = (_dot(p.astype(BF16), v) + _dot(pc.astype(BF16), cv)) / l
        o_ref[0, :, pl.ds(q0, blk), :] = o.reshape(4, blk, HEAD_DIM).astype(BF16)
        return carry

    lax.fori_loop(0, n // blk, body, 0)


def _window_attn(sink, q, k, v, ck, cv):
    b, hq, n, d = q.shape
    hkv = k.shape[1]
    c = ck.shape[2]
    return pl.pallas_call(
        functools.partial(_win_kernel, blk=WINDOW),
        grid=(b, hkv),
        in_specs=[
            pl.BlockSpec(memory_space=pltpu.SMEM),
            pl.BlockSpec((1, 4, n, d), lambda bi, h: (bi, h, 0, 0)),
            pl.BlockSpec((1, 1, n, d), lambda bi, h: (bi, h, 0, 0)),
            pl.BlockSpec((1, 1, n, d), lambda bi, h: (bi, h, 0, 0)),
            pl.BlockSpec((1, 1, c, d), lambda bi, h: (bi, h, 0, 0)),
            pl.BlockSpec((1, 1, c, d), lambda bi, h: (bi, h, 0, 0)),
        ],
        out_specs=pl.BlockSpec((1, 4, n, d), lambda bi, h: (bi, h, 0, 0)),
        out_shape=jax.ShapeDtypeStruct((b, hq, n, d), BF16),
        compiler_params=_vmem(40 * 1024 * 1024),
        name="window_attn",
    )(sink, q, k, v, ck, cv)


def _na_bias_kernel(rpb_ref, o_ref):
    h = pl.program_id(0)
    n_roff = 2 * NA_ROWS - 1
    n_coff = 2 * NA_COLS - 1
    cq = lax.broadcasted_iota(jnp.int32, (GRID_W, 2 * GRID_W), 0)
    lane = lax.broadcasted_iota(jnp.int32, (GRID_W, 2 * GRID_W), 1)
    ck = lane % GRID_W
    cstart = jnp.clip(cq - NA_COLS // 2, 0, GRID_W - NA_COLS)
    valid = (ck >= cstart) & (ck < cstart + NA_COLS)
    diff = ck - cq + (NA_COLS - 1)
    tiles = []
    for roff in range(n_roff):
        t = jnp.full((GRID_W, 2 * GRID_W), NEG, F32)
        for o in range(n_coff):
            t = jnp.where(valid & (diff == o), rpb_ref[(h * n_roff + roff) * n_coff + o], t)
        tiles.append(t)
    for ro in range(n_roff - 1):
        o_ref[0, ro] = jnp.where(lane < GRID_W, tiles[ro], tiles[ro + 1])


def _na_bias(rpb):
    hq, n_roff, n_coff = rpb.shape
    return pl.pallas_call(
        _na_bias_kernel,
        grid=(hq,),
        in_specs=[pl.BlockSpec(memory_space=pltpu.SMEM)],
        out_specs=pl.BlockSpec((1, n_roff - 1, GRID_W, 2 * GRID_W), lambda h: (h, 0, 0, 0)),
        out_shape=jax.ShapeDtypeStruct((hq, n_roff - 1, GRID_W, 2 * GRID_W), F32),
        name="na_bias",
    )(rpb.reshape(-1))


def _na_kernel(q_ref, k_ref, v_ref, ck_ref, cv_ref, bias_ref, o_ref):
    n = k_ref.shape[2]
    rows = n // GRID_W
    kr = min(NA_ROWS, rows)
    nkey = kr * GRID_W
    ck = ck_ref[0, 0]
    cv = cv_ref[0, 0]

    def body(r, carry):
        q0 = pl.multiple_of(r * GRID_W, GRID_W)
        rs = jnp.clip(r - kr // 2, 0, rows - kr)
        k0 = pl.multiple_of(rs * GRID_W, GRID_W)
        q = q_ref[0, :, pl.ds(q0, GRID_W), :].reshape(4 * GRID_W, HEAD_DIM)
        k = k_ref[0, 0, pl.ds(k0, nkey), :]
        v = v_ref[0, 0, pl.ds(k0, nkey), :]
        ro0 = rs - r + (NA_ROWS - 1)
        bias = jnp.concatenate(
            [jnp.concatenate([bias_ref[g, ro0 + 2 * i] for i in range(kr // 2)], axis=1) for g in range(4)], axis=0)
        s = _dot_nt(q, k) + bias
        sc = _dot_nt(q, ck)
        (p, pc), l = _softmax_parts([s, sc])
        o = (_dot(p.astype(BF16), v) + _dot(pc.astype(BF16), cv)) / l
        o_ref[0, :, pl.ds(q0, GRID_W), :] = o.reshape(4, GRID_W, HEAD_DIM).astype(BF16)
        return carry

    lax.fori_loop(0, rows, body, 0)


def _neighbourhood_attn(q, k, v, ck, cv, bias):
    b, hq, n, d = q.shape
    hkv = k.shape[1]
    c = ck.shape[2]
    return pl.pallas_call(
        _na_kernel,
        grid=(b, hkv),
        in_specs=[
            pl.BlockSpec((1, 4, n, d), lambda bi, h: (bi, h, 0, 0)),
            pl.BlockSpec((1, 1, n, d), lambda bi, h: (bi, h, 0, 0)),
            pl.BlockSpec((1, 1, n, d), lambda bi, h: (bi, h, 0, 0)),
            pl.BlockSpec((1, 1, c, d), lambda bi, h: (bi, h, 0, 0)),
            pl.BlockSpec((1, 1, c, d), lambda bi, h: (bi, h, 0, 0)),
            pl.BlockSpec((4,) + bias.shape[1:], lambda bi, h: (h, 0, 0, 0)),
        ],
        out_specs=pl.BlockSpec((1, 4, n, d), lambda bi, h: (bi, h, 0, 0)),
        out_shape=jax.ShapeDtypeStruct((b, hq, n, d), BF16),
        compiler_params=_vmem(40 * 1024 * 1024),
        name="neighbourhood_attn",
    )(q, k, v, ck, cv, bias)


def _ctx_kernel(sink_ref, q_ref, k_ref, v_ref, o_ref):
    c = k_ref.shape[2]
    sink_col = _group_sink_col(sink_ref, pl.program_id(1) * 4, c)
    q = q_ref[0].reshape(4 * c, HEAD_DIM)
    (p,), l = _softmax_parts([_dot_nt(q, k_ref[0, 0])], sink_col)
    o = _dot(p.astype(BF16), v_ref[0, 0]) / l
    o_ref[0] = o.reshape(4, c, HEAD_DIM).astype(BF16)


def _ctx_attn(sink, q, k, v):
    b, hq, c, d = q.shape
    hkv = k.shape[1]
    return pl.pallas_call(
        _ctx_kernel,
        grid=(b, hkv),
        in_specs=[
            pl.BlockSpec(memory_space=pltpu.SMEM),
            pl.BlockSpec((1, 4, c, d), lambda bi, h: (bi, h, 0, 0)),
            pl.BlockSpec((1, 1, c, d), lambda bi, h: (bi, h, 0, 0)),
            pl.BlockSpec((1, 1, c, d), lambda bi, h: (bi, h, 0, 0)),
        ],
        out_specs=pl.BlockSpec((1, 4, c, d), lambda bi, h: (bi, h, 0, 0)),
        out_shape=jax.ShapeDtypeStruct((b, hq, c, d), BF16),
        name="ctx_attn",
    )(sink, q, k, v)


def _diff_kernel(lq1_ref, lk1_ref, lq2_ref, lk2_ref, g_ref, qt_ref, kk_ref, kkc_ref, vt_ref, vtc_ref, o_ref,
                 qz_ref, kb_ref, kmax_ref, m_ref, l_ref, acc1_ref, acc2_ref, *, lam_init):
    tq = qt_ref.shape[-1]
    n_lt, dv, tk = vt_ref.shape[2:]
    n_kt = n_lt + 1
    assert kkc_ref.shape[2] == tk and vtc_ref.shape[2] == 1

    def k_tile(j):
        return kk_ref[0, 0, j * tk:(j + 1) * tk, :] if j < n_lt else kkc_ref[0, 0]

    def v_tile(j):
        return vt_ref[0, 0, j] if j < n_lt else vtc_ref[0, 0, 0]

    @pl.when(pl.program_id(2) == 0)
    def _():
        def group_max(kt):
            kf = kt.astype(F32)
            return jnp.max((kf * kf).reshape(tk // 8, 8, dv), axis=0)

        def kbody(j, carry):
            kb_ref[j] = group_max(kk_ref[0, 0, pl.ds(pl.multiple_of(j * tk, tk), tk), :])
            return carry

        lax.fori_loop(0, n_lt, kbody, 0)
        kb_ref[n_lt] = group_max(kkc_ref[0, 0])
        best = kb_ref[...].reshape(n_kt * 8, dv)
        half = lax.broadcasted_iota(jnp.int32, best.shape, 1) < HEAD_DIM
        k1 = jnp.max(jnp.sum(jnp.where(half, best, 0.0), axis=1, keepdims=True), axis=0, keepdims=True)
        k2 = jnp.max(jnp.sum(jnp.where(half, 0.0, best), axis=1, keepdims=True), axis=0, keepdims=True)
        kmax_ref[:, :tq] = jnp.broadcast_to(jnp.sqrt(k1), (1, tq))
        kmax_ref[:, tq:] = jnp.broadcast_to(jnp.sqrt(k2), (1, tq))

    qt = qt_ref[0, 0]
    zero = jnp.zeros((HEAD_DIM, tq), BF16)
    qz_ref[0:HEAD_DIM, 0:tq] = qt[0:HEAD_DIM]
    qz_ref[0:HEAD_DIM, tq:] = zero
    qz_ref[HEAD_DIM:, 0:tq] = zero
    qz_ref[HEAD_DIM:, tq:] = qt[HEAD_DIM:]
    qz = qz_ref[...]
    qf = qt.astype(F32)
    qsq = qf * qf
    qn = jnp.concatenate([jnp.sum(qsq[:HEAD_DIM], axis=0, keepdims=True),
                          jnp.sum(qsq[HEAD_DIM:], axis=0, keepdims=True)], axis=1)
    shift = jnp.sqrt(qn) * kmax_ref[...] * (1.0 + 2.0 ** -10)

    l8 = jnp.zeros((8, 2 * tq), F32)
    acc1 = jnp.zeros(acc1_ref.shape, F32)
    acc2 = jnp.zeros(acc2_ref.shape, F32)
    ahead = 2
    scores = [_dot(k_tile(j), qz) for j in range(ahead)]
    for j in range(n_kt):
        if j + ahead < n_kt:
            scores.append(_dot(k_tile(j + ahead), qz))
        p = jnp.exp2(scores[j] - shift)
        scores[j] = None
        l8 = l8 + jnp.sum(p.reshape(tk // 8, 8, 2 * tq), axis=0)
        pb = p.astype(BF16)
        vt = v_tile(j)
        acc1 = acc1 + _dot(vt, pb[:, :tq])
        acc2 = acc2 + _dot(vt, pb[:, tq:])
    l_fast = jnp.sum(l8, axis=0, keepdims=True)
    l_ref[...] = l_fast
    acc1_ref[...] = acc1
    acc2_ref[...] = acc2

    @pl.when(jnp.min(l_fast) < 2.0 ** -80)
    def _():
        m_ref[...] = jnp.full(m_ref.shape, NEG, F32)
        l_ref[...] = jnp.zeros(l_ref.shape, F32)
        acc1_ref[...] = jnp.zeros(acc1_ref.shape, F32)
        acc2_ref[...] = jnp.zeros(acc2_ref.shape, F32)

        def step(kt, vt):
            st = _dot(kt, qz_ref[...])
            m_old = m_ref[...]
            m_new = jnp.maximum(m_old, jnp.max(st, axis=0, keepdims=True))
            alpha = jnp.exp2(m_old - m_new)
            p = jnp.exp2(st - m_new)
            l_ref[...] = alpha * l_ref[...] + jnp.sum(p, axis=0, keepdims=True)
            m_ref[...] = m_new
            pb = p.astype(BF16)
            acc1_ref[...] = alpha[:, :tq] * acc1_ref[...] + _dot(vt, pb[:, :tq])
            acc2_ref[...] = alpha[:, tq:] * acc2_ref[...] + _dot(vt, pb[:, tq:])

        def body(j, carry):
            step(kk_ref[0, 0, pl.ds(pl.multiple_of(j * tk, tk), tk), :], vt_ref[0, 0, j])
            return carry

        lax.fori_loop(0, n_lt, body, 0)
        step(kkc_ref[0, 0], vtc_ref[0, 0, 0])

    lam = (jnp.exp(jnp.sum(lq1_ref[...] * lk1_ref[...], axis=-1, keepdims=True))
           - jnp.exp(jnp.sum(lq2_ref[...] * lk2_ref[...], axis=-1, keepdims=True)) + lam_init)
    l = l_ref[...]
    o = acc1_ref[...] / l[:, :tq] - lam * (acc2_ref[...] / l[:, tq:])
    ms = jnp.mean(o * o, axis=0, keepdims=True)
    o = (o * lax.rsqrt(ms + EPS) * g_ref[...]) * (1.0 - lam_init)
    o_ref[0] = o.T.astype(BF16)


def _diff_attn(lam_vecs, sub_g, qt, kk, kkc, vt, vtc, lam_init, tq):
    b, h, dv, n = qt.shape
    n_ctx = kkc.shape[2]
    n_lt, _, tk = vt.shape[2:]
    vec_spec = pl.BlockSpec((1, HEAD_DIM), lambda bi, hi, i: (0, 0))
    return pl.pallas_call(
        functools.partial(_diff_kernel, lam_init=lam_init),
        grid=(b, h, n // tq),
        in_specs=[vec_spec] * 4 + [
            pl.BlockSpec((dv, 1), lambda bi, hi, i: (0, 0)),
            pl.BlockSpec((1, 1, dv, tq), lambda bi, hi, i: (bi, hi, 0, i)),
            pl.BlockSpec((1, 1, n, dv), lambda bi, hi, i: (bi, hi, 0, 0)),
            pl.BlockSpec((1, 1, n_ctx, dv), lambda bi, hi, i: (bi, hi, 0, 0)),
            pl.BlockSpec((1, 1, n_lt, dv, tk), lambda bi, hi, i: (bi, hi, 0, 0, 0)),
            pl.BlockSpec((1, 1, 1, dv, tk), lambda bi, hi, i: (bi, hi, 0, 0, 0)),
        ],
        out_specs=pl.BlockSpec((1, tq, dv), lambda bi, hi, i: (bi, i, hi)),
        out_shape=jax.ShapeDtypeStruct((b, n, h * dv), BF16),
        scratch_shapes=[
            pltpu.VMEM((dv, 2 * tq), BF16),
            pltpu.VMEM((n_lt + 1, 8, dv), F32),
            pltpu.VMEM((1, 2 * tq), F32),
            pltpu.VMEM((1, 2 * tq), F32),
            pltpu.VMEM((1, 2 * tq), F32),
            pltpu.VMEM((dv, tq), F32),
            pltpu.VMEM((dv, tq), F32),
        ],
        compiler_params=pltpu.CompilerParams(
            dimension_semantics=("arbitrary", "arbitrary", "arbitrary"), vmem_limit_bytes=32 * 1024 * 1024),
        name="diff_attn",
    )(*lam_vecs, sub_g.reshape(dv, 1), qt, kk, kkc, vt, vtc)


def _mlp_kernel(*refs, final, ff_chunk):
    if final:
        y_ref, h_ref, mod_ref, g_ref, wo_ref, w1_ref, w2_ref, fg_ref, o_ref = refs
    else:
        y_ref, h_ref, mod_ref, g_ref, wo_ref, w1_ref, w2_ref, o_ref = refs
    hx = h_ref[0] + mod_ref[0, 2:3, :] * _dot(y_ref[0], wo_ref[...])
    a = _modulated(hx, g_ref[...], mod_ref[0, 3:4, :], mod_ref[0, 4:5, :]).astype(BF16)
    acc = jnp.zeros(hx.shape, F32)
    for c0 in range(0, w1_ref.shape[1], ff_chunk):
        u = jnp.maximum(_dot(a, w1_ref[:, c0:c0 + ff_chunk]), 0.0)
        acc = acc + _dot((u * u).astype(BF16), w2_ref[c0:c0 + ff_chunk, :])
    out = hx + mod_ref[0, 5:6, :] * acc
    if final:
        ms = jnp.mean(out * out, axis=-1, keepdims=True)
        out = out * lax.rsqrt(ms + EPS) * fg_ref[...]
    o_ref[0] = out


def _out_mlp(y, h, mod9, mod_row, g2, wo, w1, w2, final_g, tm, name):
    b, n, d = h.shape
    dff = w1.shape[1]
    final = final_g is not None
    if mod_row is None:
        mod_map = lambda bi, i: (bi, 0, 0)
    else:
        mod_map = lambda bi, i: (mod_row, 0, 0)
    tile = pl.BlockSpec((1, tm, d), lambda bi, i: (bi, i, 0))
    in_specs = [tile, tile, pl.BlockSpec((1, 6, d), mod_map), _const_spec((1, d)),
                _const_spec((d, d)), _const_spec((d, dff)), _const_spec((dff, d))]
    args = [y, h, mod9, g2.reshape(1, d), wo, w1, w2]
    if final:
        in_specs.append(_const_spec((1, d)))
        args.append(final_g.reshape(1, d))
    return pl.pallas_call(
        functools.partial(_mlp_kernel, final=final, ff_chunk=1024),
        grid=(b, n // tm),
        in_specs=in_specs,
        out_specs=tile,
        out_shape=jax.ShapeDtypeStruct((b, n, d), F32),
        compiler_params=_vmem(56 * 1024 * 1024),
        name=name,
    )(*args)


def _rope_tables(n):
    t = jnp.arange(n, dtype=jnp.int32)
    row = (t // GRID_W).astype(F32)
    col = (t % GRID_W).astype(F32)
    quarter = HEAD_DIM // 4
    inv_freq = ROPE_THETA ** (-jnp.arange(quarter, dtype=F32) / quarter)
    ar = row[:, None] * inv_freq[None, :]
    ac = col[:, None] * inv_freq[None, :]
    ang = jnp.concatenate([ar, ar, ac, ac], axis=-1)
    cos, sin = jnp.cos(ang), jnp.sin(ang)
    even = (jnp.arange(HEAD_DIM) // quarter) % 2 == 0
    rep = LANES // HEAD_DIM
    sin_a = jnp.tile(jnp.where(even, -sin, 0.0), (1, rep))
    sin_b = jnp.tile(jnp.where(even, 0.0, sin), (1, rep))
    token_major = (jnp.tile(cos, (1, rep)), sin_a, sin_b)
    feature_major = (cos.T, jnp.where(even, -sin, sin).T)
    return token_major, feature_major


def _heads(p, lo, hi, width=HEAD_DIM):
    b, n, _ = p.shape
    return p[:, :, lo:hi].reshape(b, n, (hi - lo) // width, width).transpose(0, 2, 1, 3)


def _unheads(y):
    b, h, n, d = y.shape
    return y.transpose(0, 2, 1, 3).reshape(b, n, h * d)


def kernel(x, c, ctx, c_ctx, ada_w, ada_b, norm1_g, norm2_g, even_w_in, even_w_out, a_sink, b_rpb,
           odd_w_in, odd_w_out, lam_q1, lam_k1, lam_q2, lam_k2, subln_g, mlp_w1, mlp_w2, final_g):
    b, n, d = x.shape
    n_ctx = ctx.shape[1]
    assert ada_w.shape[0] == 2 and d == 1024 and n % 512 == 0
    scale = HEAD_DIM ** -0.5
    a_qw, a_kvw = 8 * HEAD_DIM, 2 * HEAD_DIM
    b_q0 = a_qw + 2 * a_kvw

    cc = jnp.zeros((16, d), F32).at[:b].set(c).at[b].set(c_ctx)
    mod = _ada_call(cc, ada_w, ada_b)[:, :b + 1].reshape(2, b + 1, 6, d)
    rope, rope_t = _rope_tables(n)

    cols0 = jnp.arange(even_w_in.shape[2])
    qcols0 = (cols0 < a_qw) | ((cols0 >= b_q0) & (cols0 < b_q0 + a_qw))
    w_in0 = (even_w_in[0] * jnp.where(qcols0, scale, 1.0)).astype(BF16)
    cols1 = jnp.arange(odd_w_in.shape[2])
    w_in1 = (odd_w_in[0] * jnp.where(cols1 < 1024, scale * math.log2(math.e), 1.0)).astype(BF16)
    w_out0, w_out1 = even_w_out[0].astype(BF16), odd_w_out[0].astype(BF16)
    w1, w2 = mlp_w1.astype(BF16), mlp_w2.astype(BF16)

    px = _project(x, mod[0], None, norm1_g[0], w_in0, (5,) + rope, 512, "proj0_x")
    pc = _project(ctx, mod[0], b, norm1_g[0], w_in0, None, n_ctx, "proj0_ctx")
    ak0, av0, bq0, bk0, bv0 = a_qw, a_qw + a_kvw, b_q0, b_q0 + a_qw, b_q0 + a_qw + a_kvw
    aq, ak, av = _heads(px, 0, ak0), _heads(px, ak0, av0), _heads(px, av0, bq0)
    bq, bk, bv = _heads(px, bq0, bk0), _heads(px, bk0, bv0), _heads(px, bv0, bv0 + a_kvw)
    caq, cak, cav = _heads(pc, 0, ak0), _heads(pc, ak0, av0), _heads(pc, av0, bq0)
    cbq, cbk, cbv = _heads(pc, bq0, bk0), _heads(pc, bk0, bv0), _heads(pc, bv0, bv0 + a_kvw)
    sink = a_sink[0].astype(F32)
    ya = _window_attn(sink, aq, ak, av, cak, cav)
    yb = _neighbourhood_attn(bq, bk, bv, cbk, cbv, _na_bias(b_rpb[0].astype(F32)))
    yx = _unheads(jnp.concatenate([ya, yb], axis=1))
    sink_ab = jnp.concatenate([sink, jnp.full_like(sink, NEG)])
    yc = _unheads(_ctx_attn(sink_ab, jnp.concatenate([caq, cbq], axis=1), jnp.concatenate([cak, cbk], axis=1),
                            jnp.concatenate([cav, cbv], axis=1)))
    hx = _out_mlp(yx, x, mod[0], None, norm2_g[0], w_out0, w1[0], w2[0], None, 512, "mlp0_x")
    hc = _out_mlp(yc, ctx, mod[0], b, norm2_g[0], w_out0, w1[0], w2[0], None, n_ctx, "mlp0_ctx")

    lam_init = 0.8 - 0.6 * math.exp(-0.3 * 1)
    tk = n_ctx
    wq1, wk1, wv1 = w_in1[:, :1024], w_in1[:, 1024:2048], w_in1[:, 2048:]
    qt, kk, vt = _project1(hx, mod[1], None, norm1_g[1], wk1, jnp.concatenate([wq1, wv1], axis=1).T,
                           rope, rope_t, 512, tk, "proj1_x")
    kkc, vtc = _project1(hc, mod[1], b, norm1_g[1], wk1, wv1.T, None, None, n_ctx, tk, "proj1_ctx")
    lam_vecs = [v[0].reshape(1, HEAD_DIM).astype(F32) for v in (lam_q1, lam_k1, lam_q2, lam_k2)]
    yx = _diff_attn(lam_vecs, subln_g[0].astype(F32), qt, kk, kkc, vt, vtc, lam_init, 512)
    return _out_mlp(yx, hx, mod[1], None, norm2_g[1], w_out1, w1[1], w2[1], final_g, 512, "mlp1_x")
```

```python
import functools
import math

import jax
import jax.numpy as jnp
from jax import lax
from jax.experimental import pallas as pl
from jax.experimental.pallas import tpu as pltpu

F32 = jnp.float32
BF16 = jnp.bfloat16

GRID_W = 64
HEAD_DIM = 64
WINDOW = 128
NA_ROWS = 8
NA_COLS = 16
ROPE_THETA = 10000.0
EPS = 1e-6
NEG = -1e30
LOG2E = math.log2(math.e)

V7X_VMEM_BYTES = 64 * 1024 * 1024
LANES = 128


def _vmem(nbytes):
    return pltpu.CompilerParams(vmem_limit_bytes=int(min(nbytes, V7X_VMEM_BYTES - 8 * 1024 * 1024)))


def _const_spec(shape):
    n = len(shape)
    return pl.BlockSpec(shape, lambda *_: (0,) * n, pipeline_mode=pl.Buffered(1))


def _dot(a, b):
    return jnp.dot(a, b, preferred_element_type=F32)


def _dot_nt(a, b):
    return lax.dot_general(a, b, (((1,), (1,)), ((), ())), preferred_element_type=F32)


def _ada_kernel(c_ref, w_ref, b_ref, o_ref):
    cc = c_ref[...]
    s = cc * jax.nn.sigmoid(cc)
    o_ref[0] = jnp.dot(s, w_ref[0], preferred_element_type=F32, precision=lax.Precision.HIGHEST) + b_ref[0]


def _ada_call(cc, ada_w, ada_b):
    depth, d, six_d = ada_w.shape
    tn = 1536
    return pl.pallas_call(
        _ada_kernel,
        grid=(depth, six_d // tn),
        in_specs=[
            pl.BlockSpec((16, d), lambda l, j: (0, 0)),
            pl.BlockSpec((1, d, tn), lambda l, j: (l, 0, j)),
            pl.BlockSpec((1, 1, tn), lambda l, j: (l, 0, j)),
        ],
        out_specs=pl.BlockSpec((1, 16, tn), lambda l, j: (l, 0, j)),
        out_shape=jax.ShapeDtypeStruct((depth, 16, six_d), F32),
        compiler_params=_vmem(32 * 1024 * 1024),
        name="ada_mod",
    )(cc, ada_w, ada_b.reshape(depth, 1, six_d))


def _modulated(x, g, shift, scale):
    ms = jnp.mean(x * x, axis=-1, keepdims=True)
    return (x * lax.rsqrt(ms + EPS) * g) * (1.0 + scale) + shift


def _proj_kernel(*refs, rope_chunks, n_chunk):
    if rope_chunks:
        h_ref, mod_ref, g_ref, w_ref, cos_ref, sa_ref, sb_ref, o_ref = refs
    else:
        h_ref, mod_ref, g_ref, w_ref, o_ref = refs
    a = _modulated(h_ref[0], g_ref[...], mod_ref[0, 0:1, :], mod_ref[0, 1:2, :]).astype(BF16)
    nout = o_ref.shape[-1]
    for c0 in range(0, nout, n_chunk):
        p = _dot(a, w_ref[:, c0:c0 + n_chunk])
        for j in range(n_chunk // LANES):
            lo = c0 + j * LANES
            xc = p[:, j * LANES:(j + 1) * LANES]
            if lo < rope_chunks * LANES:
                xc = (xc * cos_ref[...] + pltpu.roll(xc, LANES - 16, 1) * sa_ref[...]
                      + pltpu.roll(xc, 16, 1) * sb_ref[...])
            o_ref[0, :, lo:lo + LANES] = xc.astype(BF16)


def _project(h, mod9, mod_row, g, w, rope, tm, name):
    b, n, d = h.shape
    nout = w.shape[1]
    rope_chunks = 0 if rope is None else rope[0]
    n_chunk = 512
    if mod_row is None:
        mod_map = lambda bi, i: (bi, 0, 0)
    else:
        mod_map = lambda bi, i: (mod_row, 0, 0)
    in_specs = [
        pl.BlockSpec((1, tm, d), lambda bi, i: (bi, i, 0)),
        pl.BlockSpec((1, 6, d), mod_map),
        _const_spec((1, d)),
        _const_spec((d, nout)),
    ]
    args = [h, mod9, g.reshape(1, d), w]
    if rope_chunks:
        in_specs += [pl.BlockSpec((tm, LANES), lambda bi, i: (i, 0))] * 3
        args += list(rope[1:])
    return pl.pallas_call(
        functools.partial(_proj_kernel, rope_chunks=rope_chunks, n_chunk=n_chunk),
        grid=(b, n // tm),
        in_specs=in_specs,
        out_specs=pl.BlockSpec((1, tm, nout), lambda bi, i: (bi, i, 0)),
        out_shape=jax.ShapeDtypeStruct((b, n, nout), BF16),
        compiler_params=_vmem(48 * 1024 * 1024),
        name=name,
    )(*args)


def _rope_rows(x, cos_t, sin_t):
    q = HEAD_DIM // 4
    rot = jnp.concatenate([x[q:2 * q], x[0:q], x[3 * q:4 * q], x[2 * q:3 * q]], axis=0)
    return x * cos_t + rot * sin_t


def _proj1_kernel(*refs, with_q, tk):
    if with_q:
        (h_ref, mod_ref, g_ref, wk_ref, wt_ref, cos_ref, sa_ref, sb_ref, cost_ref, sint_ref,
         qt_ref, kk_ref, vt_ref) = refs
    else:
        h_ref, mod_ref, g_ref, wk_ref, wt_ref, kk_ref, vt_ref = refs
    a = _modulated(h_ref[0], g_ref[...], mod_ref[0, 0:1, :], mod_ref[0, 1:2, :]).astype(BF16)
    tm = a.shape[0]
    dv = 2 * HEAD_DIM
    n_heads = kk_ref.shape[1]
    chunk = 4 * dv
    for c0 in range(0, n_heads * dv, chunk):
        p = _dot(a, wk_ref[:, c0:c0 + chunk])
        for j in range(chunk // dv):
            xc = p[:, j * dv:(j + 1) * dv]
            if with_q:
                xc = (xc * cos_ref[...] + pltpu.roll(xc, LANES - 16, 1) * sa_ref[...]
                      + pltpu.roll(xc, 16, 1) * sb_ref[...])
            kk_ref[0, c0 // dv + j] = xc.astype(BF16)
    n_q = n_heads * dv if with_q else 0
    for f0 in range(0, wt_ref.shape[0], chunk):
        xt = _dot_nt(wt_ref[f0:f0 + chunk, :], a)
        for j in range(chunk // dv):
            row = f0 + j * dv
            hd = xt[j * dv:(j + 1) * dv]
            if row < n_q:
                hd = jnp.concatenate([_rope_rows(hd[m * HEAD_DIM:(m + 1) * HEAD_DIM], cost_ref[...], sint_ref[...])
                                      for m in range(2)], axis=0)
                qt_ref[0, row // dv] = hd.astype(BF16)
            else:
                for t in range(tm // tk):
                    vt_ref[0, (row - n_q) // dv, t] = hd[:, t * tk:(t + 1) * tk].astype(BF16)


def _project1(h, mod9, mod_row, g, wk, wt, rope, rope_t, tm, tk, name):
    b, n, d = h.shape
    dv = 2 * HEAD_DIM
    n_heads = wk.shape[1] // dv
    with_q = rope is not None
    if mod_row is None:
        mod_map = lambda bi, i: (bi, 0, 0)
    else:
        mod_map = lambda bi, i: (mod_row, 0, 0)
    in_specs = [
        pl.BlockSpec((1, tm, d), lambda bi, i: (bi, i, 0)),
        pl.BlockSpec((1, 6, d), mod_map),
        _const_spec((1, d)),
        _const_spec(wk.shape),
        _const_spec(wt.shape),
    ]
    args = [h, mod9, g.reshape(1, d), wk, wt]
    out_specs = [
        pl.BlockSpec((1, n_heads, tm, dv), lambda bi, i: (bi, 0, i, 0)),
        pl.BlockSpec((1, n_heads, tm // tk, dv, tk), lambda bi, i: (bi, 0, i, 0, 0)),
    ]
    out_shape = [
        jax.ShapeDtypeStruct((b, n_heads, n, dv), BF16),
        jax.ShapeDtypeStruct((b, n_heads, n // tk, dv, tk), BF16),
    ]
    if with_q:
        in_specs += [pl.BlockSpec((tm, LANES), lambda bi, i: (i, 0))] * 3
        in_specs += [pl.BlockSpec((HEAD_DIM, tm), lambda bi, i: (0, i))] * 2
        args += list(rope) + list(rope_t)
        out_specs.insert(0, pl.BlockSpec((1, n_heads, dv, tm), lambda bi, i: (bi, 0, 0, i)))
        out_shape.insert(0, jax.ShapeDtypeStruct((b, n_heads, dv, n), BF16))
    return pl.pallas_call(
        functools.partial(_proj1_kernel, with_q=with_q, tk=tk),
        grid=(b, n // tm),
        in_specs=in_specs,
        out_specs=out_specs,
        out_shape=out_shape,
        compiler_params=_vmem(48 * 1024 * 1024),
        name=name,
    )(*args)


def _proj0_kernel(h_ref, mod_ref, g_ref, wk_ref, wt_ref, cos_ref, sa_ref, sb_ref, cost_ref, sint_ref,
                  qa_ref, ka_ref, va_ref, qb_ref, kb_ref, vb_ref, *, tw):
    a = _modulated(h_ref[0], g_ref[...], mod_ref[0, 0:1, :], mod_ref[0, 1:2, :]).astype(BF16)
    tm = a.shape[0]
    kw = ka_ref.shape[-1]
    p = _dot(a, wk_ref[...])
    xa = p[:, :kw]
    xa = xa * cos_ref[...] + pltpu.roll(xa, LANES - 16, 1) * sa_ref[...] + pltpu.roll(xa, 16, 1) * sb_ref[...]
    ka_ref[0] = xa.astype(BF16)
    kb_ref[0] = p[:, kw:].astype(BF16)
    qw = qa_ref.shape[2]
    rows = qw + va_ref.shape[2]
    for mixer, (q_ref, v_ref) in enumerate(((qa_ref, va_ref), (qb_ref, vb_ref))):
        xt = _dot_nt(wt_ref[mixer * rows:(mixer + 1) * rows, :], a)
        q = xt[:qw]
        if mixer == 0:
            q = jnp.concatenate([_rope_rows(q[h * HEAD_DIM:(h + 1) * HEAD_DIM], cost_ref[...], sint_ref[...])
                                 for h in range(qw // HEAD_DIM)], axis=0)
        for t in range(tm // tw):
            q_ref[0, t] = q[:, t * tw:(t + 1) * tw].astype(BF16)
            v_ref[0, t] = xt[qw:, t * tw:(t + 1) * tw].astype(BF16)


def _project0(h, mod9, g, wk, wt, rope, rope_t, tm, tw):
    b, n, d = h.shape
    kw = wk.shape[1] // 2
    rows = wt.shape[0] // 2
    qw = rows - kw
    q_spec = pl.BlockSpec((1, tm // tw, qw, tw), lambda bi, i: (bi, i, 0, 0))
    k_spec = pl.BlockSpec((1, tm, kw), lambda bi, i: (bi, i, 0))
    v_spec = pl.BlockSpec((1, tm // tw, kw, tw), lambda bi, i: (bi, i, 0, 0))
    q_shape = jax.ShapeDtypeStruct((b, n // tw, qw, tw), BF16)
    k_shape = jax.ShapeDtypeStruct((b, n, kw), BF16)
    v_shape = jax.ShapeDtypeStruct((b, n // tw, kw, tw), BF16)
    return pl.pallas_call(
        functools.partial(_proj0_kernel, tw=tw),
        grid=(b, n // tm),
        in_specs=[
            pl.BlockSpec((1, tm, d), lambda bi, i: (bi, i, 0)),
            pl.BlockSpec((1, 6, d), lambda bi, i: (bi, 0, 0)),
            _const_spec((1, d)),
            _const_spec(wk.shape),
            _const_spec(wt.shape),
        ] + [pl.BlockSpec((tm, LANES), lambda bi, i: (i, 0))] * 3
          + [pl.BlockSpec((HEAD_DIM, tm), lambda bi, i: (0, i))] * 2,
        out_specs=[q_spec, k_spec, v_spec, q_spec, k_spec, v_spec],
        out_shape=[q_shape, k_shape, v_shape, q_shape, k_shape, v_shape],
        compiler_params=_vmem(48 * 1024 * 1024),
        name="proj0_x",
    )(h, mod9, g.reshape(1, d), wk, wt, *rope, *rope_t)


def _softmax_parts(scores, extra_col=None):
    m = functools.reduce(jnp.maximum, [jnp.max(s, axis=-1, keepdims=True) for s in scores])
    if extra_col is not None:
        m = jnp.maximum(m, extra_col)
    ps = [jnp.exp2(s - m) for s in scores]
    l = functools.reduce(jnp.add, [jnp.sum(p, axis=-1, keepdims=True) for p in ps])
    if extra_col is not None:
        l = l + jnp.exp2(extra_col - m)
    return ps, l


def _group_sink_col(sink_ref, base, rows_per_head):
    rows = lax.broadcasted_iota(jnp.int32, (4 * rows_per_head, 1), 0) // rows_per_head
    col = jnp.full((4 * rows_per_head, 1), sink_ref[base + 3], F32)
    for g in range(3):
        col = jnp.where(rows == g, sink_ref[base + g], col)
    return col


def _pipelined(items, ahead):
    scores = [items[k][0]() for k in range(min(ahead, len(items)))]
    for k in range(len(items)):
        if k + ahead < len(items):
            scores.append(items[k + ahead][0]())
        items[k][1](scores[k])
        scores[k] = None


def _column_softmax(pieces, extra_row=None):
    m = functools.reduce(jnp.maximum, [jnp.max(s, axis=0, keepdims=True) for s in pieces])
    if extra_row is not None:
        m = jnp.maximum(m, extra_row)
    ps = [jnp.exp2(s - m) for s in pieces]
    l = functools.reduce(jnp.add, [jnp.sum(p, axis=0, keepdims=True) for p in ps])
    if extra_row is not None:
        l = l + jnp.exp2(extra_row - m)
    return jnp.concatenate([p.astype(BF16) for p in ps], axis=0), l


def _padded_queries(top, h):
    z = jnp.zeros_like(top)
    return jnp.concatenate([top, z] if h == 0 else [z, top], axis=0)


def _win_kernel(sink_ref, q_ref, k_ref, v_ref, kc_ref, vc_ref, o_ref, *, blk):
    n = k_ref.shape[1]
    nb = n // blk
    c_idx = lax.broadcasted_iota(jnp.int32, (blk, blk), 0)
    r_idx = lax.broadcasted_iota(jnp.int32, (blk, blk), 1)
    lower = jnp.tile(jnp.where(c_idx >= r_idx, 0.0, NEG), (1, 4))
    upper = jnp.tile(jnp.where(c_idx <= r_idx, 0.0, NEG), (1, 4))
    lane_head = lax.broadcasted_iota(jnp.int32, (1, 4 * blk), 1) // blk

    def sink_row(h):
        row = jnp.full((1, 4 * blk), sink_ref[4 * h + 3], F32)
        for g in range(3):
            row = jnp.where(lane_head == g, sink_ref[4 * h + g], row)
        return row

    sinks = [sink_row(0), sink_row(1)]

    def make_item(i, h, k_rows, v_tiles, biases):
        def score():
            top = jnp.concatenate([q_ref[0, i, (4 * h + g) * HEAD_DIM:(4 * h + g + 1) * HEAD_DIM, :]
                                   for g in range(4)], axis=1)
            return _dot(jnp.concatenate([k_rows(), kc_ref[0]], axis=0), _padded_queries(top, h))

        def finish(s):
            pieces = [s[t * blk:(t + 1) * blk] if bias is None else s[t * blk:(t + 1) * blk] + bias
                      for t, bias in enumerate(biases)]
            pieces.append(s[len(biases) * blk:])
            p, l = _column_softmax(pieces, sinks[h])
            vall = jnp.concatenate(v_tiles() + [vc_ref[0]], axis=1)
            o = _dot(vall, p)[h * HEAD_DIM:(h + 1) * HEAD_DIM] / l
            o4 = jnp.concatenate([o[:, g * blk:(g + 1) * blk] for g in range(4)], axis=0)
            q0 = i * blk if isinstance(i, int) else pl.multiple_of(i * blk, blk)
            o_ref[0, pl.ds(q0, blk), 4 * h * HEAD_DIM:4 * (h + 1) * HEAD_DIM] = o4.T.astype(BF16)

        return score, finish

    def edge_items(i, t0, biases):
        return [make_item(i, h, lambda: k_ref[0, t0 * blk:(t0 + 2) * blk, :],
                          lambda: [v_ref[0, t0], v_ref[0, t0 + 1]], biases) for h in range(2)]

    def inner_items(i):
        k0 = pl.multiple_of((i - 1) * blk, blk)
        return [make_item(i, h, lambda: k_ref[0, pl.ds(k0, 3 * blk), :],
                          lambda: [v_ref[0, i - 1], v_ref[0, i], v_ref[0, i + 1]], [lower, None, upper])
                for h in range(2)]

    _pipelined(edge_items(0, 0, [None, upper]), 1)

    def body(j, carry):
        _pipelined(inner_items(1 + 2 * j) + inner_items(2 + 2 * j), 2)
        return carry

    lax.fori_loop(0, (nb - 2) // 2, body, 0)
    _pipelined(edge_items(nb - 1, nb - 2, [lower, None]), 1)


def _window_attn(sink, q_t, k, v_t, kc, vc_t):
    b, nb, hd, blk = q_t.shape
    n = k.shape[1]
    c = kc.shape[1]
    assert blk == WINDOW and (nb - 2) % 2 == 0
    return pl.pallas_call(
        functools.partial(_win_kernel, blk=blk),
        grid=(b,),
        in_specs=[
            pl.BlockSpec(memory_space=pltpu.SMEM),
            pl.BlockSpec((1, nb, hd, blk), lambda bi: (bi, 0, 0, 0)),
            pl.BlockSpec((1, n, 2 * HEAD_DIM), lambda bi: (bi, 0, 0)),
            pl.BlockSpec((1, nb, 2 * HEAD_DIM, blk), lambda bi: (bi, 0, 0, 0)),
            pl.BlockSpec((1, c, 2 * HEAD_DIM), lambda bi: (bi, 0, 0)),
            pl.BlockSpec((1, 2 * HEAD_DIM, c), lambda bi: (bi, 0, 0)),
        ],
        out_specs=pl.BlockSpec((1, n, hd), lambda bi: (bi, 0, 0)),
        out_shape=jax.ShapeDtypeStruct((b, n, hd), BF16),
        compiler_params=_vmem(48 * 1024 * 1024),
        name="window_attn",
    )(sink, q_t, k, v_t, kc, vc_t)


def _na_bias_kernel(rpb_ref, o_ref):
    h = pl.program_id(0)
    n_roff = 2 * NA_ROWS - 1
    n_coff = 2 * NA_COLS - 1
    shape = (GRID_W, 4 * GRID_W)
    ck = lax.broadcasted_iota(jnp.int32, shape, 0)
    lane = lax.broadcasted_iota(jnp.int32, shape, 1)
    cq = lane % GRID_W
    head = lane // GRID_W
    cstart = jnp.clip(cq - NA_COLS // 2, 0, GRID_W - NA_COLS)
    valid = (ck >= cstart) & (ck < cstart + NA_COLS)
    diff = ck - cq + (NA_COLS - 1)
    hits = [[valid & (diff == o) & (head == g) for o in range(n_coff)] for g in range(4)]
    tiles = []
    for roff in range(n_roff):
        t = jnp.full(shape, NEG, F32)
        for g in range(4):
            for o in range(n_coff):
                t = jnp.where(hits[g][o], rpb_ref[((4 * h + g) * n_roff + roff) * n_coff + o] * LOG2E, t)
        tiles.append(t)
    for ro in range(n_roff - 1):
        o_ref[0, ro] = jnp.concatenate([tiles[ro], tiles[ro + 1]], axis=0)


def _na_bias(rpb):
    hq, n_roff, n_coff = rpb.shape
    return pl.pallas_call(
        _na_bias_kernel,
        grid=(hq // 4,),
        in_specs=[pl.BlockSpec(memory_space=pltpu.SMEM)],
        out_specs=pl.BlockSpec((1, n_roff - 1, 2 * GRID_W, 4 * GRID_W), lambda h: (h, 0, 0, 0)),
        out_shape=jax.ShapeDtypeStruct((hq // 4, n_roff - 1, 2 * GRID_W, 4 * GRID_W), F32),
        name="na_bias",
    )(rpb.reshape(-1))


def _na_kernel(q_ref, k_ref, v_ref, kc_ref, vc_ref, bias_ref, o_ref, vall_ref):
    n = k_ref.shape[1]
    rows = n // GRID_W
    n_tiles = rows // 2
    kr = min(NA_ROWS, rows)
    nkey = kr * GRID_W
    lo = lax.broadcasted_iota(jnp.int32, (2 * HEAD_DIM, 2 * GRID_W), 1) < GRID_W
    lo_q = lax.broadcasted_iota(jnp.int32, (HEAD_DIM, 2 * GRID_W), 1) < GRID_W

    def swap(x):
        return pltpu.roll(x, GRID_W, 1)

    def build(t, carry):
        a = v_ref[0, t]
        vall_ref[0, t] = a
        nxt = v_ref[0, jnp.minimum(t + 1, n_tiles - 1)]
        vall_ref[1, t] = jnp.where(lo, swap(a.astype(F32)), swap(nxt.astype(F32))).astype(BF16)
        return carry

    lax.fori_loop(0, n_tiles, build, 0)

    def row_start(r):
        return jnp.clip(r - kr // 2, 0, rows - kr)

    def make_item(t, par, h, done):
        r = 2 * t + par

        def score():
            cols = []
            for c in range(2):
                g0 = 4 * h + 2 * c
                a0 = q_ref[0, t, g0 * HEAD_DIM:(g0 + 1) * HEAD_DIM, :].astype(F32)
                a1 = q_ref[0, t, (g0 + 1) * HEAD_DIM:(g0 + 2) * HEAD_DIM, :].astype(F32)
                cols.append(jnp.where(lo_q, a0, swap(a1)) if par == 0 else jnp.where(lo_q, swap(a0), a1))
            top = jnp.concatenate(cols, axis=1).astype(BF16)
            k0 = pl.multiple_of(row_start(r) * GRID_W, GRID_W)
            return _dot(jnp.concatenate([k_ref[0, pl.ds(k0, nkey), :], kc_ref[0]], axis=0), _padded_queries(top, h))

        def finish(s):
            rs = row_start(r)
            ro0 = rs - r + (NA_ROWS - 1)
            bias = jnp.concatenate([bias_ref[h, ro0 + 2 * i] for i in range(kr // 2)], axis=0)
            p, l = _column_softmax([s[:nkey] + bias, s[nkey:]])
            vwin = [vall_ref[rs & 1, (rs >> 1) + i] for i in range(nkey // (2 * GRID_W))]
            o = _dot(jnp.concatenate(vwin + [vc_ref[0]], axis=1), p)[h * HEAD_DIM:(h + 1) * HEAD_DIM] / l
            done.append(o)
            if par == 1:
                o_even, o_odd = done
                ws = []
                for c in range(2):
                    ce = o_even[:, c * 2 * GRID_W:(c + 1) * 2 * GRID_W]
                    co = o_odd[:, c * 2 * GRID_W:(c + 1) * 2 * GRID_W]
                    ws.append(jnp.where(lo_q, ce, swap(co)))
                    ws.append(jnp.where(lo_q, swap(ce), co))
                w = jnp.concatenate(ws, axis=0)
                q0 = pl.multiple_of(t * 2 * GRID_W, 2 * GRID_W)
                o_ref[0, pl.ds(q0, 2 * GRID_W), 4 * h * HEAD_DIM:4 * (h + 1) * HEAD_DIM] = w.T.astype(BF16)

        return score, finish

    def body(t, carry):
        items = []
        for h in range(2):
            done = []
            items += [make_item(t, 0, h, done), make_item(t, 1, h, done)]
        _pipelined(items, 2)
        return carry

    lax.fori_loop(0, n_tiles, body, 0)


def _neighbourhood_attn(q_t, k, v_t, kc, vc_t, bias):
    b, n_tiles, hd, tw = q_t.shape
    n = k.shape[1]
    c = kc.shape[1]
    assert tw == 2 * GRID_W and n_tiles * tw == n
    return pl.pallas_call(
        _na_kernel,
        grid=(b,),
        in_specs=[
            pl.BlockSpec((1, n_tiles, hd, tw), lambda bi: (bi, 0, 0, 0)),
            pl.BlockSpec((1, n, 2 * HEAD_DIM), lambda bi: (bi, 0, 0)),
            pl.BlockSpec((1, n_tiles, 2 * HEAD_DIM, tw), lambda bi: (bi, 0, 0, 0)),
            pl.BlockSpec((1, c, 2 * HEAD_DIM), lambda bi: (bi, 0, 0)),
            pl.BlockSpec((1, 2 * HEAD_DIM, c), lambda bi: (bi, 0, 0)),
            _const_spec(bias.shape),
        ],
        out_specs=pl.BlockSpec((1, n, hd), lambda bi: (bi, 0, 0)),
        out_shape=jax.ShapeDtypeStruct((b, n, hd), BF16),
        scratch_shapes=[pltpu.VMEM((2, n_tiles, 2 * HEAD_DIM, tw), BF16)],
        compiler_params=_vmem(52 * 1024 * 1024),
        name="neighbourhood_attn",
    )(q_t, k, v_t, kc, vc_t, bias)


def _ctx_kernel(sink_ref, q_ref, k_ref, v_ref, o_ref):
    c = k_ref.shape[2]
    sink_col = _group_sink_col(sink_ref, pl.program_id(1) * 4, c)
    q = q_ref[0].reshape(4 * c, HEAD_DIM)
    (p,), l = _softmax_parts([_dot_nt(q, k_ref[0, 0])], sink_col)
    o = _dot(p.astype(BF16), v_ref[0, 0]) / l
    o_ref[0] = o.reshape(4, c, HEAD_DIM).astype(BF16)


def _ctx_attn(sink, q, k, v):
    b, hq, c, d = q.shape
    hkv = k.shape[1]
    return pl.pallas_call(
        _ctx_kernel,
        grid=(b, hkv),
        in_specs=[
            pl.BlockSpec(memory_space=pltpu.SMEM),
            pl.BlockSpec((1, 4, c, d), lambda bi, h: (bi, h, 0, 0)),
            pl.BlockSpec((1, 1, c, d), lambda bi, h: (bi, h, 0, 0)),
            pl.BlockSpec((1, 1, c, d), lambda bi, h: (bi, h, 0, 0)),
        ],
        out_specs=pl.BlockSpec((1, 4, c, d), lambda bi, h: (bi, h, 0, 0)),
        out_shape=jax.ShapeDtypeStruct((b, hq, c, d), BF16),
        name="ctx_attn",
    )(sink, q, k, v)


def _diff_kernel(lq1_ref, lk1_ref, lq2_ref, lk2_ref, g_ref, qt_ref, kk_ref, kkc_ref, vt_ref, vtc_ref, o_ref,
                 qz_ref, kb_ref, kmax_ref, m_ref, l_ref, acc1_ref, acc2_ref, *, lam_init):
    tq = qt_ref.shape[-1]
    n_lt, dv, tk = vt_ref.shape[2:]
    n_kt = n_lt + 1
    assert kkc_ref.shape[2] == tk and vtc_ref.shape[2] == 1

    def k_tile(j):
        return kk_ref[0, 0, j * tk:(j + 1) * tk, :] if j < n_lt else kkc_ref[0, 0]

    def v_tile(j):
        return vt_ref[0, 0, j] if j < n_lt else vtc_ref[0, 0, 0]

    @pl.when(pl.program_id(2) == 0)
    def _():
        def group_max(kt):
            kf = kt.astype(F32)
            return jnp.max((kf * kf).reshape(tk // 8, 8, dv), axis=0)

        def kbody(j, carry):
            kb_ref[j] = group_max(kk_ref[0, 0, pl.ds(pl.multiple_of(j * tk, tk), tk), :])
            return carry

        lax.fori_loop(0, n_lt, kbody, 0)
        kb_ref[n_lt] = group_max(kkc_ref[0, 0])
        best = kb_ref[...].reshape(n_kt * 8, dv)
        half = lax.broadcasted_iota(jnp.int32, best.shape, 1) < HEAD_DIM
        k1 = jnp.max(jnp.sum(jnp.where(half, best, 0.0), axis=1, keepdims=True), axis=0, keepdims=True)
        k2 = jnp.max(jnp.sum(jnp.where(half, 0.0, best), axis=1, keepdims=True), axis=0, keepdims=True)
        kmax_ref[:, :tq] = jnp.broadcast_to(jnp.sqrt(k1), (1, tq))
        kmax_ref[:, tq:] = jnp.broadcast_to(jnp.sqrt(k2), (1, tq))

    qt = qt_ref[0, 0]
    zero = jnp.zeros((HEAD_DIM, tq), BF16)
    qz_ref[0:HEAD_DIM, 0:tq] = qt[0:HEAD_DIM]
    qz_ref[0:HEAD_DIM, tq:] = zero
    qz_ref[HEAD_DIM:, 0:tq] = zero
    qz_ref[HEAD_DIM:, tq:] = qt[HEAD_DIM:]
    qz = qz_ref[...]
    qf = qt.astype(F32)
    qsq = qf * qf
    qn = jnp.concatenate([jnp.sum(qsq[:HEAD_DIM], axis=0, keepdims=True),
                          jnp.sum(qsq[HEAD_DIM:], axis=0, keepdims=True)], axis=1)
    shift = jnp.sqrt(qn) * kmax_ref[...] * (1.0 + 2.0 ** -10)

    l8 = jnp.zeros((8, 2 * tq), F32)
    acc1 = jnp.zeros(acc1_ref.shape, F32)
    acc2 = jnp.zeros(acc2_ref.shape, F32)
    ahead = 2
    scores = [_dot(k_tile(j), qz) for j in range(ahead)]
    for j in range(n_kt):
        if j + ahead < n_kt:
            scores.append(_dot(k_tile(j + ahead), qz))
        p = jnp.exp2(scores[j] - shift)
        scores[j] = None
        l8 = l8 + jnp.sum(p.reshape(tk // 8, 8, 2 * tq), axis=0)
        pb = p.astype(BF16)
        vt = v_tile(j)
        acc1 = acc1 + _dot(vt, pb[:, :tq])
        acc2 = acc2 + _dot(vt, pb[:, tq:])
    l_fast = jnp.sum(l8, axis=0, keepdims=True)
    l_ref[...] = l_fast
    acc1_ref[...] = acc1
    acc2_ref[...] = acc2

    @pl.when(jnp.min(l_fast) < 2.0 ** -80)
    def _():
        m_ref[...] = jnp.full(m_ref.shape, NEG, F32)
        l_ref[...] = jnp.zeros(l_ref.shape, F32)
        acc1_ref[...] = jnp.zeros(acc1_ref.shape, F32)
        acc2_ref[...] = jnp.zeros(acc2_ref.shape, F32)

        def step(kt, vt):
            st = _dot(kt, qz_ref[...])
            m_old = m_ref[...]
            m_new = jnp.maximum(m_old, jnp.max(st, axis=0, keepdims=True))
            alpha = jnp.exp2(m_old - m_new)
            p = jnp.exp2(st - m_new)
            l_ref[...] = alpha * l_ref[...] + jnp.sum(p, axis=0, keepdims=True)
            m_ref[...] = m_new
            pb = p.astype(BF16)
            acc1_ref[...] = alpha[:, :tq] * acc1_ref[...] + _dot(vt, pb[:, :tq])
            acc2_ref[...] = alpha[:, tq:] * acc2_ref[...] + _dot(vt, pb[:, tq:])

        def body(j, carry):
            step(kk_ref[0, 0, pl.ds(pl.multiple_of(j * tk, tk), tk), :], vt_ref[0, 0, j])
            return carry

        lax.fori_loop(0, n_lt, body, 0)
        step(kkc_ref[0, 0], vtc_ref[0, 0, 0])

    lam = (jnp.exp(jnp.sum(lq1_ref[...] * lk1_ref[...], axis=-1, keepdims=True))
           - jnp.exp(jnp.sum(lq2_ref[...] * lk2_ref[...], axis=-1, keepdims=True)) + lam_init)
    l = l_ref[...]
    o = acc1_ref[...] / l[:, :tq] - lam * (acc2_ref[...] / l[:, tq:])
    ms = jnp.mean(o * o, axis=0, keepdims=True)
    o = (o * lax.rsqrt(ms + EPS) * g_ref[...]) * (1.0 - lam_init)
    o_ref[0] = o.T.astype(BF16)


def _diff_attn(lam_vecs, sub_g, qt, kk, kkc, vt, vtc, lam_init, tq):
    b, h, dv, n = qt.shape
    n_ctx = kkc.shape[2]
    n_lt, _, tk = vt.shape[2:]
    vec_spec = pl.BlockSpec((1, HEAD_DIM), lambda bi, hi, i: (0, 0))
    return pl.pallas_call(
        functools.partial(_diff_kernel, lam_init=lam_init),
        grid=(b, h, n // tq),
        in_specs=[vec_spec] * 4 + [
            pl.BlockSpec((dv, 1), lambda bi, hi, i: (0, 0)),
            pl.BlockSpec((1, 1, dv, tq), lambda bi, hi, i: (bi, hi, 0, i)),
            pl.BlockSpec((1, 1, n, dv), lambda bi, hi, i: (bi, hi, 0, 0)),
            pl.BlockSpec((1, 1, n_ctx, dv), lambda bi, hi, i: (bi, hi, 0, 0)),
            pl.BlockSpec((1, 1, n_lt, dv, tk), lambda bi, hi, i: (bi, hi, 0, 0, 0)),
            pl.BlockSpec((1, 1, 1, dv, tk), lambda bi, hi, i: (bi, hi, 0, 0, 0)),
        ],
        out_specs=pl.BlockSpec((1, tq, dv), lambda bi, hi, i: (bi, i, hi)),
        out_shape=jax.ShapeDtypeStruct((b, n, h * dv), BF16),
        scratch_shapes=[
            pltpu.VMEM((dv, 2 * tq), BF16),
            pltpu.VMEM((n_lt + 1, 8, dv), F32),
            pltpu.VMEM((1, 2 * tq), F32),
            pltpu.VMEM((1, 2 * tq), F32),
            pltpu.VMEM((1, 2 * tq), F32),
            pltpu.VMEM((dv, tq), F32),
            pltpu.VMEM((dv, tq), F32),
        ],
        compiler_params=pltpu.CompilerParams(
            dimension_semantics=("arbitrary", "arbitrary", "arbitrary"), vmem_limit_bytes=32 * 1024 * 1024),
        name="diff_attn",
    )(*lam_vecs, sub_g.reshape(dv, 1), qt, kk, kkc, vt, vtc)


def _mlp_kernel(*refs, n_y, final, ff_chunk):
    y_refs, refs = refs[:n_y], refs[n_y:]
    if final:
        h_ref, mod_ref, g_ref, wo_ref, w1_ref, w2_ref, fg_ref, o_ref = refs
    else:
        h_ref, mod_ref, g_ref, wo_ref, w1_ref, w2_ref, o_ref = refs
    yw = wo_ref.shape[0] // n_y
    y = functools.reduce(jnp.add, [_dot(y_ref[0], wo_ref[j * yw:(j + 1) * yw, :]) for j, y_ref in enumerate(y_refs)])
    hx = h_ref[0] + mod_ref[0, 2:3, :] * y
    a = _modulated(hx, g_ref[...], mod_ref[0, 3:4, :], mod_ref[0, 4:5, :]).astype(BF16)
    acc = jnp.zeros(hx.shape, F32)
    for c0 in range(0, w1_ref.shape[1], ff_chunk):
        u = jnp.maximum(_dot(a, w1_ref[:, c0:c0 + ff_chunk]), 0.0)
        acc = acc + _dot((u * u).astype(BF16), w2_ref[c0:c0 + ff_chunk, :])
    out = hx + mod_ref[0, 5:6, :] * acc
    if final:
        ms = jnp.mean(out * out, axis=-1, keepdims=True)
        out = out * lax.rsqrt(ms + EPS) * fg_ref[...]
    o_ref[0] = out


def _out_mlp(ys, h, mod9, mod_row, g2, wo, w1, w2, final_g, tm, name):
    b, n, d = h.shape
    dff = w1.shape[1]
    final = final_g is not None
    if mod_row is None:
        mod_map = lambda bi, i: (bi, 0, 0)
    else:
        mod_map = lambda bi, i: (mod_row, 0, 0)
    tile = pl.BlockSpec((1, tm, d), lambda bi, i: (bi, i, 0))
    in_specs = [pl.BlockSpec((1, tm, y.shape[-1]), lambda bi, i: (bi, i, 0)) for y in ys]
    in_specs += [tile, pl.BlockSpec((1, 6, d), mod_map), _const_spec((1, d)),
                 _const_spec((d, d)), _const_spec((d, dff)), _const_spec((dff, d))]
    args = list(ys) + [h, mod9, g2.reshape(1, d), wo, w1, w2]
    if final:
        in_specs.append(_const_spec((1, d)))
        args.append(final_g.reshape(1, d))
    return pl.pallas_call(
        functools.partial(_mlp_kernel, n_y=len(ys), final=final, ff_chunk=1024),
        grid=(b, n // tm),
        in_specs=in_specs,
        out_specs=tile,
        out_shape=jax.ShapeDtypeStruct((b, n, d), F32),
        compiler_params=_vmem(56 * 1024 * 1024),
        name=name,
    )(*args)


def _rope_tables(n):
    t = jnp.arange(n, dtype=jnp.int32)
    row = (t // GRID_W).astype(F32)
    col = (t % GRID_W).astype(F32)
    quarter = HEAD_DIM // 4
    inv_freq = ROPE_THETA ** (-jnp.arange(quarter, dtype=F32) / quarter)
    ar = row[:, None] * inv_freq[None, :]
    ac = col[:, None] * inv_freq[None, :]
    ang = jnp.concatenate([ar, ar, ac, ac], axis=-1)
    cos, sin = jnp.cos(ang), jnp.sin(ang)
    even = (jnp.arange(HEAD_DIM) // quarter) % 2 == 0
    rep = LANES // HEAD_DIM
    sin_a = jnp.tile(jnp.where(even, -sin, 0.0), (1, rep))
    sin_b = jnp.tile(jnp.where(even, 0.0, sin), (1, rep))
    token_major = (jnp.tile(cos, (1, rep)), sin_a, sin_b)
    feature_major = (cos.T, jnp.where(even, -sin, sin).T)
    return token_major, feature_major


def _heads(p, lo, hi, width=HEAD_DIM):
    b, n, _ = p.shape
    return p[:, :, lo:hi].reshape(b, n, (hi - lo) // width, width).transpose(0, 2, 1, 3)


def _unheads(y):
    b, h, n, d = y.shape
    return y.transpose(0, 2, 1, 3).reshape(b, n, h * d)


def kernel(x, c, ctx, c_ctx, ada_w, ada_b, norm1_g, norm2_g, even_w_in, even_w_out, a_sink, b_rpb,
           odd_w_in, odd_w_out, lam_q1, lam_k1, lam_q2, lam_k2, subln_g, mlp_w1, mlp_w2, final_g):
    b, n, d = x.shape
    n_ctx = ctx.shape[1]
    assert ada_w.shape[0] == 2 and d == 1024 and n % 512 == 0
    scale = HEAD_DIM ** -0.5
    a_qw, a_kvw = 8 * HEAD_DIM, 2 * HEAD_DIM
    b_q0 = a_qw + 2 * a_kvw

    cc = jnp.zeros((16, d), F32).at[:b].set(c).at[b].set(c_ctx)
    mod = _ada_call(cc, ada_w, ada_b)[:, :b + 1].reshape(2, b + 1, 6, d)
    rope, rope_t = _rope_tables(n)

    qscale = scale * LOG2E
    cols0 = jnp.arange(even_w_in.shape[2])
    qcols0 = (cols0 < a_qw) | ((cols0 >= b_q0) & (cols0 < b_q0 + a_qw))
    w_in0 = (even_w_in[0] * jnp.where(qcols0, qscale, 1.0)).astype(BF16)
    cols1 = jnp.arange(odd_w_in.shape[2])
    w_in1 = (odd_w_in[0] * jnp.where(cols1 < 1024, qscale, 1.0)).astype(BF16)
    w_out0, w_out1 = even_w_out[0].astype(BF16), odd_w_out[0].astype(BF16)
    w1, w2 = mlp_w1.astype(BF16), mlp_w2.astype(BF16)

    ak0, av0, bq0, bk0, bv0 = a_qw, a_qw + a_kvw, b_q0, b_q0 + a_qw, b_q0 + a_qw + a_kvw
    wk0 = jnp.concatenate([w_in0[:, ak0:av0], w_in0[:, bk0:bv0]], axis=1)
    wt0 = jnp.concatenate([w_in0[:, :ak0], w_in0[:, av0:bq0], w_in0[:, bq0:bk0], w_in0[:, bv0:]], axis=1).T
    aq_t, ak, av_t, bq_t, bk, bv_t = _project0(x, mod[0], norm1_g[0], wk0, wt0, rope, rope_t, 512, WINDOW)
    pc = _project(ctx, mod[0], b, norm1_g[0], w_in0, None, n_ctx, "proj0_ctx")
    cak, cav_t = pc[:, :, ak0:av0], pc[:, :, av0:bq0].transpose(0, 2, 1)
    cbk, cbv_t = pc[:, :, bk0:bv0], pc[:, :, bv0:].transpose(0, 2, 1)
    sink = a_sink[0].astype(F32) * LOG2E
    ya = _window_attn(sink, aq_t, ak, av_t, cak, cav_t)
    yb = _neighbourhood_attn(bq_t, bk, bv_t, cbk, cbv_t, _na_bias(b_rpb[0].astype(F32)))
    caq, cbq = _heads(pc, 0, ak0), _heads(pc, bq0, bk0)
    sink_ab = jnp.concatenate([sink, jnp.full_like(sink, NEG)])
    yc = _unheads(_ctx_attn(sink_ab, jnp.concatenate([caq, cbq], axis=1),
                            jnp.concatenate([_heads(pc, ak0, av0), _heads(pc, bk0, bv0)], axis=1),
                            jnp.concatenate([_heads(pc, av0, bq0), _heads(pc, bv0, bv0 + a_kvw)], axis=1)))
    hx = _out_mlp([ya, yb], x, mod[0], None, norm2_g[0], w_out0, w1[0], w2[0], None, 512, "mlp0_x")
    hc = _out_mlp([yc], ctx, mod[0], b, norm2_g[0], w_out0, w1[0], w2[0], None, n_ctx, "mlp0_ctx")

    lam_init = 0.8 - 0.6 * math.exp(-0.3 * 1)
    tk = n_ctx
    wq1, wk1, wv1 = w_in1[:, :1024], w_in1[:, 1024:2048], w_in1[:, 2048:]
    qt, kk, vt = _project1(hx, mod[1], None, norm1_g[1], wk1, jnp.concatenate([wq1, wv1], axis=1).T,
                           rope, rope_t, 512, tk, "proj1_x")
    kkc, vtc = _project1(hc, mod[1], b, norm1_g[1], wk1, wv1.T, None, None, n_ctx, tk, "proj1_ctx")
    lam_vecs = [v[0].reshape(1, HEAD_DIM).astype(F32) for v in (lam_q1, lam_k1, lam_q2, lam_k2)]
    yx = _diff_attn(lam_vecs, subln_g[0].astype(F32), qt, kk, kkc, vt, vtc, lam_init, 512)
    return _out_mlp([yx], hx, mod[1], None, norm2_g[1], w_out1, w1[1], w2[1], final_g, 512, "mlp1_x")
```

```python
import functools
import math

import jax
import jax.numpy as jnp
from jax import lax
from jax.experimental import pallas as pl
from jax.experimental.pallas import tpu as pltpu

F32 = jnp.float32
BF16 = jnp.bfloat16

GRID_W = 64
HEAD_DIM = 64
WINDOW = 128
NA_ROWS = 8
NA_COLS = 16
ROPE_THETA = 10000.0
EPS = 1e-6
NEG = -1e30
LOG2E = math.log2(math.e)

V7X_VMEM_BYTES = 64 * 1024 * 1024
LANES = 128


def _vmem(nbytes):
    return pltpu.CompilerParams(vmem_limit_bytes=int(min(nbytes, V7X_VMEM_BYTES - 8 * 1024 * 1024)))


def _const_spec(shape):
    n = len(shape)
    return pl.BlockSpec(shape, lambda *_: (0,) * n, pipeline_mode=pl.Buffered(1))


def _dot(a, b):
    return jnp.dot(a, b, preferred_element_type=F32)


def _dot_nt(a, b):
    return lax.dot_general(a, b, (((1,), (1,)), ((), ())), preferred_element_type=F32)


def _ada_kernel(c_ref, w_ref, b_ref, o_ref):
    cc = c_ref[...]
    s = cc * jax.nn.sigmoid(cc)
    o_ref[0] = jnp.dot(s, w_ref[0], preferred_element_type=F32, precision=lax.Precision.HIGHEST) + b_ref[0]


def _ada_call(cc, ada_w, ada_b):
    depth, d, six_d = ada_w.shape
    tn = 1536
    return pl.pallas_call(
        _ada_kernel,
        grid=(depth, six_d // tn),
        in_specs=[
            pl.BlockSpec((16, d), lambda l, j: (0, 0)),
            pl.BlockSpec((1, d, tn), lambda l, j: (l, 0, j)),
            pl.BlockSpec((1, 1, tn), lambda l, j: (l, 0, j)),
        ],
        out_specs=pl.BlockSpec((1, 16, tn), lambda l, j: (l, 0, j)),
        out_shape=jax.ShapeDtypeStruct((depth, 16, six_d), F32),
        compiler_params=_vmem(32 * 1024 * 1024),
        name="ada_mod",
    )(cc, ada_w, ada_b.reshape(depth, 1, six_d))


def _modulated(x, g, shift, scale):
    ms = jnp.mean(x * x, axis=-1, keepdims=True)
    return (x * lax.rsqrt(ms + EPS) * g) * (1.0 + scale) + shift


def _proj_kernel(*refs, rope_chunks, n_chunk):
    if rope_chunks:
        h_ref, mod_ref, g_ref, w_ref, cos_ref, sa_ref, sb_ref, o_ref = refs
    else:
        h_ref, mod_ref, g_ref, w_ref, o_ref = refs
    a = _modulated(h_ref[0], g_ref[...], mod_ref[0, 0:1, :], mod_ref[0, 1:2, :]).astype(BF16)
    nout = o_ref.shape[-1]
    for c0 in range(0, nout, n_chunk):
        p = _dot(a, w_ref[:, c0:c0 + n_chunk])
        for j in range(n_chunk // LANES):
            lo = c0 + j * LANES
            xc = p[:, j * LANES:(j + 1) * LANES]
            if lo < rope_chunks * LANES:
                xc = (xc * cos_ref[...] + pltpu.roll(xc, LANES - 16, 1) * sa_ref[...]
                      + pltpu.roll(xc, 16, 1) * sb_ref[...])
            o_ref[0, :, lo:lo + LANES] = xc.astype(BF16)


def _project(h, mod9, mod_row, g, w, rope, tm, name):
    b, n, d = h.shape
    nout = w.shape[1]
    rope_chunks = 0 if rope is None else rope[0]
    n_chunk = 512
    if mod_row is None:
        mod_map = lambda bi, i: (bi, 0, 0)
    else:
        mod_map = lambda bi, i: (mod_row, 0, 0)
    in_specs = [
        pl.BlockSpec((1, tm, d), lambda bi, i: (bi, i, 0)),
        pl.BlockSpec((1, 6, d), mod_map),
        _const_spec((1, d)),
        _const_spec((d, nout)),
    ]
    args = [h, mod9, g.reshape(1, d), w]
    if rope_chunks:
        in_specs += [pl.BlockSpec((tm, LANES), lambda bi, i: (i, 0))] * 3
        args += list(rope[1:])
    return pl.pallas_call(
        functools.partial(_proj_kernel, rope_chunks=rope_chunks, n_chunk=n_chunk),
        grid=(b, n // tm),
        in_specs=in_specs,
        out_specs=pl.BlockSpec((1, tm, nout), lambda bi, i: (bi, i, 0)),
        out_shape=jax.ShapeDtypeStruct((b, n, nout), BF16),
        compiler_params=_vmem(48 * 1024 * 1024),
        name=name,
    )(*args)


def _rope_rows(x, cos_t, sin_t):
    q = HEAD_DIM // 4
    rot = jnp.concatenate([x[q:2 * q], x[0:q], x[3 * q:4 * q], x[2 * q:3 * q]], axis=0)
    return x * cos_t + rot * sin_t


def _proj1_kernel(*refs, with_q, tk):
    if with_q:
        (h_ref, mod_ref, g_ref, wk_ref, wt_ref, cos_ref, sa_ref, sb_ref, cost_ref, sint_ref,
         qt_ref, kk_ref, vt_ref) = refs
    else:
        h_ref, mod_ref, g_ref, wk_ref, wt_ref, kk_ref, vt_ref = refs
    a = _modulated(h_ref[0], g_ref[...], mod_ref[0, 0:1, :], mod_ref[0, 1:2, :]).astype(BF16)
    tm = a.shape[0]
    dv = 2 * HEAD_DIM
    n_heads = kk_ref.shape[1]
    chunk = 4 * dv
    for c0 in range(0, n_heads * dv, chunk):
        p = _dot(a, wk_ref[:, c0:c0 + chunk])
        for j in range(chunk // dv):
            xc = p[:, j * dv:(j + 1) * dv]
            if with_q:
                xc = (xc * cos_ref[...] + pltpu.roll(xc, LANES - 16, 1) * sa_ref[...]
                      + pltpu.roll(xc, 16, 1) * sb_ref[...])
            kk_ref[0, c0 // dv + j] = xc.astype(BF16)
    n_q = n_heads * dv if with_q else 0
    for f0 in range(0, wt_ref.shape[0], chunk):
        xt = _dot_nt(wt_ref[f0:f0 + chunk, :], a)
        for j in range(chunk // dv):
            row = f0 + j * dv
            hd = xt[j * dv:(j + 1) * dv]
            if row < n_q:
                hd = jnp.concatenate([_rope_rows(hd[m * HEAD_DIM:(m + 1) * HEAD_DIM], cost_ref[...], sint_ref[...])
                                      for m in range(2)], axis=0)
                qt_ref[0, row // dv] = hd.astype(BF16)
            else:
                for t in range(tm // tk):
                    vt_ref[0, (row - n_q) // dv, t] = hd[:, t * tk:(t + 1) * tk].astype(BF16)


def _project1(h, mod9, mod_row, g, wk, wt, rope, rope_t, tm, tk, name):
    b, n, d = h.shape
    dv = 2 * HEAD_DIM
    n_heads = wk.shape[1] // dv
    with_q = rope is not None
    if mod_row is None:
        mod_map = lambda bi, i: (bi, 0, 0)
    else:
        mod_map = lambda bi, i: (mod_row, 0, 0)
    in_specs = [
        pl.BlockSpec((1, tm, d), lambda bi, i: (bi, i, 0)),
        pl.BlockSpec((1, 6, d), mod_map),
        _const_spec((1, d)),
        _const_spec(wk.shape),
        _const_spec(wt.shape),
    ]
    args = [h, mod9, g.reshape(1, d), wk, wt]
    out_specs = [
        pl.BlockSpec((1, n_heads, tm, dv), lambda bi, i: (bi, 0, i, 0)),
        pl.BlockSpec((1, n_heads, tm // tk, dv, tk), lambda bi, i: (bi, 0, i, 0, 0)),
    ]
    out_shape = [
        jax.ShapeDtypeStruct((b, n_heads, n, dv), BF16),
        jax.ShapeDtypeStruct((b, n_heads, n // tk, dv, tk), BF16),
    ]
    if with_q:
        in_specs += [pl.BlockSpec((tm, LANES), lambda bi, i: (i, 0))] * 3
        in_specs += [pl.BlockSpec((HEAD_DIM, tm), lambda bi, i: (0, i))] * 2
        args += list(rope) + list(rope_t)
        out_specs.insert(0, pl.BlockSpec((1, n_heads, dv, tm), lambda bi, i: (bi, 0, 0, i)))
        out_shape.insert(0, jax.ShapeDtypeStruct((b, n_heads, dv, n), BF16))
    return pl.pallas_call(
        functools.partial(_proj1_kernel, with_q=with_q, tk=tk),
        grid=(b, n // tm),
        in_specs=in_specs,
        out_specs=out_specs,
        out_shape=out_shape,
        compiler_params=_vmem(48 * 1024 * 1024),
        name=name,
    )(*args)


def _proj0_kernel(h_ref, mod_ref, g_ref, wk_ref, wt_ref, cos_ref, sa_ref, sb_ref, cost_ref, sint_ref,
                  qa_ref, ka_ref, va_ref, qb_ref, kb_ref, vb_ref, *, tw):
    a = _modulated(h_ref[0], g_ref[...], mod_ref[0, 0:1, :], mod_ref[0, 1:2, :]).astype(BF16)
    tm = a.shape[0]
    kw = ka_ref.shape[-1]
    p = _dot(a, wk_ref[...])
    xa = p[:, :kw]
    xa = xa * cos_ref[...] + pltpu.roll(xa, LANES - 16, 1) * sa_ref[...] + pltpu.roll(xa, 16, 1) * sb_ref[...]
    ka_ref[0] = xa.astype(BF16)
    kb_ref[0] = p[:, kw:].astype(BF16)
    qw = qa_ref.shape[2]
    rows = qw + va_ref.shape[2]
    for mixer, (q_ref, v_ref) in enumerate(((qa_ref, va_ref), (qb_ref, vb_ref))):
        xt = _dot_nt(wt_ref[mixer * rows:(mixer + 1) * rows, :], a)
        q = xt[:qw]
        if mixer == 0:
            q = jnp.concatenate([_rope_rows(q[h * HEAD_DIM:(h + 1) * HEAD_DIM], cost_ref[...], sint_ref[...])
                                 for h in range(qw // HEAD_DIM)], axis=0)
        for t in range(tm // tw):
            q_ref[0, t] = q[:, t * tw:(t + 1) * tw].astype(BF16)
            v_ref[0, t] = xt[qw:, t * tw:(t + 1) * tw].astype(BF16)


def _project0(h, mod9, g, wk, wt, rope, rope_t, tm, tw):
    b, n, d = h.shape
    kw = wk.shape[1] // 2
    rows = wt.shape[0] // 2
    qw = rows - kw
    q_spec = pl.BlockSpec((1, tm // tw, qw, tw), lambda bi, i: (bi, i, 0, 0))
    k_spec = pl.BlockSpec((1, tm, kw), lambda bi, i: (bi, i, 0))
    v_spec = pl.BlockSpec((1, tm // tw, kw, tw), lambda bi, i: (bi, i, 0, 0))
    q_shape = jax.ShapeDtypeStruct((b, n // tw, qw, tw), BF16)
    k_shape = jax.ShapeDtypeStruct((b, n, kw), BF16)
    v_shape = jax.ShapeDtypeStruct((b, n // tw, kw, tw), BF16)
    return pl.pallas_call(
        functools.partial(_proj0_kernel, tw=tw),
        grid=(b, n // tm),
        in_specs=[
            pl.BlockSpec((1, tm, d), lambda bi, i: (bi, i, 0)),
            pl.BlockSpec((1, 6, d), lambda bi, i: (bi, 0, 0)),
            _const_spec((1, d)),
            _const_spec(wk.shape),
            _const_spec(wt.shape),
        ] + [pl.BlockSpec((tm, LANES), lambda bi, i: (i, 0))] * 3
          + [pl.BlockSpec((HEAD_DIM, tm), lambda bi, i: (0, i))] * 2,
        out_specs=[q_spec, k_spec, v_spec, q_spec, k_spec, v_spec],
        out_shape=[q_shape, k_shape, v_shape, q_shape, k_shape, v_shape],
        compiler_params=_vmem(48 * 1024 * 1024),
        name="proj0_x",
    )(h, mod9, g.reshape(1, d), wk, wt, *rope, *rope_t)


def _softmax_parts(scores, extra_col=None):
    m = functools.reduce(jnp.maximum, [jnp.max(s, axis=-1, keepdims=True) for s in scores])
    if extra_col is not None:
        m = jnp.maximum(m, extra_col)
    ps = [jnp.exp2(s - m) for s in scores]
    l = functools.reduce(jnp.add, [jnp.sum(p, axis=-1, keepdims=True) for p in ps])
    if extra_col is not None:
        l = l + jnp.exp2(extra_col - m)
    return ps, l


def _group_sink_col(sink_ref, base, rows_per_head):
    rows = lax.broadcasted_iota(jnp.int32, (4 * rows_per_head, 1), 0) // rows_per_head
    col = jnp.full((4 * rows_per_head, 1), sink_ref[base + 3], F32)
    for g in range(3):
        col = jnp.where(rows == g, sink_ref[base + g], col)
    return col


def _pipelined(items, ahead):
    scores = [items[k][0]() for k in range(min(ahead, len(items)))]
    for k in range(len(items)):
        if k + ahead < len(items):
            scores.append(items[k + ahead][0]())
        items[k][1](scores[k])
        scores[k] = None


def _column_softmax(pieces, extra_row=None):
    m = functools.reduce(jnp.maximum, [jnp.max(s, axis=0, keepdims=True) for s in pieces])
    if extra_row is not None:
        m = jnp.maximum(m, extra_row)
    ps = [jnp.exp2(s - m) for s in pieces]
    l = functools.reduce(jnp.add, [jnp.sum(p, axis=0, keepdims=True) for p in ps])
    if extra_row is not None:
        l = l + jnp.exp2(extra_row - m)
    return jnp.concatenate([p.astype(BF16) for p in ps], axis=0), l


def _padded_queries(top, h):
    z = jnp.zeros_like(top)
    return jnp.concatenate([top, z] if h == 0 else [z, top], axis=0)


def _win_kernel(sink_ref, q_ref, k_ref, v_ref, kc_ref, vc_ref, o_ref, *, blk, per_iter):
    n = k_ref.shape[1]
    nb = n // blk
    c_idx = lax.broadcasted_iota(jnp.int32, (blk, blk), 0)
    r_idx = lax.broadcasted_iota(jnp.int32, (blk, blk), 1)
    lower = jnp.tile(jnp.where(c_idx >= r_idx, 0.0, NEG), (1, 4))
    upper = jnp.tile(jnp.where(c_idx <= r_idx, 0.0, NEG), (1, 4))
    lane_head = lax.broadcasted_iota(jnp.int32, (1, 4 * blk), 1) // blk

    def sink_row(h):
        row = jnp.full((1, 4 * blk), sink_ref[4 * h + 3], F32)
        for g in range(3):
            row = jnp.where(lane_head == g, sink_ref[4 * h + g], row)
        return row

    sinks = [sink_row(0), sink_row(1)]

    def make_item(i, h, k_rows, v_tiles, biases):
        def score():
            top = jnp.concatenate([q_ref[0, i, (4 * h + g) * HEAD_DIM:(4 * h + g + 1) * HEAD_DIM, :]
                                   for g in range(4)], axis=1)
            return _dot(jnp.concatenate([k_rows(), kc_ref[0]], axis=0), _padded_queries(top, h))

        def finish(s):
            pieces = [s[t * blk:(t + 1) * blk] if bias is None else s[t * blk:(t + 1) * blk] + bias
                      for t, bias in enumerate(biases)]
            pieces.append(s[len(biases) * blk:])
            p, l = _column_softmax(pieces, sinks[h])
            vall = jnp.concatenate(v_tiles() + [vc_ref[0]], axis=1)
            o = _dot(vall, p)[h * HEAD_DIM:(h + 1) * HEAD_DIM] / l
            o4 = jnp.concatenate([o[:, g * blk:(g + 1) * blk] for g in range(4)], axis=0)
            q0 = i * blk if isinstance(i, int) else pl.multiple_of(i * blk, blk)
            o_ref[0, pl.ds(q0, blk), 4 * h * HEAD_DIM:4 * (h + 1) * HEAD_DIM] = o4.T.astype(BF16)

        return score, finish

    def edge_items(i, t0, biases):
        return [make_item(i, h, lambda: k_ref[0, t0 * blk:(t0 + 2) * blk, :],
                          lambda: [v_ref[0, t0], v_ref[0, t0 + 1]], biases) for h in range(2)]

    def inner_items(i):
        k0 = pl.multiple_of((i - 1) * blk, blk)
        return [make_item(i, h, lambda: k_ref[0, pl.ds(k0, 3 * blk), :],
                          lambda: [v_ref[0, i - 1], v_ref[0, i], v_ref[0, i + 1]], [lower, None, upper])
                for h in range(2)]

    _pipelined(edge_items(0, 0, [None, upper]), 1)

    def body(j, carry):
        _pipelined([it for u in range(per_iter) for it in inner_items(1 + per_iter * j + u)], 2)
        return carry

    lax.fori_loop(0, (nb - 2) // per_iter, body, 0)
    _pipelined(edge_items(nb - 1, nb - 2, [lower, None]), 1)


def _window_attn(sink, q_t, k, v_t, kc, vc_t):
    b, nb, hd, blk = q_t.shape
    n = k.shape[1]
    c = kc.shape[1]
    per_iter = 3
    assert blk == WINDOW and (nb - 2) % per_iter == 0
    return pl.pallas_call(
        functools.partial(_win_kernel, blk=blk, per_iter=per_iter),
        grid=(b,),
        in_specs=[
            pl.BlockSpec(memory_space=pltpu.SMEM),
            pl.BlockSpec((1, nb, hd, blk), lambda bi: (bi, 0, 0, 0)),
            pl.BlockSpec((1, n, 2 * HEAD_DIM), lambda bi: (bi, 0, 0)),
            pl.BlockSpec((1, nb, 2 * HEAD_DIM, blk), lambda bi: (bi, 0, 0, 0)),
            pl.BlockSpec((1, c, 2 * HEAD_DIM), lambda bi: (bi, 0, 0)),
            pl.BlockSpec((1, 2 * HEAD_DIM, c), lambda bi: (bi, 0, 0)),
        ],
        out_specs=pl.BlockSpec((1, n, hd), lambda bi: (bi, 0, 0)),
        out_shape=jax.ShapeDtypeStruct((b, n, hd), BF16),
        compiler_params=_vmem(48 * 1024 * 1024),
        name="window_attn",
    )(sink, q_t, k, v_t, kc, vc_t)


def _na_bias_kernel(rpb_ref, o_ref):
    h = pl.program_id(0)
    n_roff = 2 * NA_ROWS - 1
    n_coff = 2 * NA_COLS - 1
    shape = (GRID_W, 4 * GRID_W)
    ck = lax.broadcasted_iota(jnp.int32, shape, 0)
    lane = lax.broadcasted_iota(jnp.int32, shape, 1)
    cq = lane % GRID_W
    head = lane // GRID_W
    cstart = jnp.clip(cq - NA_COLS // 2, 0, GRID_W - NA_COLS)
    valid = (ck >= cstart) & (ck < cstart + NA_COLS)
    diff = ck - cq + (NA_COLS - 1)
    hits = [[valid & (diff == o) & (head == g) for o in range(n_coff)] for g in range(4)]
    tiles = []
    for roff in range(n_roff):
        t = jnp.full(shape, NEG, F32)
        for g in range(4):
            for o in range(n_coff):
                t = jnp.where(hits[g][o], rpb_ref[((4 * h + g) * n_roff + roff) * n_coff + o] * LOG2E, t)
        tiles.append(t)
    for ro in range(n_roff - 1):
        o_ref[0, ro] = jnp.concatenate([tiles[ro], tiles[ro + 1]], axis=0)


def _na_bias(rpb):
    hq, n_roff, n_coff = rpb.shape
    return pl.pallas_call(
        _na_bias_kernel,
        grid=(hq // 4,),
        in_specs=[pl.BlockSpec(memory_space=pltpu.SMEM)],
        out_specs=pl.BlockSpec((1, n_roff - 1, 2 * GRID_W, 4 * GRID_W), lambda h: (h, 0, 0, 0)),
        out_shape=jax.ShapeDtypeStruct((hq // 4, n_roff - 1, 2 * GRID_W, 4 * GRID_W), F32),
        name="na_bias",
    )(rpb.reshape(-1))


def _na_kernel(q_ref, k_ref, v_ref, kc_ref, vc_ref, bias_ref, o_ref, vall_ref):
    n = k_ref.shape[1]
    rows = n // GRID_W
    n_tiles = rows // 2
    kr = min(NA_ROWS, rows)
    nkey = kr * GRID_W
    lo = lax.broadcasted_iota(jnp.int32, (2 * HEAD_DIM, 2 * GRID_W), 1) < GRID_W
    lo_q = lax.broadcasted_iota(jnp.int32, (HEAD_DIM, 2 * GRID_W), 1) < GRID_W

    def swap(x):
        return pltpu.roll(x, GRID_W, 1)

    def build(t, carry):
        a = v_ref[0, t]
        vall_ref[0, t] = a
        nxt = v_ref[0, jnp.minimum(t + 1, n_tiles - 1)]
        vall_ref[1, t] = jnp.where(lo, swap(a.astype(F32)), swap(nxt.astype(F32))).astype(BF16)
        return carry

    lax.fori_loop(0, n_tiles, build, 0)

    def row_start(r):
        return jnp.clip(r - kr // 2, 0, rows - kr)

    def make_item(t, par, h, done):
        r = 2 * t + par

        def score():
            cols = []
            for c in range(2):
                g0 = 4 * h + 2 * c
                a0 = q_ref[0, t, g0 * HEAD_DIM:(g0 + 1) * HEAD_DIM, :].astype(F32)
                a1 = q_ref[0, t, (g0 + 1) * HEAD_DIM:(g0 + 2) * HEAD_DIM, :].astype(F32)
                cols.append(jnp.where(lo_q, a0, swap(a1)) if par == 0 else jnp.where(lo_q, swap(a0), a1))
            top = jnp.concatenate(cols, axis=1).astype(BF16)
            k0 = pl.multiple_of(row_start(r) * GRID_W, GRID_W)
            return _dot(jnp.concatenate([k_ref[0, pl.ds(k0, nkey), :], kc_ref[0]], axis=0), _padded_queries(top, h))

        def finish(s):
            rs = row_start(r)
            ro0 = rs - r + (NA_ROWS - 1)
            bias = jnp.concatenate([bias_ref[h, ro0 + 2 * i] for i in range(kr // 2)], axis=0)
            p, l = _column_softmax([s[:nkey] + bias, s[nkey:]])
            vwin = [vall_ref[rs & 1, (rs >> 1) + i] for i in range(nkey // (2 * GRID_W))]
            o = _dot(jnp.concatenate(vwin + [vc_ref[0]], axis=1), p)[h * HEAD_DIM:(h + 1) * HEAD_DIM] / l
            done.append(o)
            if par == 1:
                o_even, o_odd = done
                ws = []
                for c in range(2):
                    ce = o_even[:, c * 2 * GRID_W:(c + 1) * 2 * GRID_W]
                    co = o_odd[:, c * 2 * GRID_W:(c + 1) * 2 * GRID_W]
                    ws.append(jnp.where(lo_q, ce, swap(co)))
                    ws.append(jnp.where(lo_q, swap(ce), co))
                w = jnp.concatenate(ws, axis=0)
                q0 = pl.multiple_of(t * 2 * GRID_W, 2 * GRID_W)
                o_ref[0, pl.ds(q0, 2 * GRID_W), 4 * h * HEAD_DIM:4 * (h + 1) * HEAD_DIM] = w.T.astype(BF16)

        return score, finish

    tiles_per_iter = 2

    def body(j, carry):
        items = []
        for u in range(tiles_per_iter):
            for h in range(2):
                done = []
                items += [make_item(tiles_per_iter * j + u, par, h, done) for par in range(2)]
        _pipelined(items, 2)
        return carry

    lax.fori_loop(0, n_tiles // tiles_per_iter, body, 0)


def _neighbourhood_attn(q_t, k, v_t, kc, vc_t, bias):
    b, n_tiles, hd, tw = q_t.shape
    n = k.shape[1]
    c = kc.shape[1]
    assert tw == 2 * GRID_W and n_tiles * tw == n
    return pl.pallas_call(
        _na_kernel,
        grid=(b,),
        in_specs=[
            pl.BlockSpec((1, n_tiles, hd, tw), lambda bi: (bi, 0, 0, 0)),
            pl.BlockSpec((1, n, 2 * HEAD_DIM), lambda bi: (bi, 0, 0)),
            pl.BlockSpec((1, n_tiles, 2 * HEAD_DIM, tw), lambda bi: (bi, 0, 0, 0)),
            pl.BlockSpec((1, c, 2 * HEAD_DIM), lambda bi: (bi, 0, 0)),
            pl.BlockSpec((1, 2 * HEAD_DIM, c), lambda bi: (bi, 0, 0)),
            _const_spec(bias.shape),
        ],
        out_specs=pl.BlockSpec((1, n, hd), lambda bi: (bi, 0, 0)),
        out_shape=jax.ShapeDtypeStruct((b, n, hd), BF16),
        scratch_shapes=[pltpu.VMEM((2, n_tiles, 2 * HEAD_DIM, tw), BF16)],
        compiler_params=_vmem(52 * 1024 * 1024),
        name="neighbourhood_attn",
    )(q_t, k, v_t, kc, vc_t, bias)


def _ctx_kernel(sink_ref, q_ref, k_ref, v_ref, o_ref):
    c = k_ref.shape[2]
    sink_col = _group_sink_col(sink_ref, pl.program_id(1) * 4, c)
    q = q_ref[0].reshape(4 * c, HEAD_DIM)
    (p,), l = _softmax_parts([_dot_nt(q, k_ref[0, 0])], sink_col)
    o = _dot(p.astype(BF16), v_ref[0, 0]) / l
    o_ref[0] = o.reshape(4, c, HEAD_DIM).astype(BF16)


def _ctx_attn(sink, q, k, v):
    b, hq, c, d = q.shape
    hkv = k.shape[1]
    return pl.pallas_call(
        _ctx_kernel,
        grid=(b, hkv),
        in_specs=[
            pl.BlockSpec(memory_space=pltpu.SMEM),
            pl.BlockSpec((1, 4, c, d), lambda bi, h: (bi, h, 0, 0)),
            pl.BlockSpec((1, 1, c, d), lambda bi, h: (bi, h, 0, 0)),
            pl.BlockSpec((1, 1, c, d), lambda bi, h: (bi, h, 0, 0)),
        ],
        out_specs=pl.BlockSpec((1, 4, c, d), lambda bi, h: (bi, h, 0, 0)),
        out_shape=jax.ShapeDtypeStruct((b, hq, c, d), BF16),
        name="ctx_attn",
    )(sink, q, k, v)


def _diff_kernel(lq1_ref, lk1_ref, lq2_ref, lk2_ref, g_ref, qt_ref, kk_ref, kkc_ref, vt_ref, vtc_ref, o_ref,
                 qz_ref, kb_ref, kmax_ref, m_ref, l_ref, acc1_ref, acc2_ref, *, lam_init, n_sub):
    tq = qt_ref.shape[-1] // n_sub
    n_lt, dv, tk = vt_ref.shape[2:]
    n_kt = n_lt + 1
    assert kkc_ref.shape[2] == tk and vtc_ref.shape[2] == 1

    def k_tile(j):
        return kk_ref[0, 0, j * tk:(j + 1) * tk, :] if j < n_lt else kkc_ref[0, 0]

    def v_tile(j):
        return vt_ref[0, 0, j] if j < n_lt else vtc_ref[0, 0, 0]

    @pl.when(pl.program_id(2) == 0)
    def _():
        def group_max(kt):
            kf = kt.astype(F32)
            return jnp.max((kf * kf).reshape(tk // 8, 8, dv), axis=0)

        def kbody(j, carry):
            kb_ref[j] = group_max(kk_ref[0, 0, pl.ds(pl.multiple_of(j * tk, tk), tk), :])
            return carry

        lax.fori_loop(0, n_lt, kbody, 0)
        kb_ref[n_lt] = group_max(kkc_ref[0, 0])
        best = kb_ref[...].reshape(n_kt * 8, dv)
        half = lax.broadcasted_iota(jnp.int32, best.shape, 1) < HEAD_DIM
        k1 = jnp.max(jnp.sum(jnp.where(half, best, 0.0), axis=1, keepdims=True), axis=0, keepdims=True)
        k2 = jnp.max(jnp.sum(jnp.where(half, 0.0, best), axis=1, keepdims=True), axis=0, keepdims=True)
        kmax_ref[:, :tq] = jnp.broadcast_to(jnp.sqrt(k1), (1, tq))
        kmax_ref[:, tq:] = jnp.broadcast_to(jnp.sqrt(k2), (1, tq))

    zero = jnp.zeros((HEAD_DIM, tq), BF16)
    state = []
    for s in range(n_sub):
        qt = qt_ref[0, 0, :, s * tq:(s + 1) * tq]
        qz_ref[s, 0:HEAD_DIM, 0:tq] = qt[0:HEAD_DIM]
        qz_ref[s, 0:HEAD_DIM, tq:] = zero
        qz_ref[s, HEAD_DIM:, 0:tq] = zero
        qz_ref[s, HEAD_DIM:, tq:] = qt[HEAD_DIM:]
        qf = qt.astype(F32)
        qsq = qf * qf
        qn = jnp.concatenate([jnp.sum(qsq[:HEAD_DIM], axis=0, keepdims=True),
                              jnp.sum(qsq[HEAD_DIM:], axis=0, keepdims=True)], axis=1)
        state.append(dict(qz=qz_ref[s], shift=jnp.sqrt(qn) * kmax_ref[...] * (1.0 + 2.0 ** -10),
                          l8=jnp.zeros((8, 2 * tq), F32), acc1=jnp.zeros((dv, tq), F32), acc2=jnp.zeros((dv, tq), F32)))

    lam = (jnp.exp(jnp.sum(lq1_ref[...] * lk1_ref[...], axis=-1, keepdims=True))
           - jnp.exp(jnp.sum(lq2_ref[...] * lk2_ref[...], axis=-1, keepdims=True)) + lam_init)

    def finalize(s, l, acc1, acc2):
        o = acc1 / l[:, :tq] - lam * (acc2 / l[:, tq:])
        ms = jnp.mean(o * o, axis=0, keepdims=True)
        o = (o * lax.rsqrt(ms + EPS) * g_ref[...]) * (1.0 - lam_init)
        o_ref[0, s * tq:(s + 1) * tq, :] = o.T.astype(BF16)

    def make_item(s, j):
        st = state[s]

        def finish(scores):
            p = jnp.exp2(scores - st["shift"])
            st["l8"] = st["l8"] + jnp.sum(p.reshape(tk // 8, 8, 2 * tq), axis=0)
            pb = p.astype(BF16)
            vt = v_tile(j)
            st["acc1"] = st["acc1"] + _dot(vt, pb[:, :tq])
            st["acc2"] = st["acc2"] + _dot(vt, pb[:, tq:])
            if j == n_kt - 1:
                l_fast = jnp.sum(st["l8"], axis=0, keepdims=True)
                l_ref[s] = l_fast
                finalize(s, l_fast, st["acc1"], st["acc2"])

        return (lambda: _dot(k_tile(j), st["qz"])), finish

    _pipelined([make_item(s, j) for s in range(n_sub) for j in range(n_kt)], 2)
    l_min = functools.reduce(jnp.minimum, [l_ref[s] for s in range(n_sub)])

    @pl.when(jnp.min(l_min) < 2.0 ** -80)
    def _():
        for s in range(n_sub):
            m_ref[...] = jnp.full(m_ref.shape, NEG, F32)
            l_ref[s] = jnp.zeros((1, 2 * tq), F32)
            acc1_ref[s] = jnp.zeros((dv, tq), F32)
            acc2_ref[s] = jnp.zeros((dv, tq), F32)

            def step(kt, vt):
                st = _dot(kt, qz_ref[s])
                m_old = m_ref[...]
                m_new = jnp.maximum(m_old, jnp.max(st, axis=0, keepdims=True))
                alpha = jnp.exp2(m_old - m_new)
                p = jnp.exp2(st - m_new)
                l_ref[s] = alpha * l_ref[s] + jnp.sum(p, axis=0, keepdims=True)
                m_ref[...] = m_new
                pb = p.astype(BF16)
                acc1_ref[s] = alpha[:, :tq] * acc1_ref[s] + _dot(vt, pb[:, :tq])
                acc2_ref[s] = alpha[:, tq:] * acc2_ref[s] + _dot(vt, pb[:, tq:])

            def body(j, carry):
                step(kk_ref[0, 0, pl.ds(pl.multiple_of(j * tk, tk), tk), :], vt_ref[0, 0, j])
                return carry

            lax.fori_loop(0, n_lt, body, 0)
            step(kkc_ref[0, 0], vtc_ref[0, 0, 0])
            finalize(s, l_ref[s], acc1_ref[s], acc2_ref[s])


def _diff_attn(lam_vecs, sub_g, qt, kk, kkc, vt, vtc, lam_init, tq_block, n_sub):
    b, h, dv, n = qt.shape
    n_ctx = kkc.shape[2]
    n_lt, _, tk = vt.shape[2:]
    tq = tq_block // n_sub
    vec_spec = pl.BlockSpec((1, HEAD_DIM), lambda bi, hi, i: (0, 0))
    return pl.pallas_call(
        functools.partial(_diff_kernel, lam_init=lam_init, n_sub=n_sub),
        grid=(b, h, n // tq_block),
        in_specs=[vec_spec] * 4 + [
            pl.BlockSpec((dv, 1), lambda bi, hi, i: (0, 0)),
            pl.BlockSpec((1, 1, dv, tq_block), lambda bi, hi, i: (bi, hi, 0, i)),
            pl.BlockSpec((1, 1, n, dv), lambda bi, hi, i: (bi, hi, 0, 0)),
            pl.BlockSpec((1, 1, n_ctx, dv), lambda bi, hi, i: (bi, hi, 0, 0)),
            pl.BlockSpec((1, 1, n_lt, dv, tk), lambda bi, hi, i: (bi, hi, 0, 0, 0)),
            pl.BlockSpec((1, 1, 1, dv, tk), lambda bi, hi, i: (bi, hi, 0, 0, 0)),
        ],
        out_specs=pl.BlockSpec((1, tq_block, dv), lambda bi, hi, i: (bi, i, hi)),
        out_shape=jax.ShapeDtypeStruct((b, n, h * dv), BF16),
        scratch_shapes=[
            pltpu.VMEM((n_sub, dv, 2 * tq), BF16),
            pltpu.VMEM((n_lt + 1, 8, dv), F32),
            pltpu.VMEM((1, 2 * tq), F32),
            pltpu.VMEM((1, 2 * tq), F32),
            pltpu.VMEM((n_sub, 1, 2 * tq), F32),
            pltpu.VMEM((n_sub, dv, tq), F32),
            pltpu.VMEM((n_sub, dv, tq), F32),
        ],
        compiler_params=pltpu.CompilerParams(
            dimension_semantics=("arbitrary", "arbitrary", "arbitrary"), vmem_limit_bytes=32 * 1024 * 1024),
        name="diff_attn",
    )(*lam_vecs, sub_g.reshape(dv, 1), qt, kk, kkc, vt, vtc)


def _mlp_kernel(*refs, n_y, final, ff_chunk):
    y_refs, refs = refs[:n_y], refs[n_y:]
    if final:
        h_ref, mod_ref, g_ref, wo_ref, w1_ref, w2_ref, fg_ref, o_ref = refs
    else:
        h_ref, mod_ref, g_ref, wo_ref, w1_ref, w2_ref, o_ref = refs
    yw = wo_ref.shape[0] // n_y
    y = functools.reduce(jnp.add, [_dot(y_ref[0], wo_ref[j * yw:(j + 1) * yw, :]) for j, y_ref in enumerate(y_refs)])
    hx = h_ref[0] + mod_ref[0, 2:3, :] * y
    a = _modulated(hx, g_ref[...], mod_ref[0, 3:4, :], mod_ref[0, 4:5, :]).astype(BF16)
    acc = jnp.zeros(hx.shape, F32)
    for c0 in range(0, w1_ref.shape[1], ff_chunk):
        u = jnp.maximum(_dot(a, w1_ref[:, c0:c0 + ff_chunk]), 0.0)
        acc = acc + _dot((u * u).astype(BF16), w2_ref[c0:c0 + ff_chunk, :])
    out = hx + mod_ref[0, 5:6, :] * acc
    if final:
        ms = jnp.mean(out * out, axis=-1, keepdims=True)
        out = out * lax.rsqrt(ms + EPS) * fg_ref[...]
    o_ref[0] = out


def _out_mlp(ys, h, mod9, mod_row, g2, wo, w1, w2, final_g, tm, name):
    b, n, d = h.shape
    dff = w1.shape[1]
    final = final_g is not None
    if mod_row is None:
        mod_map = lambda bi, i: (bi, 0, 0)
    else:
        mod_map = lambda bi, i: (mod_row, 0, 0)
    tile = pl.BlockSpec((1, tm, d), lambda bi, i: (bi, i, 0))
    in_specs = [pl.BlockSpec((1, tm, y.shape[-1]), lambda bi, i: (bi, i, 0)) for y in ys]
    in_specs += [tile, pl.BlockSpec((1, 6, d), mod_map), _const_spec((1, d)),
                 _const_spec((d, d)), _const_spec((d, dff)), _const_spec((dff, d))]
    args = list(ys) + [h, mod9, g2.reshape(1, d), wo, w1, w2]
    if final:
        in_specs.append(_const_spec((1, d)))
        args.append(final_g.reshape(1, d))
    return pl.pallas_call(
        functools.partial(_mlp_kernel, n_y=len(ys), final=final, ff_chunk=1024),
        grid=(b, n // tm),
        in_specs=in_specs,
        out_specs=tile,
        out_shape=jax.ShapeDtypeStruct((b, n, d), F32),
        compiler_params=_vmem(56 * 1024 * 1024),
        name=name,
    )(*args)


def _rope_tables(n):
    t = jnp.arange(n, dtype=jnp.int32)
    row = (t // GRID_W).astype(F32)
    col = (t % GRID_W).astype(F32)
    quarter = HEAD_DIM // 4
    inv_freq = ROPE_THETA ** (-jnp.arange(quarter, dtype=F32) / quarter)
    ar = row[:, None] * inv_freq[None, :]
    ac = col[:, None] * inv_freq[None, :]
    ang = jnp.concatenate([ar, ar, ac, ac], axis=-1)
    cos, sin = jnp.cos(ang), jnp.sin(ang)
    even = (jnp.arange(HEAD_DIM) // quarter) % 2 == 0
    rep = LANES // HEAD_DIM
    sin_a = jnp.tile(jnp.where(even, -sin, 0.0), (1, rep))
    sin_b = jnp.tile(jnp.where(even, 0.0, sin), (1, rep))
    token_major = (jnp.tile(cos, (1, rep)), sin_a, sin_b)
    feature_major = (cos.T, jnp.where(even, -sin, sin).T)
    return token_major, feature_major


def _heads(p, lo, hi, width=HEAD_DIM):
    b, n, _ = p.shape
    return p[:, :, lo:hi].reshape(b, n, (hi - lo) // width, width).transpose(0, 2, 1, 3)


def _unheads(y):
    b, h, n, d = y.shape
    return y.transpose(0, 2, 1, 3).reshape(b, n, h * d)


def kernel(x, c, ctx, c_ctx, ada_w, ada_b, norm1_g, norm2_g, even_w_in, even_w_out, a_sink, b_rpb,
           odd_w_in, odd_w_out, lam_q1, lam_k1, lam_q2, lam_k2, subln_g, mlp_w1, mlp_w2, final_g):
    b, n, d = x.shape
    n_ctx = ctx.shape[1]
    assert ada_w.shape[0] == 2 and d == 1024 and n % 512 == 0
    scale = HEAD_DIM ** -0.5
    a_qw, a_kvw = 8 * HEAD_DIM, 2 * HEAD_DIM
    b_q0 = a_qw + 2 * a_kvw

    cc = jnp.zeros((16, d), F32).at[:b].set(c).at[b].set(c_ctx)
    mod = _ada_call(cc, ada_w, ada_b)[:, :b + 1].reshape(2, b + 1, 6, d)
    rope, rope_t = _rope_tables(n)

    qscale = scale * LOG2E
    cols0 = jnp.arange(even_w_in.shape[2])
    qcols0 = (cols0 < a_qw) | ((cols0 >= b_q0) & (cols0 < b_q0 + a_qw))
    w_in0 = (even_w_in[0] * jnp.where(qcols0, qscale, 1.0)).astype(BF16)
    cols1 = jnp.arange(odd_w_in.shape[2])
    w_in1 = (odd_w_in[0] * jnp.where(cols1 < 1024, qscale, 1.0)).astype(BF16)
    w_out0, w_out1 = even_w_out[0].astype(BF16), odd_w_out[0].astype(BF16)
    w1, w2 = mlp_w1.astype(BF16), mlp_w2.astype(BF16)

    ak0, av0, bq0, bk0, bv0 = a_qw, a_qw + a_kvw, b_q0, b_q0 + a_qw, b_q0 + a_qw + a_kvw
    wk0 = jnp.concatenate([w_in0[:, ak0:av0], w_in0[:, bk0:bv0]], axis=1)
    wt0 = jnp.concatenate([w_in0[:, :ak0], w_in0[:, av0:bq0], w_in0[:, bq0:bk0], w_in0[:, bv0:]], axis=1).T
    aq_t, ak, av_t, bq_t, bk, bv_t = _project0(x, mod[0], norm1_g[0], wk0, wt0, rope, rope_t, 512, WINDOW)
    pc = _project(ctx, mod[0], b, norm1_g[0], w_in0, None, n_ctx, "proj0_ctx")
    cak, cav_t = pc[:, :, ak0:av0], pc[:, :, av0:bq0].transpose(0, 2, 1)
    cbk, cbv_t = pc[:, :, bk0:bv0], pc[:, :, bv0:].transpose(0, 2, 1)
    sink = a_sink[0].astype(F32) * LOG2E
    ya = _window_attn(sink, aq_t, ak, av_t, cak, cav_t)
    yb = _neighbourhood_attn(bq_t, bk, bv_t, cbk, cbv_t, _na_bias(b_rpb[0].astype(F32)))
    caq, cbq = _heads(pc, 0, ak0), _heads(pc, bq0, bk0)
    sink_ab = jnp.concatenate([sink, jnp.full_like(sink, NEG)])
    yc = _unheads(_ctx_attn(sink_ab, jnp.concatenate([caq, cbq], axis=1),
                            jnp.concatenate([_heads(pc, ak0, av0), _heads(pc, bk0, bv0)], axis=1),
                            jnp.concatenate([_heads(pc, av0, bq0), _heads(pc, bv0, bv0 + a_kvw)], axis=1)))
    hx = _out_mlp([ya, yb], x, mod[0], None, norm2_g[0], w_out0, w1[0], w2[0], None, 512, "mlp0_x")
    hc = _out_mlp([yc], ctx, mod[0], b, norm2_g[0], w_out0, w1[0], w2[0], None, n_ctx, "mlp0_ctx")

    lam_init = 0.8 - 0.6 * math.exp(-0.3 * 1)
    tk = n_ctx
    wq1, wk1, wv1 = w_in1[:, :1024], w_in1[:, 1024:2048], w_in1[:, 2048:]
    qt, kk, vt = _project1(hx, mod[1], None, norm1_g[1], wk1, jnp.concatenate([wq1, wv1], axis=1).T,
                           rope, rope_t, 512, tk, "proj1_x")
    kkc, vtc = _project1(hc, mod[1], b, norm1_g[1], wk1, wv1.T, None, None, n_ctx, tk, "proj1_ctx")
    lam_vecs = [v[0].reshape(1, HEAD_DIM).astype(F32) for v in (lam_q1, lam_k1, lam_q2, lam_k2)]
    yx = _diff_attn(lam_vecs, subln_g[0].astype(F32), qt, kk, kkc, vt, vtc, lam_init, 1024, 4)
    return _out_mlp([yx], hx, mod[1], None, norm2_g[1], w_out1, w1[1], w2[1], final_g, 512, "mlp1_x")
```

```python
import functools
import math

import jax
import jax.numpy as jnp
from jax import lax
from jax.experimental import pallas as pl
from jax.experimental.pallas import tpu as pltpu

F32 = jnp.float32
BF16 = jnp.bfloat16

GRID_W = 64
HEAD_DIM = 64
WINDOW = 128
NA_ROWS = 8
NA_COLS = 16
ROPE_THETA = 10000.0
EPS = 1e-6
NEG = -1e30
LOG2E = math.log2(math.e)

V7X_VMEM_BYTES = 64 * 1024 * 1024
LANES = 128


def _vmem(nbytes):
    return pltpu.CompilerParams(vmem_limit_bytes=int(min(nbytes, V7X_VMEM_BYTES - 8 * 1024 * 1024)))


def _const_spec(shape):
    n = len(shape)
    return pl.BlockSpec(shape, lambda *_: (0,) * n, pipeline_mode=pl.Buffered(1))


def _dot(a, b):
    return jnp.dot(a, b, preferred_element_type=F32)


def _dot_nt(a, b):
    return lax.dot_general(a, b, (((1,), (1,)), ((), ())), preferred_element_type=F32)


def _ada_kernel(c_ref, w_ref, b_ref, o_ref):
    cc = c_ref[...]
    s = cc * jax.nn.sigmoid(cc)
    o_ref[0] = jnp.dot(s, w_ref[0], preferred_element_type=F32, precision=lax.Precision.HIGHEST) + b_ref[0]


def _ada_call(cc, ada_w, ada_b):
    depth, d, six_d = ada_w.shape
    tn = 1536
    return pl.pallas_call(
        _ada_kernel,
        grid=(depth, six_d // tn),
        in_specs=[
            pl.BlockSpec((16, d), lambda l, j: (0, 0)),
            pl.BlockSpec((1, d, tn), lambda l, j: (l, 0, j)),
            pl.BlockSpec((1, 1, tn), lambda l, j: (l, 0, j)),
        ],
        out_specs=pl.BlockSpec((1, 16, tn), lambda l, j: (l, 0, j)),
        out_shape=jax.ShapeDtypeStruct((depth, 16, six_d), F32),
        compiler_params=_vmem(32 * 1024 * 1024),
        name="ada_mod",
    )(cc, ada_w, ada_b.reshape(depth, 1, six_d))


def _modulated(x, g, shift, scale):
    ms = jnp.mean(x * x, axis=-1, keepdims=True)
    return (x * lax.rsqrt(ms + EPS) * g) * (1.0 + scale) + shift


def _proj_kernel(*refs, rope_chunks, n_chunk):
    if rope_chunks:
        h_ref, mod_ref, g_ref, w_ref, cos_ref, sa_ref, sb_ref, o_ref = refs
    else:
        h_ref, mod_ref, g_ref, w_ref, o_ref = refs
    a = _modulated(h_ref[0], g_ref[...], mod_ref[0, 0:1, :], mod_ref[0, 1:2, :]).astype(BF16)
    nout = o_ref.shape[-1]
    for c0 in range(0, nout, n_chunk):
        p = _dot(a, w_ref[:, c0:c0 + n_chunk])
        for j in range(n_chunk // LANES):
            lo = c0 + j * LANES
            xc = p[:, j * LANES:(j + 1) * LANES]
            if lo < rope_chunks * LANES:
                xc = (xc * cos_ref[...] + pltpu.roll(xc, LANES - 16, 1) * sa_ref[...]
                      + pltpu.roll(xc, 16, 1) * sb_ref[...])
            o_ref[0, :, lo:lo + LANES] = xc.astype(BF16)


def _project(h, mod9, mod_row, g, w, rope, tm, name):
    b, n, d = h.shape
    nout = w.shape[1]
    rope_chunks = 0 if rope is None else rope[0]
    n_chunk = 512
    if mod_row is None:
        mod_map = lambda bi, i: (bi, 0, 0)
    else:
        mod_map = lambda bi, i: (mod_row, 0, 0)
    in_specs = [
        pl.BlockSpec((1, tm, d), lambda bi, i: (bi, i, 0)),
        pl.BlockSpec((1, 6, d), mod_map),
        _const_spec((1, d)),
        _const_spec((d, nout)),
    ]
    args = [h, mod9, g.reshape(1, d), w]
    if rope_chunks:
        in_specs += [pl.BlockSpec((tm, LANES), lambda bi, i: (i, 0))] * 3
        args += list(rope[1:])
    return pl.pallas_call(
        functools.partial(_proj_kernel, rope_chunks=rope_chunks, n_chunk=n_chunk),
        grid=(b, n // tm),
        in_specs=in_specs,
        out_specs=pl.BlockSpec((1, tm, nout), lambda bi, i: (bi, i, 0)),
        out_shape=jax.ShapeDtypeStruct((b, n, nout), BF16),
        compiler_params=_vmem(48 * 1024 * 1024),
        name=name,
    )(*args)


def _rope_rows(x, cos_t, sin_t):
    q = HEAD_DIM // 4
    rot = jnp.concatenate([x[q:2 * q], x[0:q], x[3 * q:4 * q], x[2 * q:3 * q]], axis=0)
    return x * cos_t + rot * sin_t


def _proj1_kernel(*refs, with_q, tk):
    if with_q:
        (h_ref, mod_ref, g_ref, wk_ref, wt_ref, cos_ref, sa_ref, sb_ref, cost_ref, sint_ref,
         qt_ref, kk_ref, vt_ref) = refs
    else:
        h_ref, mod_ref, g_ref, wk_ref, wt_ref, kk_ref, vt_ref = refs
    a = _modulated(h_ref[0], g_ref[...], mod_ref[0, 0:1, :], mod_ref[0, 1:2, :]).astype(BF16)
    tm = a.shape[0]
    dv = 2 * HEAD_DIM
    n_heads = kk_ref.shape[1]
    chunk = 4 * dv
    for c0 in range(0, n_heads * dv, chunk):
        p = _dot(a, wk_ref[:, c0:c0 + chunk])
        for j in range(chunk // dv):
            xc = p[:, j * dv:(j + 1) * dv]
            if with_q:
                xc = (xc * cos_ref[...] + pltpu.roll(xc, LANES - 16, 1) * sa_ref[...]
                      + pltpu.roll(xc, 16, 1) * sb_ref[...])
            kk_ref[0, c0 // dv + j] = xc.astype(BF16)
    n_q = n_heads * dv if with_q else 0
    for f0 in range(0, wt_ref.shape[0], chunk):
        xt = _dot_nt(wt_ref[f0:f0 + chunk, :], a)
        for j in range(chunk // dv):
            row = f0 + j * dv
            hd = xt[j * dv:(j + 1) * dv]
            if row < n_q:
                hd = jnp.concatenate([_rope_rows(hd[m * HEAD_DIM:(m + 1) * HEAD_DIM], cost_ref[...], sint_ref[...])
                                      for m in range(2)], axis=0)
                qt_ref[0, row // dv] = hd.astype(BF16)
            else:
                for t in range(tm // tk):
                    vt_ref[0, (row - n_q) // dv, t] = hd[:, t * tk:(t + 1) * tk].astype(BF16)


def _project1(h, mod9, mod_row, g, wk, wt, rope, rope_t, tm, tk, name):
    b, n, d = h.shape
    dv = 2 * HEAD_DIM
    n_heads = wk.shape[1] // dv
    with_q = rope is not None
    if mod_row is None:
        mod_map = lambda bi, i: (bi, 0, 0)
    else:
        mod_map = lambda bi, i: (mod_row, 0, 0)
    in_specs = [
        pl.BlockSpec((1, tm, d), lambda bi, i: (bi, i, 0)),
        pl.BlockSpec((1, 6, d), mod_map),
        _const_spec((1, d)),
        _const_spec(wk.shape),
        _const_spec(wt.shape),
    ]
    args = [h, mod9, g.reshape(1, d), wk, wt]
    out_specs = [
        pl.BlockSpec((1, n_heads, tm, dv), lambda bi, i: (bi, 0, i, 0)),
        pl.BlockSpec((1, n_heads, tm // tk, dv, tk), lambda bi, i: (bi, 0, i, 0, 0)),
    ]
    out_shape = [
        jax.ShapeDtypeStruct((b, n_heads, n, dv), BF16),
        jax.ShapeDtypeStruct((b, n_heads, n // tk, dv, tk), BF16),
    ]
    if with_q:
        in_specs += [pl.BlockSpec((tm, LANES), lambda bi, i: (i, 0))] * 3
        in_specs += [pl.BlockSpec((HEAD_DIM, tm), lambda bi, i: (0, i))] * 2
        args += list(rope) + list(rope_t)
        out_specs.insert(0, pl.BlockSpec((1, n_heads, dv, tm), lambda bi, i: (bi, 0, 0, i)))
        out_shape.insert(0, jax.ShapeDtypeStruct((b, n_heads, dv, n), BF16))
    return pl.pallas_call(
        functools.partial(_proj1_kernel, with_q=with_q, tk=tk),
        grid=(b, n // tm),
        in_specs=in_specs,
        out_specs=out_specs,
        out_shape=out_shape,
        compiler_params=_vmem(48 * 1024 * 1024),
        name=name,
    )(*args)


def _proj0_kernel(h_ref, mod_ref, g_ref, wk_ref, wt_ref, cos_ref, sa_ref, sb_ref, cost_ref, sint_ref,
                  qa_ref, ka_ref, va_ref, qb_ref, kb_ref, vb_ref, *, tw):
    a = _modulated(h_ref[0], g_ref[...], mod_ref[0, 0:1, :], mod_ref[0, 1:2, :]).astype(BF16)
    tm = a.shape[0]
    kw = ka_ref.shape[-1]
    p = _dot(a, wk_ref[...])
    xa = p[:, :kw]
    xa = xa * cos_ref[...] + pltpu.roll(xa, LANES - 16, 1) * sa_ref[...] + pltpu.roll(xa, 16, 1) * sb_ref[...]
    ka_ref[0] = xa.astype(BF16)
    kb_ref[0] = p[:, kw:].astype(BF16)
    qw = qa_ref.shape[2]
    rows = qw + va_ref.shape[2]
    for mixer, (q_ref, v_ref) in enumerate(((qa_ref, va_ref), (qb_ref, vb_ref))):
        xt = _dot_nt(wt_ref[mixer * rows:(mixer + 1) * rows, :], a)
        q = xt[:qw]
        if mixer == 0:
            q = jnp.concatenate([_rope_rows(q[h * HEAD_DIM:(h + 1) * HEAD_DIM], cost_ref[...], sint_ref[...])
                                 for h in range(qw // HEAD_DIM)], axis=0)
        for t in range(tm // tw):
            q_ref[0, t] = q[:, t * tw:(t + 1) * tw].astype(BF16)
            v_ref[0, t] = xt[qw:, t * tw:(t + 1) * tw].astype(BF16)


def _project0(h, mod9, g, wk, wt, rope, rope_t, tm, tw):
    b, n, d = h.shape
    kw = wk.shape[1] // 2
    rows = wt.shape[0] // 2
    qw = rows - kw
    q_spec = pl.BlockSpec((1, tm // tw, qw, tw), lambda bi, i: (bi, i, 0, 0))
    k_spec = pl.BlockSpec((1, tm, kw), lambda bi, i: (bi, i, 0))
    v_spec = pl.BlockSpec((1, tm // tw, kw, tw), lambda bi, i: (bi, i, 0, 0))
    q_shape = jax.ShapeDtypeStruct((b, n // tw, qw, tw), BF16)
    k_shape = jax.ShapeDtypeStruct((b, n, kw), BF16)
    v_shape = jax.ShapeDtypeStruct((b, n // tw, kw, tw), BF16)
    return pl.pallas_call(
        functools.partial(_proj0_kernel, tw=tw),
        grid=(b, n // tm),
        in_specs=[
            pl.BlockSpec((1, tm, d), lambda bi, i: (bi, i, 0)),
            pl.BlockSpec((1, 6, d), lambda bi, i: (bi, 0, 0)),
            _const_spec((1, d)),
            _const_spec(wk.shape),
            _const_spec(wt.shape),
        ] + [pl.BlockSpec((tm, LANES), lambda bi, i: (i, 0))] * 3
          + [pl.BlockSpec((HEAD_DIM, tm), lambda bi, i: (0, i))] * 2,
        out_specs=[q_spec, k_spec, v_spec, q_spec, k_spec, v_spec],
        out_shape=[q_shape, k_shape, v_shape, q_shape, k_shape, v_shape],
        compiler_params=_vmem(48 * 1024 * 1024),
        name="proj0_x",
    )(h, mod9, g.reshape(1, d), wk, wt, *rope, *rope_t)


def _softmax_parts(scores, extra_col=None):
    m = functools.reduce(jnp.maximum, [jnp.max(s, axis=-1, keepdims=True) for s in scores])
    if extra_col is not None:
        m = jnp.maximum(m, extra_col)
    ps = [jnp.exp2(s - m) for s in scores]
    l = functools.reduce(jnp.add, [jnp.sum(p, axis=-1, keepdims=True) for p in ps])
    if extra_col is not None:
        l = l + jnp.exp2(extra_col - m)
    return ps, l


def _group_sink_col(sink_ref, base, rows_per_head):
    rows = lax.broadcasted_iota(jnp.int32, (4 * rows_per_head, 1), 0) // rows_per_head
    col = jnp.full((4 * rows_per_head, 1), sink_ref[base + 3], F32)
    for g in range(3):
        col = jnp.where(rows == g, sink_ref[base + g], col)
    return col


def _pipelined(items, ahead):
    scores = [items[k][0]() for k in range(min(ahead, len(items)))]
    for k in range(len(items)):
        if k + ahead < len(items):
            scores.append(items[k + ahead][0]())
        items[k][1](scores[k])
        scores[k] = None


SHIFT_SLACK = 1.0 + 2.0 ** -10
L_FLOOR = 2.0 ** -80


def _key_norm_bounds(k_ref, kc_ref, kb_ref):
    tile = kc_ref.shape[1]
    nt = k_ref.shape[1] // tile
    width = k_ref.shape[2]

    def group_max(kt):
        kf = kt.astype(F32)
        return jnp.max((kf * kf).reshape(tile // 8, 8, width), axis=0)

    def body(j, carry):
        kb_ref[j] = group_max(k_ref[0, pl.ds(pl.multiple_of(j * tile, tile), tile), :])
        return carry

    lax.fori_loop(0, nt, body, 0)
    kb_ref[nt] = group_max(kc_ref[0])
    best = kb_ref[...].reshape((nt + 1) * 8, width)
    half = lax.broadcasted_iota(jnp.int32, best.shape, 1) < width // 2
    k0 = jnp.max(jnp.sum(jnp.where(half, best, 0.0), axis=1, keepdims=True), axis=0, keepdims=True)
    k1 = jnp.max(jnp.sum(jnp.where(half, 0.0, best), axis=1, keepdims=True), axis=0, keepdims=True)
    return jnp.sqrt(k0), jnp.sqrt(k1)


def _column_softmax(pieces, extra_row=None, shift=None):
    if shift is None:
        m = functools.reduce(jnp.maximum, [jnp.max(s, axis=0, keepdims=True) for s in pieces])
        if extra_row is not None:
            m = jnp.maximum(m, extra_row)
    else:
        m = shift
    ps = [jnp.exp2(s - m) for s in pieces]
    l = functools.reduce(jnp.add, [jnp.sum(p, axis=0, keepdims=True) for p in ps])
    if extra_row is not None:
        l = l + jnp.exp2(extra_row - m)
    return jnp.concatenate([p.astype(BF16) for p in ps], axis=0), l


def _padded_queries(top, h):
    z = jnp.zeros_like(top)
    return jnp.concatenate([top, z] if h == 0 else [z, top], axis=0)


def _win_kernel(sink_ref, q_ref, k_ref, v_ref, kc_ref, vc_ref, o_ref, kb_ref, *, blk, per_iter):
    n = k_ref.shape[1]
    nb = n // blk
    c_idx = lax.broadcasted_iota(jnp.int32, (blk, blk), 0)
    r_idx = lax.broadcasted_iota(jnp.int32, (blk, blk), 1)
    lower = jnp.tile(jnp.where(c_idx >= r_idx, 0.0, NEG), (1, 4))
    upper = jnp.tile(jnp.where(c_idx <= r_idx, 0.0, NEG), (1, 4))
    lane_head = lax.broadcasted_iota(jnp.int32, (1, 4 * blk), 1) // blk

    def sink_row(h):
        row = jnp.full((1, 4 * blk), sink_ref[4 * h + 3], F32)
        for g in range(3):
            row = jnp.where(lane_head == g, sink_ref[4 * h + g], row)
        return row

    sinks = [sink_row(0), sink_row(1)]
    kmax = _key_norm_bounds(k_ref, kc_ref, kb_ref)

    def run(exact):
        l_min = [jnp.full((1, 4 * blk), jnp.inf, F32)]

        def make_item(i, h, k_rows, v_tiles, biases):
            cache = {}

            def score():
                top = jnp.concatenate([q_ref[0, i, (4 * h + g) * HEAD_DIM:(4 * h + g + 1) * HEAD_DIM, :]
                                       for g in range(4)], axis=1)
                if not exact:
                    tf = top.astype(F32)
                    qn = jnp.sqrt(jnp.sum(tf * tf, axis=0, keepdims=True))
                    cache["shift"] = jnp.maximum(qn * kmax[h] * SHIFT_SLACK, sinks[h])
                return _dot(jnp.concatenate([k_rows(), kc_ref[0]], axis=0), _padded_queries(top, h))

            def finish(s):
                pieces = [s[t * blk:(t + 1) * blk] if bias is None else s[t * blk:(t + 1) * blk] + bias
                          for t, bias in enumerate(biases)]
                pieces.append(s[len(biases) * blk:])
                p, l = _column_softmax(pieces, sinks[h], cache.get("shift"))
                if not exact:
                    l_min[0] = jnp.minimum(l_min[0], l)
                vall = jnp.concatenate(v_tiles() + [vc_ref[0]], axis=1)
                o = _dot(vall, p)[h * HEAD_DIM:(h + 1) * HEAD_DIM] / l
                o4 = jnp.concatenate([o[:, g * blk:(g + 1) * blk] for g in range(4)], axis=0)
                q0 = i * blk if isinstance(i, int) else pl.multiple_of(i * blk, blk)
                o_ref[0, pl.ds(q0, blk), 4 * h * HEAD_DIM:4 * (h + 1) * HEAD_DIM] = o4.T.astype(BF16)

            return score, finish

        def edge_items(i, t0, biases):
            return [make_item(i, h, lambda: k_ref[0, t0 * blk:(t0 + 2) * blk, :],
                              lambda: [v_ref[0, t0], v_ref[0, t0 + 1]], biases) for h in range(2)]

        def inner_items(i):
            k0 = pl.multiple_of((i - 1) * blk, blk)
            return [make_item(i, h, lambda: k_ref[0, pl.ds(k0, 3 * blk), :],
                              lambda: [v_ref[0, i - 1], v_ref[0, i], v_ref[0, i + 1]], [lower, None, upper])
                    for h in range(2)]

        _pipelined(edge_items(0, 0, [None, upper]), 1)

        def body(j, carry):
            l_min[0] = carry
            _pipelined([it for u in range(per_iter) for it in inner_items(1 + per_iter * j + u)], 2)
            return l_min[0]

        l_min[0] = lax.fori_loop(0, (nb - 2) // per_iter, body, l_min[0])
        _pipelined(edge_items(nb - 1, nb - 2, [lower, None]), 1)
        return l_min[0]

    l_fast = run(exact=False)

    @pl.when(jnp.min(l_fast) < L_FLOOR)
    def _():
        run(exact=True)


def _window_attn(sink, q_t, k, v_t, kc, vc_t):
    b, nb, hd, blk = q_t.shape
    n = k.shape[1]
    c = kc.shape[1]
    per_iter = 3
    assert blk == WINDOW and (nb - 2) % per_iter == 0
    return pl.pallas_call(
        functools.partial(_win_kernel, blk=blk, per_iter=per_iter),
        grid=(b,),
        in_specs=[
            pl.BlockSpec(memory_space=pltpu.SMEM),
            pl.BlockSpec((1, nb, hd, blk), lambda bi: (bi, 0, 0, 0)),
            pl.BlockSpec((1, n, 2 * HEAD_DIM), lambda bi: (bi, 0, 0)),
            pl.BlockSpec((1, nb, 2 * HEAD_DIM, blk), lambda bi: (bi, 0, 0, 0)),
            pl.BlockSpec((1, c, 2 * HEAD_DIM), lambda bi: (bi, 0, 0)),
            pl.BlockSpec((1, 2 * HEAD_DIM, c), lambda bi: (bi, 0, 0)),
        ],
        out_specs=pl.BlockSpec((1, n, hd), lambda bi: (bi, 0, 0)),
        out_shape=jax.ShapeDtypeStruct((b, n, hd), BF16),
        scratch_shapes=[pltpu.VMEM((n // c + 1, 8, 2 * HEAD_DIM), F32)],
        compiler_params=_vmem(48 * 1024 * 1024),
        name="window_attn",
    )(sink, q_t, k, v_t, kc, vc_t)


def _na_bias_kernel(rpb_ref, o_ref):
    h = pl.program_id(0)
    n_roff = 2 * NA_ROWS - 1
    n_coff = 2 * NA_COLS - 1
    shape = (GRID_W, 4 * GRID_W)
    ck = lax.broadcasted_iota(jnp.int32, shape, 0)
    lane = lax.broadcasted_iota(jnp.int32, shape, 1)
    cq = lane % GRID_W
    head = lane // GRID_W
    cstart = jnp.clip(cq - NA_COLS // 2, 0, GRID_W - NA_COLS)
    valid = (ck >= cstart) & (ck < cstart + NA_COLS)
    diff = ck - cq + (NA_COLS - 1)
    hits = [[valid & (diff == o) & (head == g) for o in range(n_coff)] for g in range(4)]
    tiles = []
    for roff in range(n_roff):
        t = jnp.full(shape, NEG, F32)
        for g in range(4):
            for o in range(n_coff):
                t = jnp.where(hits[g][o], rpb_ref[((4 * h + g) * n_roff + roff) * n_coff + o] * LOG2E, t)
        tiles.append(t)
    for ro in range(n_roff - 1):
        o_ref[0, ro] = jnp.concatenate([tiles[ro], tiles[ro + 1]], axis=0)


def _na_bias(rpb):
    hq, n_roff, n_coff = rpb.shape
    return pl.pallas_call(
        _na_bias_kernel,
        grid=(hq // 4,),
        in_specs=[pl.BlockSpec(memory_space=pltpu.SMEM)],
        out_specs=pl.BlockSpec((1, n_roff - 1, 2 * GRID_W, 4 * GRID_W), lambda h: (h, 0, 0, 0)),
        out_shape=jax.ShapeDtypeStruct((hq // 4, n_roff - 1, 2 * GRID_W, 4 * GRID_W), F32),
        name="na_bias",
    )(rpb.reshape(-1))


def _na_kernel(q_ref, k_ref, v_ref, kc_ref, vc_ref, bias_ref, o_ref, vall_ref, kb_ref):
    n = k_ref.shape[1]
    rows = n // GRID_W
    n_tiles = rows // 2
    kr = min(NA_ROWS, rows)
    nkey = kr * GRID_W
    lo = lax.broadcasted_iota(jnp.int32, (2 * HEAD_DIM, 2 * GRID_W), 1) < GRID_W
    lo_q = lax.broadcasted_iota(jnp.int32, (HEAD_DIM, 2 * GRID_W), 1) < GRID_W

    def swap(x):
        return pltpu.roll(x, GRID_W, 1)

    def build(t, carry):
        a = v_ref[0, t]
        vall_ref[0, t] = a
        nxt = v_ref[0, jnp.minimum(t + 1, n_tiles - 1)]
        vall_ref[1, t] = jnp.where(lo, swap(a.astype(F32)), swap(nxt.astype(F32))).astype(BF16)
        return carry

    lax.fori_loop(0, n_tiles, build, 0)

    def row_start(r):
        return jnp.clip(r - kr // 2, 0, rows - kr)

    kmax = _key_norm_bounds(k_ref, kc_ref, kb_ref)
    bias_max = [jnp.maximum(jnp.max(jnp.max(bias_ref[h], axis=0), axis=0, keepdims=True), 0.0) for h in range(2)]
    tiles_per_iter = 2

    def run(exact):
        l_min = [jnp.full((1, 4 * GRID_W), jnp.inf, F32)]

        def make_item(t, par, h, done):
            r = 2 * t + par
            cache = {}

            def score():
                cols = []
                for c in range(2):
                    g0 = 4 * h + 2 * c
                    a0 = q_ref[0, t, g0 * HEAD_DIM:(g0 + 1) * HEAD_DIM, :].astype(F32)
                    a1 = q_ref[0, t, (g0 + 1) * HEAD_DIM:(g0 + 2) * HEAD_DIM, :].astype(F32)
                    cols.append(jnp.where(lo_q, a0, swap(a1)) if par == 0 else jnp.where(lo_q, swap(a0), a1))
                top = jnp.concatenate(cols, axis=1)
                if not exact:
                    qn = jnp.sqrt(jnp.sum(top * top, axis=0, keepdims=True))
                    cache["shift"] = qn * kmax[h] * SHIFT_SLACK + bias_max[h]
                k0 = pl.multiple_of(row_start(r) * GRID_W, GRID_W)
                return _dot(jnp.concatenate([k_ref[0, pl.ds(k0, nkey), :], kc_ref[0]], axis=0),
                            _padded_queries(top.astype(BF16), h))

            def finish(s):
                rs = row_start(r)
                ro0 = rs - r + (NA_ROWS - 1)
                bias = jnp.concatenate([bias_ref[h, ro0 + 2 * i] for i in range(kr // 2)], axis=0)
                p, l = _column_softmax([s[:nkey] + bias, s[nkey:]], None, cache.get("shift"))
                if not exact:
                    l_min[0] = jnp.minimum(l_min[0], l)
                vwin = [vall_ref[rs & 1, (rs >> 1) + i] for i in range(nkey // (2 * GRID_W))]
                o = _dot(jnp.concatenate(vwin + [vc_ref[0]], axis=1), p)[h * HEAD_DIM:(h + 1) * HEAD_DIM] / l
                done.append(o)
                if par == 1:
                    o_even, o_odd = done
                    ws = []
                    for c in range(2):
                        ce = o_even[:, c * 2 * GRID_W:(c + 1) * 2 * GRID_W]
                        co = o_odd[:, c * 2 * GRID_W:(c + 1) * 2 * GRID_W]
                        ws.append(jnp.where(lo_q, ce, swap(co)))
                        ws.append(jnp.where(lo_q, swap(ce), co))
                    w = jnp.concatenate(ws, axis=0)
                    q0 = pl.multiple_of(t * 2 * GRID_W, 2 * GRID_W)
                    o_ref[0, pl.ds(q0, 2 * GRID_W), 4 * h * HEAD_DIM:4 * (h + 1) * HEAD_DIM] = w.T.astype(BF16)

            return score, finish

        def body(j, carry):
            l_min[0] = carry
            items = []
            for u in range(tiles_per_iter):
                for h in range(2):
                    done = []
                    items += [make_item(tiles_per_iter * j + u, par, h, done) for par in range(2)]
            _pipelined(items, 2)
            return l_min[0]

        return lax.fori_loop(0, n_tiles // tiles_per_iter, body, l_min[0])

    l_fast = run(exact=False)

    @pl.when(jnp.min(l_fast) < L_FLOOR)
    def _():
        run(exact=True)


def _neighbourhood_attn(q_t, k, v_t, kc, vc_t, bias):
    b, n_tiles, hd, tw = q_t.shape
    n = k.shape[1]
    c = kc.shape[1]
    assert tw == 2 * GRID_W and n_tiles * tw == n
    return pl.pallas_call(
        _na_kernel,
        grid=(b,),
        in_specs=[
            pl.BlockSpec((1, n_tiles, hd, tw), lambda bi: (bi, 0, 0, 0)),
            pl.BlockSpec((1, n, 2 * HEAD_DIM), lambda bi: (bi, 0, 0)),
            pl.BlockSpec((1, n_tiles, 2 * HEAD_DIM, tw), lambda bi: (bi, 0, 0, 0)),
            pl.BlockSpec((1, c, 2 * HEAD_DIM), lambda bi: (bi, 0, 0)),
            pl.BlockSpec((1, 2 * HEAD_DIM, c), lambda bi: (bi, 0, 0)),
            _const_spec(bias.shape),
        ],
        out_specs=pl.BlockSpec((1, n, hd), lambda bi: (bi, 0, 0)),
        out_shape=jax.ShapeDtypeStruct((b, n, hd), BF16),
        scratch_shapes=[pltpu.VMEM((2, n_tiles, 2 * HEAD_DIM, tw), BF16),
                        pltpu.VMEM((n // c + 1, 8, 2 * HEAD_DIM), F32)],
        compiler_params=_vmem(52 * 1024 * 1024),
        name="neighbourhood_attn",
    )(q_t, k, v_t, kc, vc_t, bias)


def _ctx_kernel(sink_ref, q_ref, k_ref, v_ref, o_ref):
    c = k_ref.shape[2]
    sink_col = _group_sink_col(sink_ref, pl.program_id(1) * 4, c)
    q = q_ref[0].reshape(4 * c, HEAD_DIM)
    (p,), l = _softmax_parts([_dot_nt(q, k_ref[0, 0])], sink_col)
    o = _dot(p.astype(BF16), v_ref[0, 0]) / l
    o_ref[0] = o.reshape(4, c, HEAD_DIM).astype(BF16)


def _ctx_attn(sink, q, k, v):
    b, hq, c, d = q.shape
    hkv = k.shape[1]
    return pl.pallas_call(
        _ctx_kernel,
        grid=(b, hkv),
        in_specs=[
            pl.BlockSpec(memory_space=pltpu.SMEM),
            pl.BlockSpec((1, 4, c, d), lambda bi, h: (bi, h, 0, 0)),
            pl.BlockSpec((1, 1, c, d), lambda bi, h: (bi, h, 0, 0)),
            pl.BlockSpec((1, 1, c, d), lambda bi, h: (bi, h, 0, 0)),
        ],
        out_specs=pl.BlockSpec((1, 4, c, d), lambda bi, h: (bi, h, 0, 0)),
        out_shape=jax.ShapeDtypeStruct((b, hq, c, d), BF16),
        name="ctx_attn",
    )(sink, q, k, v)


def _diff_kernel(lq1_ref, lk1_ref, lq2_ref, lk2_ref, g_ref, qt_ref, kk_ref, kkc_ref, vt_ref, vtc_ref, o_ref,
                 qz_ref, kb_ref, kmax_ref, m_ref, l_ref, acc1_ref, acc2_ref, *, lam_init, n_sub):
    tq = qt_ref.shape[-1] // n_sub
    n_lt, dv, tk = vt_ref.shape[2:]
    n_kt = n_lt + 1
    assert kkc_ref.shape[2] == tk and vtc_ref.shape[2] == 1

    def k_tile(j):
        return kk_ref[0, 0, j * tk:(j + 1) * tk, :] if j < n_lt else kkc_ref[0, 0]

    def v_tile(j):
        return vt_ref[0, 0, j] if j < n_lt else vtc_ref[0, 0, 0]

    @pl.when(pl.program_id(2) == 0)
    def _():
        def group_max(kt):
            kf = kt.astype(F32)
            return jnp.max((kf * kf).reshape(tk // 8, 8, dv), axis=0)

        def kbody(j, carry):
            kb_ref[j] = group_max(kk_ref[0, 0, pl.ds(pl.multiple_of(j * tk, tk), tk), :])
            return carry

        lax.fori_loop(0, n_lt, kbody, 0)
        kb_ref[n_lt] = group_max(kkc_ref[0, 0])
        best = kb_ref[...].reshape(n_kt * 8, dv)
        half = lax.broadcasted_iota(jnp.int32, best.shape, 1) < HEAD_DIM
        k1 = jnp.max(jnp.sum(jnp.where(half, best, 0.0), axis=1, keepdims=True), axis=0, keepdims=True)
        k2 = jnp.max(jnp.sum(jnp.where(half, 0.0, best), axis=1, keepdims=True), axis=0, keepdims=True)
        kmax_ref[:, :tq] = jnp.broadcast_to(jnp.sqrt(k1), (1, tq))
        kmax_ref[:, tq:] = jnp.broadcast_to(jnp.sqrt(k2), (1, tq))

    zero = jnp.zeros((HEAD_DIM, tq), BF16)
    state = []
    for s in range(n_sub):
        qt = qt_ref[0, 0, :, s * tq:(s + 1) * tq]
        qz_ref[s, 0:HEAD_DIM, 0:tq] = qt[0:HEAD_DIM]
        qz_ref[s, 0:HEAD_DIM, tq:] = zero
        qz_ref[s, HEAD_DIM:, 0:tq] = zero
        qz_ref[s, HEAD_DIM:, tq:] = qt[HEAD_DIM:]
        qf = qt.astype(F32)
        qsq = qf * qf
        qn = jnp.concatenate([jnp.sum(qsq[:HEAD_DIM], axis=0, keepdims=True),
                              jnp.sum(qsq[HEAD_DIM:], axis=0, keepdims=True)], axis=1)
        state.append(dict(qz=qz_ref[s], shift=jnp.sqrt(qn) * kmax_ref[...] * (1.0 + 2.0 ** -10),
                          l8=jnp.zeros((8, 2 * tq), F32), acc1=jnp.zeros((dv, tq), F32), acc2=jnp.zeros((dv, tq), F32)))

    lam = (jnp.exp(jnp.sum(lq1_ref[...] * lk1_ref[...], axis=-1, keepdims=True))
           - jnp.exp(jnp.sum(lq2_ref[...] * lk2_ref[...], axis=-1, keepdims=True)) + lam_init)

    def finalize(s, l, acc1, acc2):
        o = acc1 / l[:, :tq] - lam * (acc2 / l[:, tq:])
        ms = jnp.mean(o * o, axis=0, keepdims=True)
        o = (o * lax.rsqrt(ms + EPS) * g_ref[...]) * (1.0 - lam_init)
        o_ref[0, s * tq:(s + 1) * tq, :] = o.T.astype(BF16)

    def make_item(s, j):
        st = state[s]

        def finish(scores):
            p = jnp.exp2(scores - st["shift"])
            st["l8"] = st["l8"] + jnp.sum(p.reshape(tk // 8, 8, 2 * tq), axis=0)
            pb = p.astype(BF16)
            vt = v_tile(j)
            st["acc1"] = st["acc1"] + _dot(vt, pb[:, :tq])
            st["acc2"] = st["acc2"] + _dot(vt, pb[:, tq:])
            if j == n_kt - 1:
                l_fast = jnp.sum(st["l8"], axis=0, keepdims=True)
                l_ref[s] = l_fast
                finalize(s, l_fast, st["acc1"], st["acc2"])

        return (lambda: _dot(k_tile(j), st["qz"])), finish

    _pipelined([make_item(s, j) for s in range(n_sub) for j in range(n_kt)], 2)
    l_min = functools.reduce(jnp.minimum, [l_ref[s] for s in range(n_sub)])

    @pl.when(jnp.min(l_min) < 2.0 ** -80)
    def _():
        for s in range(n_sub):
            m_ref[...] = jnp.full(m_ref.shape, NEG, F32)
            l_ref[s] = jnp.zeros((1, 2 * tq), F32)
            acc1_ref[s] = jnp.zeros((dv, tq), F32)
            acc2_ref[s] = jnp.zeros((dv, tq), F32)

            def step(kt, vt):
                st = _dot(kt, qz_ref[s])
                m_old = m_ref[...]
                m_new = jnp.maximum(m_old, jnp.max(st, axis=0, keepdims=True))
                alpha = jnp.exp2(m_old - m_new)
                p = jnp.exp2(st - m_new)
                l_ref[s] = alpha * l_ref[s] + jnp.sum(p, axis=0, keepdims=True)
                m_ref[...] = m_new
                pb = p.astype(BF16)
                acc1_ref[s] = alpha[:, :tq] * acc1_ref[s] + _dot(vt, pb[:, :tq])
                acc2_ref[s] = alpha[:, tq:] * acc2_ref[s] + _dot(vt, pb[:, tq:])

            def body(j, carry):
                step(kk_ref[0, 0, pl.ds(pl.multiple_of(j * tk, tk), tk), :], vt_ref[0, 0, j])
                return carry

            lax.fori_loop(0, n_lt, body, 0)
            step(kkc_ref[0, 0], vtc_ref[0, 0, 0])
            finalize(s, l_ref[s], acc1_ref[s], acc2_ref[s])


def _diff_attn(lam_vecs, sub_g, qt, kk, kkc, vt, vtc, lam_init, tq_block, n_sub):
    b, h, dv, n = qt.shape
    n_ctx = kkc.shape[2]
    n_lt, _, tk = vt.shape[2:]
    tq = tq_block // n_sub
    vec_spec = pl.BlockSpec((1, HEAD_DIM), lambda bi, hi, i: (0, 0))
    return pl.pallas_call(
        functools.partial(_diff_kernel, lam_init=lam_init, n_sub=n_sub),
        grid=(b, h, n // tq_block),
        in_specs=[vec_spec] * 4 + [
            pl.BlockSpec((dv, 1), lambda bi, hi, i: (0, 0)),
            pl.BlockSpec((1, 1, dv, tq_block), lambda bi, hi, i: (bi, hi, 0, i)),
            pl.BlockSpec((1, 1, n, dv), lambda bi, hi, i: (bi, hi, 0, 0)),
            pl.BlockSpec((1, 1, n_ctx, dv), lambda bi, hi, i: (bi, hi, 0, 0)),
            pl.BlockSpec((1, 1, n_lt, dv, tk), lambda bi, hi, i: (bi, hi, 0, 0, 0)),
            pl.BlockSpec((1, 1, 1, dv, tk), lambda bi, hi, i: (bi, hi, 0, 0, 0)),
        ],
        out_specs=pl.BlockSpec((1, tq_block, dv), lambda bi, hi, i: (bi, i, hi)),
        out_shape=jax.ShapeDtypeStruct((b, n, h * dv), BF16),
        scratch_shapes=[
            pltpu.VMEM((n_sub, dv, 2 * tq), BF16),
            pltpu.VMEM((n_lt + 1, 8, dv), F32),
            pltpu.VMEM((1, 2 * tq), F32),
            pltpu.VMEM((1, 2 * tq), F32),
            pltpu.VMEM((n_sub, 1, 2 * tq), F32),
            pltpu.VMEM((n_sub, dv, tq), F32),
            pltpu.VMEM((n_sub, dv, tq), F32),
        ],
        compiler_params=pltpu.CompilerParams(
            dimension_semantics=("arbitrary", "arbitrary", "arbitrary"), vmem_limit_bytes=32 * 1024 * 1024),
        name="diff_attn",
    )(*lam_vecs, sub_g.reshape(dv, 1), qt, kk, kkc, vt, vtc)


def _mlp_kernel(*refs, n_y, final, ff_chunk):
    y_refs, refs = refs[:n_y], refs[n_y:]
    if final:
        h_ref, mod_ref, g_ref, wo_ref, w1_ref, w2_ref, fg_ref, o_ref = refs
    else:
        h_ref, mod_ref, g_ref, wo_ref, w1_ref, w2_ref, o_ref = refs
    yw = wo_ref.shape[0] // n_y
    y = functools.reduce(jnp.add, [_dot(y_ref[0], wo_ref[j * yw:(j + 1) * yw, :]) for j, y_ref in enumerate(y_refs)])
    hx = h_ref[0] + mod_ref[0, 2:3, :] * y
    a = _modulated(hx, g_ref[...], mod_ref[0, 3:4, :], mod_ref[0, 4:5, :]).astype(BF16)
    acc = jnp.zeros(hx.shape, F32)
    for c0 in range(0, w1_ref.shape[1], ff_chunk):
        u = jnp.maximum(_dot(a, w1_ref[:, c0:c0 + ff_chunk]), 0.0)
        acc = acc + _dot((u * u).astype(BF16), w2_ref[c0:c0 + ff_chunk, :])
    out = hx + mod_ref[0, 5:6, :] * acc
    if final:
        ms = jnp.mean(out * out, axis=-1, keepdims=True)
        out = out * lax.rsqrt(ms + EPS) * fg_ref[...]
    o_ref[0] = out


def _out_mlp(ys, h, mod9, mod_row, g2, wo, w1, w2, final_g, tm, name):
    b, n, d = h.shape
    dff = w1.shape[1]
    final = final_g is not None
    if mod_row is None:
        mod_map = lambda bi, i: (bi, 0, 0)
    else:
        mod_map = lambda bi, i: (mod_row, 0, 0)
    tile = pl.BlockSpec((1, tm, d), lambda bi, i: (bi, i, 0))
    in_specs = [pl.BlockSpec((1, tm, y.shape[-1]), lambda bi, i: (bi, i, 0)) for y in ys]
    in_specs += [tile, pl.BlockSpec((1, 6, d), mod_map), _const_spec((1, d)),
                 _const_spec((d, d)), _const_spec((d, dff)), _const_spec((dff, d))]
    args = list(ys) + [h, mod9, g2.reshape(1, d), wo, w1, w2]
    if final:
        in_specs.append(_const_spec((1, d)))
        args.append(final_g.reshape(1, d))
    return pl.pallas_call(
        functools.partial(_mlp_kernel, n_y=len(ys), final=final, ff_chunk=1024),
        grid=(b, n // tm),
        in_specs=in_specs,
        out_specs=tile,
        out_shape=jax.ShapeDtypeStruct((b, n, d), F32),
        compiler_params=_vmem(56 * 1024 * 1024),
        name=name,
    )(*args)


def _rope_tables(n):
    t = jnp.arange(n, dtype=jnp.int32)
    row = (t // GRID_W).astype(F32)
    col = (t % GRID_W).astype(F32)
    quarter = HEAD_DIM // 4
    inv_freq = ROPE_THETA ** (-jnp.arange(quarter, dtype=F32) / quarter)
    ar = row[:, None] * inv_freq[None, :]
    ac = col[:, None] * inv_freq[None, :]
    ang = jnp.concatenate([ar, ar, ac, ac], axis=-1)
    cos, sin = jnp.cos(ang), jnp.sin(ang)
    even = (jnp.arange(HEAD_DIM) // quarter) % 2 == 0
    rep = LANES // HEAD_DIM
    sin_a = jnp.tile(jnp.where(even, -sin, 0.0), (1, rep))
    sin_b = jnp.tile(jnp.where(even, 0.0, sin), (1, rep))
    token_major = (jnp.tile(cos, (1, rep)), sin_a, sin_b)
    feature_major = (cos.T, jnp.where(even, -sin, sin).T)
    return token_major, feature_major


def _heads(p, lo, hi, width=HEAD_DIM):
    b, n, _ = p.shape
    return p[:, :, lo:hi].reshape(b, n, (hi - lo) // width, width).transpose(0, 2, 1, 3)


def _unheads(y):
    b, h, n, d = y.shape
    return y.transpose(0, 2, 1, 3).reshape(b, n, h * d)


def kernel(x, c, ctx, c_ctx, ada_w, ada_b, norm1_g, norm2_g, even_w_in, even_w_out, a_sink, b_rpb,
           odd_w_in, odd_w_out, lam_q1, lam_k1, lam_q2, lam_k2, subln_g, mlp_w1, mlp_w2, final_g):
    b, n, d = x.shape
    n_ctx = ctx.shape[1]
    assert ada_w.shape[0] == 2 and d == 1024 and n % 512 == 0
    scale = HEAD_DIM ** -0.5
    a_qw, a_kvw = 8 * HEAD_DIM, 2 * HEAD_DIM
    b_q0 = a_qw + 2 * a_kvw

    cc = jnp.zeros((16, d), F32).at[:b].set(c).at[b].set(c_ctx)
    mod = _ada_call(cc, ada_w, ada_b)[:, :b + 1].reshape(2, b + 1, 6, d)
    rope, rope_t = _rope_tables(n)

    qscale = scale * LOG2E
    cols0 = jnp.arange(even_w_in.shape[2])
    qcols0 = (cols0 < a_qw) | ((cols0 >= b_q0) & (cols0 < b_q0 + a_qw))
    w_in0 = (even_w_in[0] * jnp.where(qcols0, qscale, 1.0)).astype(BF16)
    cols1 = jnp.arange(odd_w_in.shape[2])
    w_in1 = (odd_w_in[0] * jnp.where(cols1 < 1024, qscale, 1.0)).astype(BF16)
    w_out0, w_out1 = even_w_out[0].astype(BF16), odd_w_out[0].astype(BF16)
    w1, w2 = mlp_w1.astype(BF16), mlp_w2.astype(BF16)

    ak0, av0, bq0, bk0, bv0 = a_qw, a_qw + a_kvw, b_q0, b_q0 + a_qw, b_q0 + a_qw + a_kvw
    wk0 = jnp.concatenate([w_in0[:, ak0:av0], w_in0[:, bk0:bv0]], axis=1)
    wt0 = jnp.concatenate([w_in0[:, :ak0], w_in0[:, av0:bq0], w_in0[:, bq0:bk0], w_in0[:, bv0:]], axis=1).T
    aq_t, ak, av_t, bq_t, bk, bv_t = _project0(x, mod[0], norm1_g[0], wk0, wt0, rope, rope_t, 512, WINDOW)
    pc = _project(ctx, mod[0], b, norm1_g[0], w_in0, None, n_ctx, "proj0_ctx")
    cak, cav_t = pc[:, :, ak0:av0], pc[:, :, av0:bq0].transpose(0, 2, 1)
    cbk, cbv_t = pc[:, :, bk0:bv0], pc[:, :, bv0:].transpose(0, 2, 1)
    sink = a_sink[0].astype(F32) * LOG2E
    ya = _window_attn(sink, aq_t, ak, av_t, cak, cav_t)
    yb = _neighbourhood_attn(bq_t, bk, bv_t, cbk, cbv_t, _na_bias(b_rpb[0].astype(F32)))
    caq, cbq = _heads(pc, 0, ak0), _heads(pc, bq0, bk0)
    sink_ab = jnp.concatenate([sink, jnp.full_like(sink, NEG)])
    yc = _unheads(_ctx_attn(sink_ab, jnp.concatenate([caq, cbq], axis=1),
                            jnp.concatenate([_heads(pc, ak0, av0), _heads(pc, bk0, bv0)], axis=1),
                            jnp.concatenate([_heads(pc, av0, bq0), _heads(pc, bv0, bv0 + a_kvw)], axis=1)))
    hx = _out_mlp([ya, yb], x, mod[0], None, norm2_g[0], w_out0, w1[0], w2[0], None, 512, "mlp0_x")
    hc = _out_mlp([yc], ctx, mod[0], b, norm2_g[0], w_out0, w1[0], w2[0], None, n_ctx, "mlp0_ctx")

    lam_init = 0.8 - 0.6 * math.exp(-0.3 * 1)
    tk = n_ctx
    wq1, wk1, wv1 = w_in1[:, :1024], w_in1[:, 1024:2048], w_in1[:, 2048:]
    qt, kk, vt = _project1(hx, mod[1], None, norm1_g[1], wk1, jnp.concatenate([wq1, wv1], axis=1).T,
                           rope, rope_t, 512, tk, "proj1_x")
    kkc, vtc = _project1(hc, mod[1], b, norm1_g[1], wk1, wv1.T, None, None, n_ctx, tk, "proj1_ctx")
    lam_vecs = [v[0].reshape(1, HEAD_DIM).astype(F32) for v in (lam_q1, lam_k1, lam_q2, lam_k2)]
    yx = _diff_attn(lam_vecs, subln_g[0].astype(F32), qt, kk, kkc, vt, vtc, lam_init, 1024, 4)
    return _out_mlp([yx], hx, mod[1], None, norm2_g[1], w_out1, w1[1], w2[1], final_g, 512, "mlp1_x")
```

```python
import functools
import math

import jax
import jax.numpy as jnp
from jax import lax
from jax.experimental import pallas as pl
from jax.experimental.pallas import tpu as pltpu

F32 = jnp.float32
BF16 = jnp.bfloat16

GRID_W = 64
HEAD_DIM = 64
WINDOW = 128
NA_ROWS = 8
NA_COLS = 16
ROPE_THETA = 10000.0
EPS = 1e-6
NEG = -1e30
LOG2E = math.log2(math.e)

V7X_VMEM_BYTES = 64 * 1024 * 1024
LANES = 128


def _vmem(nbytes):
    return pltpu.CompilerParams(vmem_limit_bytes=int(min(nbytes, V7X_VMEM_BYTES - 8 * 1024 * 1024)))


def _const_spec(shape):
    n = len(shape)
    return pl.BlockSpec(shape, lambda *_: (0,) * n, pipeline_mode=pl.Buffered(1))


def _dot(a, b):
    return jnp.dot(a, b, preferred_element_type=F32)


def _dot_nt(a, b):
    return lax.dot_general(a, b, (((1,), (1,)), ((), ())), preferred_element_type=F32)


def _ada_kernel(c_ref, w_ref, b_ref, o_ref):
    cc = c_ref[...]
    s = cc * jax.nn.sigmoid(cc)
    o_ref[0] = jnp.dot(s, w_ref[0], preferred_element_type=F32, precision=lax.Precision.HIGHEST) + b_ref[0]


def _ada_call(cc, ada_w, ada_b):
    depth, d, six_d = ada_w.shape
    tn = 1536
    return pl.pallas_call(
        _ada_kernel,
        grid=(depth, six_d // tn),
        in_specs=[
            pl.BlockSpec((16, d), lambda l, j: (0, 0)),
            pl.BlockSpec((1, d, tn), lambda l, j: (l, 0, j)),
            pl.BlockSpec((1, 1, tn), lambda l, j: (l, 0, j)),
        ],
        out_specs=pl.BlockSpec((1, 16, tn), lambda l, j: (l, 0, j)),
        out_shape=jax.ShapeDtypeStruct((depth, 16, six_d), F32),
        compiler_params=_vmem(32 * 1024 * 1024),
        name="ada_mod",
    )(cc, ada_w, ada_b.reshape(depth, 1, six_d))


def _modulated(x, g, shift, scale):
    ms = jnp.mean(x * x, axis=-1, keepdims=True)
    return (x * lax.rsqrt(ms + EPS) * g) * (1.0 + scale) + shift


def _proj_kernel(*refs, rope_chunks, n_chunk):
    if rope_chunks:
        h_ref, mod_ref, g_ref, w_ref, cos_ref, sa_ref, sb_ref, o_ref = refs
    else:
        h_ref, mod_ref, g_ref, w_ref, o_ref = refs
    a = _modulated(h_ref[0], g_ref[...], mod_ref[0, 0:1, :], mod_ref[0, 1:2, :]).astype(BF16)
    nout = o_ref.shape[-1]
    for c0 in range(0, nout, n_chunk):
        p = _dot(a, w_ref[:, c0:c0 + n_chunk])
        for j in range(n_chunk // LANES):
            lo = c0 + j * LANES
            xc = p[:, j * LANES:(j + 1) * LANES]
            if lo < rope_chunks * LANES:
                xc = (xc * cos_ref[...] + pltpu.roll(xc, LANES - 16, 1) * sa_ref[...]
                      + pltpu.roll(xc, 16, 1) * sb_ref[...])
            o_ref[0, :, lo:lo + LANES] = xc.astype(BF16)


def _project(h, mod9, mod_row, g, w, rope, tm, name):
    b, n, d = h.shape
    nout = w.shape[1]
    rope_chunks = 0 if rope is None else rope[0]
    n_chunk = 512
    if mod_row is None:
        mod_map = lambda bi, i: (bi, 0, 0)
    else:
        mod_map = lambda bi, i: (mod_row, 0, 0)
    in_specs = [
        pl.BlockSpec((1, tm, d), lambda bi, i: (bi, i, 0)),
        pl.BlockSpec((1, 6, d), mod_map),
        _const_spec((1, d)),
        _const_spec((d, nout)),
    ]
    args = [h, mod9, g.reshape(1, d), w]
    if rope_chunks:
        in_specs += [pl.BlockSpec((tm, LANES), lambda bi, i: (i, 0))] * 3
        args += list(rope[1:])
    return pl.pallas_call(
        functools.partial(_proj_kernel, rope_chunks=rope_chunks, n_chunk=n_chunk),
        grid=(b, n // tm),
        in_specs=in_specs,
        out_specs=pl.BlockSpec((1, tm, nout), lambda bi, i: (bi, i, 0)),
        out_shape=jax.ShapeDtypeStruct((b, n, nout), BF16),
        compiler_params=_vmem(48 * 1024 * 1024),
        name=name,
    )(*args)


def _rope_rows(x, cos_t, sin_t):
    q = HEAD_DIM // 4
    rot = jnp.concatenate([x[q:2 * q], x[0:q], x[3 * q:4 * q], x[2 * q:3 * q]], axis=0)
    return x * cos_t + rot * sin_t


def _proj1_kernel(*refs, with_q, tk):
    if with_q:
        (h_ref, mod_ref, g_ref, wk_ref, wt_ref, cos_ref, sa_ref, sb_ref, cost_ref, sint_ref,
         qt_ref, kk_ref, vt_ref) = refs
    else:
        h_ref, mod_ref, g_ref, wk_ref, wt_ref, kk_ref, vt_ref = refs
    a = _modulated(h_ref[0], g_ref[...], mod_ref[0, 0:1, :], mod_ref[0, 1:2, :]).astype(BF16)
    tm = a.shape[0]
    dv = 2 * HEAD_DIM
    n_heads = kk_ref.shape[1]
    chunk = 4 * dv
    for c0 in range(0, n_heads * dv, chunk):
        p = _dot(a, wk_ref[:, c0:c0 + chunk])
        for j in range(chunk // dv):
            xc = p[:, j * dv:(j + 1) * dv]
            if with_q:
                xc = (xc * cos_ref[...] + pltpu.roll(xc, LANES - 16, 1) * sa_ref[...]
                      + pltpu.roll(xc, 16, 1) * sb_ref[...])
            kk_ref[0, c0 // dv + j] = xc.astype(BF16)
    n_q = n_heads * dv if with_q else 0
    for f0 in range(0, wt_ref.shape[0], chunk):
        xt = _dot_nt(wt_ref[f0:f0 + chunk, :], a)
        for j in range(chunk // dv):
            row = f0 + j * dv
            hd = xt[j * dv:(j + 1) * dv]
            if row < n_q:
                hd = jnp.concatenate([_rope_rows(hd[m * HEAD_DIM:(m + 1) * HEAD_DIM], cost_ref[...], sint_ref[...])
                                      for m in range(2)], axis=0)
                qt_ref[0, row // dv] = hd.astype(BF16)
            else:
                for t in range(tm // tk):
                    vt_ref[0, (row - n_q) // dv, t] = hd[:, t * tk:(t + 1) * tk].astype(BF16)


def _project1(h, mod9, mod_row, g, wk, wt, rope, rope_t, tm, tk, name):
    b, n, d = h.shape
    dv = 2 * HEAD_DIM
    n_heads = wk.shape[1] // dv
    with_q = rope is not None
    if mod_row is None:
        mod_map = lambda bi, i: (bi, 0, 0)
    else:
        mod_map = lambda bi, i: (mod_row, 0, 0)
    in_specs = [
        pl.BlockSpec((1, tm, d), lambda bi, i: (bi, i, 0)),
        pl.BlockSpec((1, 6, d), mod_map),
        _const_spec((1, d)),
        _const_spec(wk.shape),
        _const_spec(wt.shape),
    ]
    args = [h, mod9, g.reshape(1, d), wk, wt]
    out_specs = [
        pl.BlockSpec((1, n_heads, tm, dv), lambda bi, i: (bi, 0, i, 0)),
        pl.BlockSpec((1, n_heads, tm // tk, dv, tk), lambda bi, i: (bi, 0, i, 0, 0)),
    ]
    out_shape = [
        jax.ShapeDtypeStruct((b, n_heads, n, dv), BF16),
        jax.ShapeDtypeStruct((b, n_heads, n // tk, dv, tk), BF16),
    ]
    if with_q:
        in_specs += [pl.BlockSpec((tm, LANES), lambda bi, i: (i, 0))] * 3
        in_specs += [pl.BlockSpec((HEAD_DIM, tm), lambda bi, i: (0, i))] * 2
        args += list(rope) + list(rope_t)
        out_specs.insert(0, pl.BlockSpec((1, n_heads, dv, tm), lambda bi, i: (bi, 0, 0, i)))
        out_shape.insert(0, jax.ShapeDtypeStruct((b, n_heads, dv, n), BF16))
    return pl.pallas_call(
        functools.partial(_proj1_kernel, with_q=with_q, tk=tk),
        grid=(b, n // tm),
        in_specs=in_specs,
        out_specs=out_specs,
        out_shape=out_shape,
        compiler_params=_vmem(48 * 1024 * 1024),
        name=name,
    )(*args)


def _proj0_kernel(h_ref, mod_ref, g_ref, wk_ref, wt_ref, cos_ref, sa_ref, sb_ref, cost_ref, sint_ref,
                  qa_ref, ka_ref, va_ref, qb_ref, kb_ref, vb_ref, *, tw):
    a = _modulated(h_ref[0], g_ref[...], mod_ref[0, 0:1, :], mod_ref[0, 1:2, :]).astype(BF16)
    tm = a.shape[0]
    kw = ka_ref.shape[-1]
    p = _dot(a, wk_ref[...])
    xa = p[:, :kw]
    xa = xa * cos_ref[...] + pltpu.roll(xa, LANES - 16, 1) * sa_ref[...] + pltpu.roll(xa, 16, 1) * sb_ref[...]
    ka_ref[0] = xa.astype(BF16)
    kb_ref[0] = p[:, kw:].astype(BF16)
    qw = qa_ref.shape[2]
    rows = qw + va_ref.shape[2]
    for mixer, (q_ref, v_ref) in enumerate(((qa_ref, va_ref), (qb_ref, vb_ref))):
        xt = _dot_nt(wt_ref[mixer * rows:(mixer + 1) * rows, :], a)
        q = xt[:qw]
        if mixer == 0:
            q = jnp.concatenate([_rope_rows(q[h * HEAD_DIM:(h + 1) * HEAD_DIM], cost_ref[...], sint_ref[...])
                                 for h in range(qw // HEAD_DIM)], axis=0)
        for t in range(tm // tw):
            q_ref[0, t] = q[:, t * tw:(t + 1) * tw].astype(BF16)
            v_ref[0, t] = xt[qw:, t * tw:(t + 1) * tw].astype(BF16)


def _project0(h, mod9, g, wk, wt, rope, rope_t, tm, tw):
    b, n, d = h.shape
    kw = wk.shape[1] // 2
    rows = wt.shape[0] // 2
    qw = rows - kw
    q_spec = pl.BlockSpec((1, tm // tw, qw, tw), lambda bi, i: (bi, i, 0, 0))
    k_spec = pl.BlockSpec((1, tm, kw), lambda bi, i: (bi, i, 0))
    v_spec = pl.BlockSpec((1, tm // tw, kw, tw), lambda bi, i: (bi, i, 0, 0))
    q_shape = jax.ShapeDtypeStruct((b, n // tw, qw, tw), BF16)
    k_shape = jax.ShapeDtypeStruct((b, n, kw), BF16)
    v_shape = jax.ShapeDtypeStruct((b, n // tw, kw, tw), BF16)
    return pl.pallas_call(
        functools.partial(_proj0_kernel, tw=tw),
        grid=(b, n // tm),
        in_specs=[
            pl.BlockSpec((1, tm, d), lambda bi, i: (bi, i, 0)),
            pl.BlockSpec((1, 6, d), lambda bi, i: (bi, 0, 0)),
            _const_spec((1, d)),
            _const_spec(wk.shape),
            _const_spec(wt.shape),
        ] + [pl.BlockSpec((tm, LANES), lambda bi, i: (i, 0))] * 3
          + [pl.BlockSpec((HEAD_DIM, tm), lambda bi, i: (0, i))] * 2,
        out_specs=[q_spec, k_spec, v_spec, q_spec, k_spec, v_spec],
        out_shape=[q_shape, k_shape, v_shape, q_shape, k_shape, v_shape],
        compiler_params=_vmem(48 * 1024 * 1024),
        name="proj0_x",
    )(h, mod9, g.reshape(1, d), wk, wt, *rope, *rope_t)


def _softmax_parts(scores, extra_col=None):
    m = functools.reduce(jnp.maximum, [jnp.max(s, axis=-1, keepdims=True) for s in scores])
    if extra_col is not None:
        m = jnp.maximum(m, extra_col)
    ps = [jnp.exp2(s - m) for s in scores]
    l = functools.reduce(jnp.add, [jnp.sum(p, axis=-1, keepdims=True) for p in ps])
    if extra_col is not None:
        l = l + jnp.exp2(extra_col - m)
    return ps, l


def _group_sink_col(sink_ref, base, rows_per_head):
    rows = lax.broadcasted_iota(jnp.int32, (4 * rows_per_head, 1), 0) // rows_per_head
    col = jnp.full((4 * rows_per_head, 1), sink_ref[base + 3], F32)
    for g in range(3):
        col = jnp.where(rows == g, sink_ref[base + g], col)
    return col


def _pipelined(items, ahead):
    scores = [items[k][0]() for k in range(min(ahead, len(items)))]
    for k in range(len(items)):
        if k + ahead < len(items):
            scores.append(items[k + ahead][0]())
        items[k][1](scores[k])
        scores[k] = None


SHIFT_SLACK = 1.0 + 2.0 ** -10
L_FLOOR = 2.0 ** -80


def _key_norm_bounds(k_ref, kc_ref, kb_ref):
    tile = kc_ref.shape[1]
    nt = k_ref.shape[1] // tile
    width = k_ref.shape[2]

    def group_max(kt):
        kf = kt.astype(F32)
        return jnp.max((kf * kf).reshape(tile // 8, 8, width), axis=0)

    def body(j, carry):
        kb_ref[j] = group_max(k_ref[0, pl.ds(pl.multiple_of(j * tile, tile), tile), :])
        return carry

    lax.fori_loop(0, nt, body, 0)
    kb_ref[nt] = group_max(kc_ref[0])
    best = kb_ref[...].reshape((nt + 1) * 8, width)
    half = lax.broadcasted_iota(jnp.int32, best.shape, 1) < width // 2
    k0 = jnp.max(jnp.sum(jnp.where(half, best, 0.0), axis=1, keepdims=True), axis=0, keepdims=True)
    k1 = jnp.max(jnp.sum(jnp.where(half, 0.0, best), axis=1, keepdims=True), axis=0, keepdims=True)
    return jnp.sqrt(k0), jnp.sqrt(k1)


def _column_softmax(pieces, extra_row=None, shift=None):
    if shift is None:
        m = functools.reduce(jnp.maximum, [jnp.max(s, axis=0, keepdims=True) for s in pieces])
        if extra_row is not None:
            m = jnp.maximum(m, extra_row)
    else:
        m = shift
    ps = [jnp.exp2(s - m) for s in pieces]
    l = functools.reduce(jnp.add, [jnp.sum(p, axis=0, keepdims=True) for p in ps])
    if extra_row is not None:
        l = l + jnp.exp2(extra_row - m)
    return jnp.concatenate([p.astype(BF16) for p in ps], axis=0), l


def _padded_queries(top, h):
    z = jnp.zeros_like(top)
    return jnp.concatenate([top, z] if h == 0 else [z, top], axis=0)


def _win_kernel(sink_ref, q_ref, k_ref, v_ref, kc_ref, vc_ref, o_ref, kb_ref, *, blk, per_iter):
    n = k_ref.shape[1]
    nb = n // blk
    c_idx = lax.broadcasted_iota(jnp.int32, (blk, blk), 0)
    r_idx = lax.broadcasted_iota(jnp.int32, (blk, blk), 1)
    lower = jnp.tile(jnp.where(c_idx >= r_idx, 0.0, NEG), (1, 4))
    upper = jnp.tile(jnp.where(c_idx <= r_idx, 0.0, NEG), (1, 4))
    lane_head = lax.broadcasted_iota(jnp.int32, (1, 4 * blk), 1) // blk

    def sink_row(h):
        row = jnp.full((1, 4 * blk), sink_ref[4 * h + 3], F32)
        for g in range(3):
            row = jnp.where(lane_head == g, sink_ref[4 * h + g], row)
        return row

    sinks = [sink_row(0), sink_row(1)]
    kmax = _key_norm_bounds(k_ref, kc_ref, kb_ref)

    def run(exact):
        l_min = [jnp.full((1, 4 * blk), jnp.inf, F32)]

        def make_item(i, h, k_rows, v_tiles, biases):
            cache = {}

            def score():
                top = jnp.concatenate([q_ref[0, i, (4 * h + g) * HEAD_DIM:(4 * h + g + 1) * HEAD_DIM, :]
                                       for g in range(4)], axis=1)
                if not exact:
                    tf = top.astype(F32)
                    qn = jnp.sqrt(jnp.sum(tf * tf, axis=0, keepdims=True))
                    cache["shift"] = jnp.maximum(qn * kmax[h] * SHIFT_SLACK, sinks[h])
                return _dot(jnp.concatenate([k_rows(), kc_ref[0]], axis=0), _padded_queries(top, h))

            def finish(s):
                pieces = [s[t * blk:(t + 1) * blk] if bias is None else s[t * blk:(t + 1) * blk] + bias
                          for t, bias in enumerate(biases)]
                pieces.append(s[len(biases) * blk:])
                p, l = _column_softmax(pieces, sinks[h], cache.get("shift"))
                if not exact:
                    l_min[0] = jnp.minimum(l_min[0], l)
                vall = jnp.concatenate(v_tiles() + [vc_ref[0]], axis=1)
                o = _dot(vall, p)[h * HEAD_DIM:(h + 1) * HEAD_DIM] / l
                o4 = jnp.concatenate([o[:, g * blk:(g + 1) * blk] for g in range(4)], axis=0)
                q0 = i * blk if isinstance(i, int) else pl.multiple_of(i * blk, blk)
                o_ref[0, pl.ds(q0, blk), 4 * h * HEAD_DIM:4 * (h + 1) * HEAD_DIM] = o4.T.astype(BF16)

            return score, finish

        def edge_items(i, t0, biases):
            return [make_item(i, h, lambda: k_ref[0, t0 * blk:(t0 + 2) * blk, :],
                              lambda: [v_ref[0, t0], v_ref[0, t0 + 1]], biases) for h in range(2)]

        def inner_items(i):
            k0 = pl.multiple_of((i - 1) * blk, blk)
            return [make_item(i, h, lambda: k_ref[0, pl.ds(k0, 3 * blk), :],
                              lambda: [v_ref[0, i - 1], v_ref[0, i], v_ref[0, i + 1]], [lower, None, upper])
                    for h in range(2)]

        _pipelined(edge_items(0, 0, [None, upper]), 1)

        def body(j, carry):
            l_min[0] = carry
            _pipelined([it for u in range(per_iter) for it in inner_items(1 + per_iter * j + u)], 2)
            return l_min[0]

        l_min[0] = lax.fori_loop(0, (nb - 2) // per_iter, body, l_min[0])
        _pipelined(edge_items(nb - 1, nb - 2, [lower, None]), 1)
        return l_min[0]

    l_fast = run(exact=False)

    @pl.when(jnp.min(l_fast) < L_FLOOR)
    def _():
        run(exact=True)


def _window_attn(sink, q_t, k, v_t, kc, vc_t):
    b, nb, hd, blk = q_t.shape
    n = k.shape[1]
    c = kc.shape[1]
    per_iter = 5
    assert blk == WINDOW and (nb - 2) % per_iter == 0
    return pl.pallas_call(
        functools.partial(_win_kernel, blk=blk, per_iter=per_iter),
        grid=(b,),
        in_specs=[
            pl.BlockSpec(memory_space=pltpu.SMEM),
            pl.BlockSpec((1, nb, hd, blk), lambda bi: (bi, 0, 0, 0)),
            pl.BlockSpec((1, n, 2 * HEAD_DIM), lambda bi: (bi, 0, 0)),
            pl.BlockSpec((1, nb, 2 * HEAD_DIM, blk), lambda bi: (bi, 0, 0, 0)),
            pl.BlockSpec((1, c, 2 * HEAD_DIM), lambda bi: (bi, 0, 0)),
            pl.BlockSpec((1, 2 * HEAD_DIM, c), lambda bi: (bi, 0, 0)),
        ],
        out_specs=pl.BlockSpec((1, n, hd), lambda bi: (bi, 0, 0)),
        out_shape=jax.ShapeDtypeStruct((b, n, hd), BF16),
        scratch_shapes=[pltpu.VMEM((n // c + 1, 8, 2 * HEAD_DIM), F32)],
        compiler_params=_vmem(48 * 1024 * 1024),
        name="window_attn",
    )(sink, q_t, k, v_t, kc, vc_t)


def _na_bias_kernel(rpb_ref, o_ref):
    h = pl.program_id(0)
    n_roff = 2 * NA_ROWS - 1
    n_coff = 2 * NA_COLS - 1
    shape = (GRID_W, 4 * GRID_W)
    ck = lax.broadcasted_iota(jnp.int32, shape, 0)
    lane = lax.broadcasted_iota(jnp.int32, shape, 1)
    cq = lane % GRID_W
    head = lane // GRID_W
    cstart = jnp.clip(cq - NA_COLS // 2, 0, GRID_W - NA_COLS)
    valid = (ck >= cstart) & (ck < cstart + NA_COLS)
    diff = ck - cq + (NA_COLS - 1)
    hits = [[valid & (diff == o) & (head == g) for o in range(n_coff)] for g in range(4)]
    tiles = []
    for roff in range(n_roff):
        t = jnp.full(shape, NEG, F32)
        for g in range(4):
            for o in range(n_coff):
                t = jnp.where(hits[g][o], rpb_ref[((4 * h + g) * n_roff + roff) * n_coff + o] * LOG2E, t)
        tiles.append(t)
    for ro in range(n_roff - 1):
        o_ref[0, ro] = jnp.concatenate([tiles[ro], tiles[ro + 1]], axis=0)


def _na_bias(rpb):
    hq, n_roff, n_coff = rpb.shape
    return pl.pallas_call(
        _na_bias_kernel,
        grid=(hq // 4,),
        in_specs=[pl.BlockSpec(memory_space=pltpu.SMEM)],
        out_specs=pl.BlockSpec((1, n_roff - 1, 2 * GRID_W, 4 * GRID_W), lambda h: (h, 0, 0, 0)),
        out_shape=jax.ShapeDtypeStruct((hq // 4, n_roff - 1, 2 * GRID_W, 4 * GRID_W), F32),
        name="na_bias",
    )(rpb.reshape(-1))


def _na_kernel(q_ref, k_ref, v_ref, kc_ref, vc_ref, bias_ref, o_ref, vall_ref, kb_ref):
    n = k_ref.shape[1]
    rows = n // GRID_W
    n_tiles = rows // 2
    kr = min(NA_ROWS, rows)
    nkey = kr * GRID_W
    lo = lax.broadcasted_iota(jnp.int32, (2 * HEAD_DIM, 2 * GRID_W), 1) < GRID_W
    lo_q = lax.broadcasted_iota(jnp.int32, (HEAD_DIM, 2 * GRID_W), 1) < GRID_W

    def swap(x):
        return pltpu.roll(x, GRID_W, 1)

    def build(t, carry):
        a = v_ref[0, t]
        vall_ref[0, t] = a
        nxt = v_ref[0, jnp.minimum(t + 1, n_tiles - 1)]
        vall_ref[1, t] = jnp.where(lo, swap(a.astype(F32)), swap(nxt.astype(F32))).astype(BF16)
        return carry

    lax.fori_loop(0, n_tiles, build, 0)

    def row_start(r):
        return jnp.clip(r - kr // 2, 0, rows - kr)

    kmax = _key_norm_bounds(k_ref, kc_ref, kb_ref)
    bias_max = [jnp.maximum(jnp.max(jnp.max(bias_ref[h], axis=0), axis=0, keepdims=True), 0.0) for h in range(2)]
    tiles_per_iter = 4

    def run(exact):
        l_min = [jnp.full((1, 4 * GRID_W), jnp.inf, F32)]

        def make_item(t, par, h, done):
            r = 2 * t + par
            cache = {}

            def score():
                cols = []
                for c in range(2):
                    g0 = 4 * h + 2 * c
                    a0 = q_ref[0, t, g0 * HEAD_DIM:(g0 + 1) * HEAD_DIM, :].astype(F32)
                    a1 = q_ref[0, t, (g0 + 1) * HEAD_DIM:(g0 + 2) * HEAD_DIM, :].astype(F32)
                    cols.append(jnp.where(lo_q, a0, swap(a1)) if par == 0 else jnp.where(lo_q, swap(a0), a1))
                top = jnp.concatenate(cols, axis=1)
                if not exact:
                    qn = jnp.sqrt(jnp.sum(top * top, axis=0, keepdims=True))
                    cache["shift"] = qn * kmax[h] * SHIFT_SLACK + bias_max[h]
                k0 = pl.multiple_of(row_start(r) * GRID_W, GRID_W)
                return _dot(jnp.concatenate([k_ref[0, pl.ds(k0, nkey), :], kc_ref[0]], axis=0),
                            _padded_queries(top.astype(BF16), h))

            def finish(s):
                rs = row_start(r)
                ro0 = rs - r + (NA_ROWS - 1)
                bias = jnp.concatenate([bias_ref[h, ro0 + 2 * i] for i in range(kr // 2)], axis=0)
                p, l = _column_softmax([s[:nkey] + bias, s[nkey:]], None, cache.get("shift"))
                if not exact:
                    l_min[0] = jnp.minimum(l_min[0], l)
                vwin = [vall_ref[rs & 1, (rs >> 1) + i] for i in range(nkey // (2 * GRID_W))]
                o = _dot(jnp.concatenate(vwin + [vc_ref[0]], axis=1), p)[h * HEAD_DIM:(h + 1) * HEAD_DIM] / l
                done.append(o)
                if par == 1:
                    o_even, o_odd = done
                    ws = []
                    for c in range(2):
                        ce = o_even[:, c * 2 * GRID_W:(c + 1) * 2 * GRID_W]
                        co = o_odd[:, c * 2 * GRID_W:(c + 1) * 2 * GRID_W]
                        ws.append(jnp.where(lo_q, ce, swap(co)))
                        ws.append(jnp.where(lo_q, swap(ce), co))
                    w = jnp.concatenate(ws, axis=0)
                    q0 = pl.multiple_of(t * 2 * GRID_W, 2 * GRID_W)
                    o_ref[0, pl.ds(q0, 2 * GRID_W), 4 * h * HEAD_DIM:4 * (h + 1) * HEAD_DIM] = w.T.astype(BF16)

            return score, finish

        def body(j, carry):
            l_min[0] = carry
            items = []
            for u in range(tiles_per_iter):
                for h in range(2):
                    done = []
                    items += [make_item(tiles_per_iter * j + u, par, h, done) for par in range(2)]
            _pipelined(items, 2)
            return l_min[0]

        return lax.fori_loop(0, n_tiles // tiles_per_iter, body, l_min[0])

    l_fast = run(exact=False)

    @pl.when(jnp.min(l_fast) < L_FLOOR)
    def _():
        run(exact=True)


def _neighbourhood_attn(q_t, k, v_t, kc, vc_t, bias):
    b, n_tiles, hd, tw = q_t.shape
    n = k.shape[1]
    c = kc.shape[1]
    assert tw == 2 * GRID_W and n_tiles * tw == n
    return pl.pallas_call(
        _na_kernel,
        grid=(b,),
        in_specs=[
            pl.BlockSpec((1, n_tiles, hd, tw), lambda bi: (bi, 0, 0, 0)),
            pl.BlockSpec((1, n, 2 * HEAD_DIM), lambda bi: (bi, 0, 0)),
            pl.BlockSpec((1, n_tiles, 2 * HEAD_DIM, tw), lambda bi: (bi, 0, 0, 0)),
            pl.BlockSpec((1, c, 2 * HEAD_DIM), lambda bi: (bi, 0, 0)),
            pl.BlockSpec((1, 2 * HEAD_DIM, c), lambda bi: (bi, 0, 0)),
            _const_spec(bias.shape),
        ],
        out_specs=pl.BlockSpec((1, n, hd), lambda bi: (bi, 0, 0)),
        out_shape=jax.ShapeDtypeStruct((b, n, hd), BF16),
        scratch_shapes=[pltpu.VMEM((2, n_tiles, 2 * HEAD_DIM, tw), BF16),
                        pltpu.VMEM((n // c + 1, 8, 2 * HEAD_DIM), F32)],
        compiler_params=_vmem(52 * 1024 * 1024),
        name="neighbourhood_attn",
    )(q_t, k, v_t, kc, vc_t, bias)


def _ctx_kernel(sink_ref, q_ref, k_ref, v_ref, o_ref):
    c = k_ref.shape[2]
    sink_col = _group_sink_col(sink_ref, pl.program_id(1) * 4, c)
    q = q_ref[0].reshape(4 * c, HEAD_DIM)
    (p,), l = _softmax_parts([_dot_nt(q, k_ref[0, 0])], sink_col)
    o = _dot(p.astype(BF16), v_ref[0, 0]) / l
    o_ref[0] = o.reshape(4, c, HEAD_DIM).astype(BF16)


def _ctx_attn(sink, q, k, v):
    b, hq, c, d = q.shape
    hkv = k.shape[1]
    return pl.pallas_call(
        _ctx_kernel,
        grid=(b, hkv),
        in_specs=[
            pl.BlockSpec(memory_space=pltpu.SMEM),
            pl.BlockSpec((1, 4, c, d), lambda bi, h: (bi, h, 0, 0)),
            pl.BlockSpec((1, 1, c, d), lambda bi, h: (bi, h, 0, 0)),
            pl.BlockSpec((1, 1, c, d), lambda bi, h: (bi, h, 0, 0)),
        ],
        out_specs=pl.BlockSpec((1, 4, c, d), lambda bi, h: (bi, h, 0, 0)),
        out_shape=jax.ShapeDtypeStruct((b, hq, c, d), BF16),
        name="ctx_attn",
    )(sink, q, k, v)


def _diff_kernel(lq1_ref, lk1_ref, lq2_ref, lk2_ref, g_ref, qt_ref, kk_ref, kkc_ref, vt_ref, vtc_ref, o_ref,
                 qz_ref, kb_ref, kmax_ref, m_ref, l_ref, acc1_ref, acc2_ref, *, lam_init, n_sub):
    tq = qt_ref.shape[-1] // n_sub
    n_lt, dv, tk = vt_ref.shape[2:]
    n_kt = n_lt + 1
    assert kkc_ref.shape[2] == tk and vtc_ref.shape[2] == 1

    def k_tile(j):
        return kk_ref[0, 0, j * tk:(j + 1) * tk, :] if j < n_lt else kkc_ref[0, 0]

    def v_tile(j):
        return vt_ref[0, 0, j] if j < n_lt else vtc_ref[0, 0, 0]

    @pl.when(pl.program_id(2) == 0)
    def _():
        def group_max(kt):
            kf = kt.astype(F32)
            return jnp.max((kf * kf).reshape(tk // 8, 8, dv), axis=0)

        def kbody(j, carry):
            kb_ref[j] = group_max(kk_ref[0, 0, pl.ds(pl.multiple_of(j * tk, tk), tk), :])
            return carry

        lax.fori_loop(0, n_lt, kbody, 0)
        kb_ref[n_lt] = group_max(kkc_ref[0, 0])
        best = kb_ref[...].reshape(n_kt * 8, dv)
        half = lax.broadcasted_iota(jnp.int32, best.shape, 1) < HEAD_DIM
        k1 = jnp.max(jnp.sum(jnp.where(half, best, 0.0), axis=1, keepdims=True), axis=0, keepdims=True)
        k2 = jnp.max(jnp.sum(jnp.where(half, 0.0, best), axis=1, keepdims=True), axis=0, keepdims=True)
        kmax_ref[:, :tq] = jnp.broadcast_to(jnp.sqrt(k1), (1, tq))
        kmax_ref[:, tq:] = jnp.broadcast_to(jnp.sqrt(k2), (1, tq))

    zero = jnp.zeros((HEAD_DIM, tq), BF16)
    state = []
    for s in range(n_sub):
        qt = qt_ref[0, 0, :, s * tq:(s + 1) * tq]
        qz_ref[s, 0:HEAD_DIM, 0:tq] = qt[0:HEAD_DIM]
        qz_ref[s, 0:HEAD_DIM, tq:] = zero
        qz_ref[s, HEAD_DIM:, 0:tq] = zero
        qz_ref[s, HEAD_DIM:, tq:] = qt[HEAD_DIM:]
        qf = qt.astype(F32)
        qsq = qf * qf
        qn = jnp.concatenate([jnp.sum(qsq[:HEAD_DIM], axis=0, keepdims=True),
                              jnp.sum(qsq[HEAD_DIM:], axis=0, keepdims=True)], axis=1)
        state.append(dict(qz=qz_ref[s], shift=jnp.sqrt(qn) * kmax_ref[...] * (1.0 + 2.0 ** -10),
                          l8=jnp.zeros((8, 2 * tq), F32), acc1=jnp.zeros((dv, tq), F32), acc2=jnp.zeros((dv, tq), F32)))

    lam = (jnp.exp(jnp.sum(lq1_ref[...] * lk1_ref[...], axis=-1, keepdims=True))
           - jnp.exp(jnp.sum(lq2_ref[...] * lk2_ref[...], axis=-1, keepdims=True)) + lam_init)

    def finalize(s, l, acc1, acc2):
        o = acc1 / l[:, :tq] - lam * (acc2 / l[:, tq:])
        ms = jnp.mean(o * o, axis=0, keepdims=True)
        o = (o * lax.rsqrt(ms + EPS) * g_ref[...]) * (1.0 - lam_init)
        o_ref[0, s * tq:(s + 1) * tq, :] = o.T.astype(BF16)

    def make_item(s, j):
        st = state[s]

        def finish(scores):
            p = jnp.exp2(scores - st["shift"])
            st["l8"] = st["l8"] + jnp.sum(p.reshape(tk // 8, 8, 2 * tq), axis=0)
            pb = p.astype(BF16)
            vt = v_tile(j)
            st["acc1"] = st["acc1"] + _dot(vt, pb[:, :tq])
            st["acc2"] = st["acc2"] + _dot(vt, pb[:, tq:])
            if j == n_kt - 1:
                l_fast = jnp.sum(st["l8"], axis=0, keepdims=True)
                l_ref[s] = l_fast
                finalize(s, l_fast, st["acc1"], st["acc2"])

        return (lambda: _dot(k_tile(j), st["qz"])), finish

    _pipelined([make_item(s, j) for s in range(n_sub) for j in range(n_kt)], 2)
    l_min = functools.reduce(jnp.minimum, [l_ref[s] for s in range(n_sub)])

    @pl.when(jnp.min(l_min) < 2.0 ** -80)
    def _():
        for s in range(n_sub):
            m_ref[...] = jnp.full(m_ref.shape, NEG, F32)
            l_ref[s] = jnp.zeros((1, 2 * tq), F32)
            acc1_ref[s] = jnp.zeros((dv, tq), F32)
            acc2_ref[s] = jnp.zeros((dv, tq), F32)

            def step(kt, vt):
                st = _dot(kt, qz_ref[s])
                m_old = m_ref[...]
                m_new = jnp.maximum(m_old, jnp.max(st, axis=0, keepdims=True))
                alpha = jnp.exp2(m_old - m_new)
                p = jnp.exp2(st - m_new)
                l_ref[s] = alpha * l_ref[s] + jnp.sum(p, axis=0, keepdims=True)
                m_ref[...] = m_new
                pb = p.astype(BF16)
                acc1_ref[s] = alpha[:, :tq] * acc1_ref[s] + _dot(vt, pb[:, :tq])
                acc2_ref[s] = alpha[:, tq:] * acc2_ref[s] + _dot(vt, pb[:, tq:])

            def body(j, carry):
                step(kk_ref[0, 0, pl.ds(pl.multiple_of(j * tk, tk), tk), :], vt_ref[0, 0, j])
                return carry

            lax.fori_loop(0, n_lt, body, 0)
            step(kkc_ref[0, 0], vtc_ref[0, 0, 0])
            finalize(s, l_ref[s], acc1_ref[s], acc2_ref[s])


def _diff_attn(lam_vecs, sub_g, qt, kk, kkc, vt, vtc, lam_init, tq_block, n_sub):
    b, h, dv, n = qt.shape
    n_ctx = kkc.shape[2]
    n_lt, _, tk = vt.shape[2:]
    tq = tq_block // n_sub
    vec_spec = pl.BlockSpec((1, HEAD_DIM), lambda bi, hi, i: (0, 0))
    return pl.pallas_call(
        functools.partial(_diff_kernel, lam_init=lam_init, n_sub=n_sub),
        grid=(b, h, n // tq_block),
        in_specs=[vec_spec] * 4 + [
            pl.BlockSpec((dv, 1), lambda bi, hi, i: (0, 0)),
            pl.BlockSpec((1, 1, dv, tq_block), lambda bi, hi, i: (bi, hi, 0, i)),
            pl.BlockSpec((1, 1, n, dv), lambda bi, hi, i: (bi, hi, 0, 0)),
            pl.BlockSpec((1, 1, n_ctx, dv), lambda bi, hi, i: (bi, hi, 0, 0)),
            pl.BlockSpec((1, 1, n_lt, dv, tk), lambda bi, hi, i: (bi, hi, 0, 0, 0)),
            pl.BlockSpec((1, 1, 1, dv, tk), lambda bi, hi, i: (bi, hi, 0, 0, 0)),
        ],
        out_specs=pl.BlockSpec((1, tq_block, dv), lambda bi, hi, i: (bi, i, hi)),
        out_shape=jax.ShapeDtypeStruct((b, n, h * dv), BF16),
        scratch_shapes=[
            pltpu.VMEM((n_sub, dv, 2 * tq), BF16),
            pltpu.VMEM((n_lt + 1, 8, dv), F32),
            pltpu.VMEM((1, 2 * tq), F32),
            pltpu.VMEM((1, 2 * tq), F32),
            pltpu.VMEM((n_sub, 1, 2 * tq), F32),
            pltpu.VMEM((n_sub, dv, tq), F32),
            pltpu.VMEM((n_sub, dv, tq), F32),
        ],
        compiler_params=pltpu.CompilerParams(
            dimension_semantics=("arbitrary", "arbitrary", "arbitrary"), vmem_limit_bytes=32 * 1024 * 1024),
        name="diff_attn",
    )(*lam_vecs, sub_g.reshape(dv, 1), qt, kk, kkc, vt, vtc)


def _mlp_kernel(*refs, n_y, final, ff_chunk):
    y_refs, refs = refs[:n_y], refs[n_y:]
    if final:
        h_ref, mod_ref, g_ref, wo_ref, w1_ref, w2_ref, fg_ref, o_ref = refs
    else:
        h_ref, mod_ref, g_ref, wo_ref, w1_ref, w2_ref, o_ref = refs
    yw = wo_ref.shape[0] // n_y
    y = functools.reduce(jnp.add, [_dot(y_ref[0], wo_ref[j * yw:(j + 1) * yw, :]) for j, y_ref in enumerate(y_refs)])
    hx = h_ref[0] + mod_ref[0, 2:3, :] * y
    a = _modulated(hx, g_ref[...], mod_ref[0, 3:4, :], mod_ref[0, 4:5, :]).astype(BF16)
    acc = jnp.zeros(hx.shape, F32)
    for c0 in range(0, w1_ref.shape[1], ff_chunk):
        u = jnp.maximum(_dot(a, w1_ref[:, c0:c0 + ff_chunk]), 0.0)
        acc = acc + _dot((u * u).astype(BF16), w2_ref[c0:c0 + ff_chunk, :])
    out = hx + mod_ref[0, 5:6, :] * acc
    if final:
        ms = jnp.mean(out * out, axis=-1, keepdims=True)
        out = out * lax.rsqrt(ms + EPS) * fg_ref[...]
    o_ref[0] = out


def _out_mlp(ys, h, mod9, mod_row, g2, wo, w1, w2, final_g, tm, name):
    b, n, d = h.shape
    dff = w1.shape[1]
    final = final_g is not None
    if mod_row is None:
        mod_map = lambda bi, i: (bi, 0, 0)
    else:
        mod_map = lambda bi, i: (mod_row, 0, 0)
    tile = pl.BlockSpec((1, tm, d), lambda bi, i: (bi, i, 0))
    in_specs = [pl.BlockSpec((1, tm, y.shape[-1]), lambda bi, i: (bi, i, 0)) for y in ys]
    in_specs += [tile, pl.BlockSpec((1, 6, d), mod_map), _const_spec((1, d)),
                 _const_spec((d, d)), _const_spec((d, dff)), _const_spec((dff, d))]
    args = list(ys) + [h, mod9, g2.reshape(1, d), wo, w1, w2]
    if final:
        in_specs.append(_const_spec((1, d)))
        args.append(final_g.reshape(1, d))
    return pl.pallas_call(
        functools.partial(_mlp_kernel, n_y=len(ys), final=final, ff_chunk=1024),
        grid=(b, n // tm),
        in_specs=in_specs,
        out_specs=tile,
        out_shape=jax.ShapeDtypeStruct((b, n, d), F32),
        compiler_params=_vmem(56 * 1024 * 1024),
        name=name,
    )(*args)


def _rope_tables(n):
    t = jnp.arange(n, dtype=jnp.int32)
    row = (t // GRID_W).astype(F32)
    col = (t % GRID_W).astype(F32)
    quarter = HEAD_DIM // 4
    inv_freq = ROPE_THETA ** (-jnp.arange(quarter, dtype=F32) / quarter)
    ar = row[:, None] * inv_freq[None, :]
    ac = col[:, None] * inv_freq[None, :]
    ang = jnp.concatenate([ar, ar, ac, ac], axis=-1)
    cos, sin = jnp.cos(ang), jnp.sin(ang)
    even = (jnp.arange(HEAD_DIM) // quarter) % 2 == 0
    rep = LANES // HEAD_DIM
    sin_a = jnp.tile(jnp.where(even, -sin, 0.0), (1, rep))
    sin_b = jnp.tile(jnp.where(even, 0.0, sin), (1, rep))
    token_major = (jnp.tile(cos, (1, rep)), sin_a, sin_b)
    feature_major = (cos.T, jnp.where(even, -sin, sin).T)
    return token_major, feature_major


def _heads(p, lo, hi, width=HEAD_DIM):
    b, n, _ = p.shape
    return p[:, :, lo:hi].reshape(b, n, (hi - lo) // width, width).transpose(0, 2, 1, 3)


def _unheads(y):
    b, h, n, d = y.shape
    return y.transpose(0, 2, 1, 3).reshape(b, n, h * d)


def kernel(x, c, ctx, c_ctx, ada_w, ada_b, norm1_g, norm2_g, even_w_in, even_w_out, a_sink, b_rpb,
           odd_w_in, odd_w_out, lam_q1, lam_k1, lam_q2, lam_k2, subln_g, mlp_w1, mlp_w2, final_g):
    b, n, d = x.shape
    n_ctx = ctx.shape[1]
    assert ada_w.shape[0] == 2 and d == 1024 and n % 512 == 0
    scale = HEAD_DIM ** -0.5
    a_qw, a_kvw = 8 * HEAD_DIM, 2 * HEAD_DIM
    b_q0 = a_qw + 2 * a_kvw

    cc = jnp.zeros((16, d), F32).at[:b].set(c).at[b].set(c_ctx)
    mod = _ada_call(cc, ada_w, ada_b)[:, :b + 1].reshape(2, b + 1, 6, d)
    rope, rope_t = _rope_tables(n)

    qscale = scale * LOG2E
    cols0 = jnp.arange(even_w_in.shape[2])
    qcols0 = (cols0 < a_qw) | ((cols0 >= b_q0) & (cols0 < b_q0 + a_qw))
    w_in0 = (even_w_in[0] * jnp.where(qcols0, qscale, 1.0)).astype(BF16)
    cols1 = jnp.arange(odd_w_in.shape[2])
    w_in1 = (odd_w_in[0] * jnp.where(cols1 < 1024, qscale, 1.0)).astype(BF16)
    w_out0, w_out1 = even_w_out[0].astype(BF16), odd_w_out[0].astype(BF16)
    w1, w2 = mlp_w1.astype(BF16), mlp_w2.astype(BF16)

    ak0, av0, bq0, bk0, bv0 = a_qw, a_qw + a_kvw, b_q0, b_q0 + a_qw, b_q0 + a_qw + a_kvw
    wk0 = jnp.concatenate([w_in0[:, ak0:av0], w_in0[:, bk0:bv0]], axis=1)
    wt0 = jnp.concatenate([w_in0[:, :ak0], w_in0[:, av0:bq0], w_in0[:, bq0:bk0], w_in0[:, bv0:]], axis=1).T
    aq_t, ak, av_t, bq_t, bk, bv_t = _project0(x, mod[0], norm1_g[0], wk0, wt0, rope, rope_t, 1024, WINDOW)
    pc = _project(ctx, mod[0], b, norm1_g[0], w_in0, None, n_ctx, "proj0_ctx")
    cak, cav_t = pc[:, :, ak0:av0], pc[:, :, av0:bq0].transpose(0, 2, 1)
    cbk, cbv_t = pc[:, :, bk0:bv0], pc[:, :, bv0:].transpose(0, 2, 1)
    sink = a_sink[0].astype(F32) * LOG2E
    ya = _window_attn(sink, aq_t, ak, av_t, cak, cav_t)
    yb = _neighbourhood_attn(bq_t, bk, bv_t, cbk, cbv_t, _na_bias(b_rpb[0].astype(F32)))
    caq, cbq = _heads(pc, 0, ak0), _heads(pc, bq0, bk0)
    sink_ab = jnp.concatenate([sink, jnp.full_like(sink, NEG)])
    yc = _unheads(_ctx_attn(sink_ab, jnp.concatenate([caq, cbq], axis=1),
                            jnp.concatenate([_heads(pc, ak0, av0), _heads(pc, bk0, bv0)], axis=1),
                            jnp.concatenate([_heads(pc, av0, bq0), _heads(pc, bv0, bv0 + a_kvw)], axis=1)))
    hx = _out_mlp([ya, yb], x, mod[0], None, norm2_g[0], w_out0, w1[0], w2[0], None, 512, "mlp0_x")
    hc = _out_mlp([yc], ctx, mod[0], b, norm2_g[0], w_out0, w1[0], w2[0], None, n_ctx, "mlp0_ctx")

    lam_init = 0.8 - 0.6 * math.exp(-0.3 * 1)
    tk = n_ctx
    wq1, wk1, wv1 = w_in1[:, :1024], w_in1[:, 1024:2048], w_in1[:, 2048:]
    qt, kk, vt = _project1(hx, mod[1], None, norm1_g[1], wk1, jnp.concatenate([wq1, wv1], axis=1).T,
                           rope, rope_t, 1024, tk, "proj1_x")
    kkc, vtc = _project1(hc, mod[1], b, norm1_g[1], wk1, wv1.T, None, None, n_ctx, tk, "proj1_ctx")
    lam_vecs = [v[0].reshape(1, HEAD_DIM).astype(F32) for v in (lam_q1, lam_k1, lam_q2, lam_k2)]
    yx = _diff_attn(lam_vecs, subln_g[0].astype(F32), qt, kk, kkc, vt, vtc, lam_init, 2048, 8)
    return _out_mlp([yx], hx, mod[1], None, norm2_g[1], w_out1, w1[1], w2[1], final_g, 512, "mlp1_x")
```

```python
import functools
import math

import jax
import jax.numpy as jnp
from jax import lax
from jax.experimental import pallas as pl
from jax.experimental.pallas import tpu as pltpu

F32 = jnp.float32
BF16 = jnp.bfloat16

GRID_W = 64
HEAD_DIM = 64
WINDOW = 128
NA_ROWS = 8
NA_COLS = 16
ROPE_THETA = 10000.0
EPS = 1e-6
NEG = -1e30
LOG2E = math.log2(math.e)

V7X_VMEM_BYTES = 64 * 1024 * 1024
LANES = 128


def _vmem(nbytes):
    return pltpu.CompilerParams(vmem_limit_bytes=int(min(nbytes, V7X_VMEM_BYTES - 8 * 1024 * 1024)))


def _const_spec(shape):
    n = len(shape)
    return pl.BlockSpec(shape, lambda *_: (0,) * n, pipeline_mode=pl.Buffered(1))


def _dot(a, b):
    return jnp.dot(a, b, preferred_element_type=F32)


def _dot_nt(a, b):
    return lax.dot_general(a, b, (((1,), (1,)), ((), ())), preferred_element_type=F32)


def _ada_kernel(c_ref, w_ref, b_ref, o_ref):
    cc = c_ref[...]
    s = cc * jax.nn.sigmoid(cc)
    o_ref[0] = jnp.dot(s, w_ref[0], preferred_element_type=F32, precision=lax.Precision.HIGHEST) + b_ref[0]


def _ada_call(cc, ada_w, ada_b):
    depth, d, six_d = ada_w.shape
    tn = 1536
    return pl.pallas_call(
        _ada_kernel,
        grid=(depth, six_d // tn),
        in_specs=[
            pl.BlockSpec((16, d), lambda l, j: (0, 0)),
            pl.BlockSpec((1, d, tn), lambda l, j: (l, 0, j)),
            pl.BlockSpec((1, 1, tn), lambda l, j: (l, 0, j)),
        ],
        out_specs=pl.BlockSpec((1, 16, tn), lambda l, j: (l, 0, j)),
        out_shape=jax.ShapeDtypeStruct((depth, 16, six_d), F32),
        compiler_params=_vmem(32 * 1024 * 1024),
        name="ada_mod",
    )(cc, ada_w, ada_b.reshape(depth, 1, six_d))


def _modulated(x, g, shift, scale):
    ms = jnp.mean(x * x, axis=-1, keepdims=True)
    return (x * lax.rsqrt(ms + EPS) * g) * (1.0 + scale) + shift


def _rope_lanes(x, cos, sin_a, sin_b):
    return x * cos + pltpu.roll(x, LANES - 16, 1) * sin_a + pltpu.roll(x, 16, 1) * sin_b


def _rope_rows(x, cos_t, sin_t):
    q = HEAD_DIM // 4
    rot = jnp.concatenate([x[q:2 * q], x[0:q], x[3 * q:4 * q], x[2 * q:3 * q]], axis=0)
    return x * cos_t + rot * sin_t


def _proj1_kernel(h_ref, mod_ref, g_ref, wk_ref, wt_ref, cos_ref, sa_ref, sb_ref, cost_ref, sint_ref,
                  qt_ref, kk_ref, vt_ref, *, tk):
    a = _modulated(h_ref[0], g_ref[...], mod_ref[0, 0:1, :], mod_ref[0, 1:2, :]).astype(BF16)
    tm = a.shape[0]
    dv = 2 * HEAD_DIM
    n_heads = kk_ref.shape[1]
    chunk = 4 * dv
    for c0 in range(0, n_heads * dv, chunk):
        p = _dot(a, wk_ref[:, c0:c0 + chunk])
        for j in range(chunk // dv):
            xc = _rope_lanes(p[:, j * dv:(j + 1) * dv], cos_ref[...], sa_ref[...], sb_ref[...])
            kk_ref[0, c0 // dv + j] = xc.astype(BF16)
    n_q = n_heads * dv
    for f0 in range(0, wt_ref.shape[0], chunk):
        xt = _dot_nt(wt_ref[f0:f0 + chunk, :], a)
        for j in range(chunk // dv):
            row = f0 + j * dv
            hd = xt[j * dv:(j + 1) * dv]
            if row < n_q:
                hd = jnp.concatenate([_rope_rows(hd[m * HEAD_DIM:(m + 1) * HEAD_DIM], cost_ref[...], sint_ref[...])
                                      for m in range(2)], axis=0)
                qt_ref[0, row // dv] = hd.astype(BF16)
            else:
                for t in range(tm // tk):
                    vt_ref[0, (row - n_q) // dv, t] = hd[:, t * tk:(t + 1) * tk].astype(BF16)


def _project1(h, mod9, g, wk, wt, rope, rope_t, tm, tk):
    b, n, d = h.shape
    dv = 2 * HEAD_DIM
    n_heads = wk.shape[1] // dv
    return pl.pallas_call(
        functools.partial(_proj1_kernel, tk=tk),
        grid=(b, n // tm),
        in_specs=[
            pl.BlockSpec((1, tm, d), lambda bi, i: (bi, i, 0)),
            pl.BlockSpec((1, 6, d), lambda bi, i: (bi, 0, 0)),
            _const_spec((1, d)),
            _const_spec(wk.shape),
            _const_spec(wt.shape),
        ] + [pl.BlockSpec((tm, LANES), lambda bi, i: (i, 0))] * 3
          + [pl.BlockSpec((HEAD_DIM, tm), lambda bi, i: (0, i))] * 2,
        out_specs=[
            pl.BlockSpec((1, n_heads, dv, tm), lambda bi, i: (bi, 0, 0, i)),
            pl.BlockSpec((1, n_heads, tm, dv), lambda bi, i: (bi, 0, i, 0)),
            pl.BlockSpec((1, n_heads, tm // tk, dv, tk), lambda bi, i: (bi, 0, i, 0, 0)),
        ],
        out_shape=[
            jax.ShapeDtypeStruct((b, n_heads, dv, n), BF16),
            jax.ShapeDtypeStruct((b, n_heads, n, dv), BF16),
            jax.ShapeDtypeStruct((b, n_heads, n // tk, dv, tk), BF16),
        ],
        compiler_params=_vmem(48 * 1024 * 1024),
        name="proj1_x",
    )(h, mod9, g.reshape(1, d), wk, wt, *rope, *rope_t)


def _proj0_kernel(h_ref, mod_ref, g_ref, wk_ref, wt_ref, cos_ref, sa_ref, sb_ref, cost_ref, sint_ref,
                  qa_ref, ka_ref, va_ref, qb_ref, kb_ref, vb_ref, *, tw):
    a = _modulated(h_ref[0], g_ref[...], mod_ref[0, 0:1, :], mod_ref[0, 1:2, :]).astype(BF16)
    tm = a.shape[0]
    kw = ka_ref.shape[-1]
    p = _dot(a, wk_ref[...])
    ka_ref[0] = _rope_lanes(p[:, :kw], cos_ref[...], sa_ref[...], sb_ref[...]).astype(BF16)
    kb_ref[0] = p[:, kw:].astype(BF16)
    qw = qa_ref.shape[2]
    rows = qw + va_ref.shape[2]
    for mixer, (q_ref, v_ref) in enumerate(((qa_ref, va_ref), (qb_ref, vb_ref))):
        xt = _dot_nt(wt_ref[mixer * rows:(mixer + 1) * rows, :], a)
        q = xt[:qw]
        if mixer == 0:
            q = jnp.concatenate([_rope_rows(q[h * HEAD_DIM:(h + 1) * HEAD_DIM], cost_ref[...], sint_ref[...])
                                 for h in range(qw // HEAD_DIM)], axis=0)
        for t in range(tm // tw):
            q_ref[0, t] = q[:, t * tw:(t + 1) * tw].astype(BF16)
            v_ref[0, t] = xt[qw:, t * tw:(t + 1) * tw].astype(BF16)


def _project0(h, mod9, g, wk, wt, rope, rope_t, tm, tw):
    b, n, d = h.shape
    kw = wk.shape[1] // 2
    rows = wt.shape[0] // 2
    qw = rows - kw
    q_spec = pl.BlockSpec((1, tm // tw, qw, tw), lambda bi, i: (bi, i, 0, 0))
    k_spec = pl.BlockSpec((1, tm, kw), lambda bi, i: (bi, i, 0))
    v_spec = pl.BlockSpec((1, tm // tw, kw, tw), lambda bi, i: (bi, i, 0, 0))
    q_shape = jax.ShapeDtypeStruct((b, n // tw, qw, tw), BF16)
    k_shape = jax.ShapeDtypeStruct((b, n, kw), BF16)
    v_shape = jax.ShapeDtypeStruct((b, n // tw, kw, tw), BF16)
    return pl.pallas_call(
        functools.partial(_proj0_kernel, tw=tw),
        grid=(b, n // tm),
        in_specs=[
            pl.BlockSpec((1, tm, d), lambda bi, i: (bi, i, 0)),
            pl.BlockSpec((1, 6, d), lambda bi, i: (bi, 0, 0)),
            _const_spec((1, d)),
            _const_spec(wk.shape),
            _const_spec(wt.shape),
        ] + [pl.BlockSpec((tm, LANES), lambda bi, i: (i, 0))] * 3
          + [pl.BlockSpec((HEAD_DIM, tm), lambda bi, i: (0, i))] * 2,
        out_specs=[q_spec, k_spec, v_spec, q_spec, k_spec, v_spec],
        out_shape=[q_shape, k_shape, v_shape, q_shape, k_shape, v_shape],
        compiler_params=_vmem(48 * 1024 * 1024),
        name="proj0_x",
    )(h, mod9, g.reshape(1, d), wk, wt, *rope, *rope_t)


def _pipelined(items, ahead):
    scores = [items[k][0]() for k in range(min(ahead, len(items)))]
    for k in range(len(items)):
        if k + ahead < len(items):
            scores.append(items[k + ahead][0]())
        items[k][1](scores[k])
        scores[k] = None


SHIFT_SLACK = 1.0 + 2.0 ** -10
L_FLOOR = 2.0 ** -80


def _key_norm_bounds(k_ref, kc_ref, kb_ref):
    tile = kc_ref.shape[1]
    nt = k_ref.shape[1] // tile
    width = k_ref.shape[2]

    def group_max(kt):
        kf = kt.astype(F32)
        return jnp.max((kf * kf).reshape(tile // 8, 8, width), axis=0)

    def body(j, carry):
        kb_ref[j] = group_max(k_ref[0, pl.ds(pl.multiple_of(j * tile, tile), tile), :])
        return carry

    lax.fori_loop(0, nt, body, 0)
    kb_ref[nt] = group_max(kc_ref[0])
    best = kb_ref[...].reshape((nt + 1) * 8, width)
    half = lax.broadcasted_iota(jnp.int32, best.shape, 1) < width // 2
    k0 = jnp.max(jnp.sum(jnp.where(half, best, 0.0), axis=1, keepdims=True), axis=0, keepdims=True)
    k1 = jnp.max(jnp.sum(jnp.where(half, 0.0, best), axis=1, keepdims=True), axis=0, keepdims=True)
    return jnp.sqrt(k0), jnp.sqrt(k1)


def _column_softmax(pieces, extra_row=None, shift=None):
    if shift is None:
        m = functools.reduce(jnp.maximum, [jnp.max(s, axis=0, keepdims=True) for s in pieces])
        if extra_row is not None:
            m = jnp.maximum(m, extra_row)
    else:
        m = shift
    ps = [jnp.exp2(s - m) for s in pieces]
    l = functools.reduce(jnp.add, [jnp.sum(p, axis=0, keepdims=True) for p in ps])
    if extra_row is not None:
        l = l + jnp.exp2(extra_row - m)
    return jnp.concatenate([p.astype(BF16) for p in ps], axis=0), l


def _padded_queries(top, h):
    z = jnp.zeros_like(top)
    return jnp.concatenate([top, z] if h == 0 else [z, top], axis=0)


def _win_kernel(sink_ref, q_ref, k_ref, v_ref, kc_ref, vc_ref, o_ref, kb_ref, *, blk, per_iter):
    n = k_ref.shape[1]
    nb = n // blk
    c_idx = lax.broadcasted_iota(jnp.int32, (blk, blk), 0)
    r_idx = lax.broadcasted_iota(jnp.int32, (blk, blk), 1)
    lower = jnp.tile(jnp.where(c_idx >= r_idx, 0.0, NEG), (1, 4))
    upper = jnp.tile(jnp.where(c_idx <= r_idx, 0.0, NEG), (1, 4))
    lane_head = lax.broadcasted_iota(jnp.int32, (1, 4 * blk), 1) // blk

    def sink_row(h):
        row = jnp.full((1, 4 * blk), sink_ref[4 * h + 3], F32)
        for g in range(3):
            row = jnp.where(lane_head == g, sink_ref[4 * h + g], row)
        return row

    sinks = [sink_row(0), sink_row(1)]
    kmax = _key_norm_bounds(k_ref, kc_ref, kb_ref)

    def run(exact):
        l_min = [jnp.full((1, 4 * blk), jnp.inf, F32)]

        def make_item(i, h, k_rows, v_tiles, biases):
            cache = {}

            def score():
                top = jnp.concatenate([q_ref[0, i, (4 * h + g) * HEAD_DIM:(4 * h + g + 1) * HEAD_DIM, :]
                                       for g in range(4)], axis=1)
                if not exact:
                    tf = top.astype(F32)
                    qn = jnp.sqrt(jnp.sum(tf * tf, axis=0, keepdims=True))
                    cache["shift"] = jnp.maximum(qn * kmax[h] * SHIFT_SLACK, sinks[h])
                return _dot(jnp.concatenate([k_rows(), kc_ref[0]], axis=0), _padded_queries(top, h))

            def finish(s):
                pieces = [s[t * blk:(t + 1) * blk] if bias is None else s[t * blk:(t + 1) * blk] + bias
                          for t, bias in enumerate(biases)]
                pieces.append(s[len(biases) * blk:])
                p, l = _column_softmax(pieces, sinks[h], cache.get("shift"))
                if not exact:
                    l_min[0] = jnp.minimum(l_min[0], l)
                vall = jnp.concatenate(v_tiles() + [vc_ref[0]], axis=1)
                o = _dot(vall, p)[h * HEAD_DIM:(h + 1) * HEAD_DIM] / l
                o4 = jnp.concatenate([o[:, g * blk:(g + 1) * blk] for g in range(4)], axis=0)
                q0 = i * blk if isinstance(i, int) else pl.multiple_of(i * blk, blk)
                o_ref[0, pl.ds(q0, blk), 4 * h * HEAD_DIM:4 * (h + 1) * HEAD_DIM] = o4.T.astype(BF16)

            return score, finish

        def edge_items(i, t0, biases):
            return [make_item(i, h, lambda: k_ref[0, t0 * blk:(t0 + 2) * blk, :],
                              lambda: [v_ref[0, t0], v_ref[0, t0 + 1]], biases) for h in range(2)]

        def inner_items(i):
            k0 = pl.multiple_of((i - 1) * blk, blk)
            return [make_item(i, h, lambda: k_ref[0, pl.ds(k0, 3 * blk), :],
                              lambda: [v_ref[0, i - 1], v_ref[0, i], v_ref[0, i + 1]], [lower, None, upper])
                    for h in range(2)]

        _pipelined(edge_items(0, 0, [None, upper]), 1)

        def body(j, carry):
            l_min[0] = carry
            _pipelined([it for u in range(per_iter) for it in inner_items(1 + per_iter * j + u)], 2)
            return l_min[0]

        l_min[0] = lax.fori_loop(0, (nb - 2) // per_iter, body, l_min[0])
        _pipelined(edge_items(nb - 1, nb - 2, [lower, None]), 1)
        return l_min[0]

    l_fast = run(exact=False)

    @pl.when(jnp.min(l_fast) < L_FLOOR)
    def _():
        run(exact=True)


def _window_attn(sink, q_t, k, v_t, kc, vc_t):
    b, nb, hd, blk = q_t.shape
    n = k.shape[1]
    c = kc.shape[1]
    per_iter = 5
    assert blk == WINDOW and (nb - 2) % per_iter == 0
    return pl.pallas_call(
        functools.partial(_win_kernel, blk=blk, per_iter=per_iter),
        grid=(b,),
        in_specs=[
            pl.BlockSpec(memory_space=pltpu.SMEM),
            pl.BlockSpec((1, nb, hd, blk), lambda bi: (bi, 0, 0, 0)),
            pl.BlockSpec((1, n, 2 * HEAD_DIM), lambda bi: (bi, 0, 0)),
            pl.BlockSpec((1, nb, 2 * HEAD_DIM, blk), lambda bi: (bi, 0, 0, 0)),
            pl.BlockSpec((1, c, 2 * HEAD_DIM), lambda bi: (bi, 0, 0)),
            pl.BlockSpec((1, 2 * HEAD_DIM, c), lambda bi: (bi, 0, 0)),
        ],
        out_specs=pl.BlockSpec((1, n, hd), lambda bi: (bi, 0, 0)),
        out_shape=jax.ShapeDtypeStruct((b, n, hd), BF16),
        scratch_shapes=[pltpu.VMEM((n // c + 1, 8, 2 * HEAD_DIM), F32)],
        compiler_params=_vmem(48 * 1024 * 1024),
        name="window_attn",
    )(sink, q_t, k, v_t, kc, vc_t)


def _na_bias_kernel(rpb_ref, o_ref):
    h = pl.program_id(0)
    n_roff = 2 * NA_ROWS - 1
    n_coff = 2 * NA_COLS - 1
    shape = (GRID_W, 4 * GRID_W)
    ck = lax.broadcasted_iota(jnp.int32, shape, 0)
    lane = lax.broadcasted_iota(jnp.int32, shape, 1)
    cq = lane % GRID_W
    head = lane // GRID_W
    cstart = jnp.clip(cq - NA_COLS // 2, 0, GRID_W - NA_COLS)
    valid = (ck >= cstart) & (ck < cstart + NA_COLS)
    diff = ck - cq + (NA_COLS - 1)
    hits = [[valid & (diff == o) & (head == g) for o in range(n_coff)] for g in range(4)]
    tiles = []
    for roff in range(n_roff):
        t = jnp.full(shape, NEG, F32)
        for g in range(4):
            for o in range(n_coff):
                t = jnp.where(hits[g][o], rpb_ref[((4 * h + g) * n_roff + roff) * n_coff + o] * LOG2E, t)
        tiles.append(t)
    for ro in range(n_roff - 1):
        o_ref[0, ro] = jnp.concatenate([tiles[ro], tiles[ro + 1]], axis=0)


def _na_bias(rpb):
    hq, n_roff, n_coff = rpb.shape
    return pl.pallas_call(
        _na_bias_kernel,
        grid=(hq // 4,),
        in_specs=[pl.BlockSpec(memory_space=pltpu.SMEM)],
        out_specs=pl.BlockSpec((1, n_roff - 1, 2 * GRID_W, 4 * GRID_W), lambda h: (h, 0, 0, 0)),
        out_shape=jax.ShapeDtypeStruct((hq // 4, n_roff - 1, 2 * GRID_W, 4 * GRID_W), F32),
        name="na_bias",
    )(rpb.reshape(-1))


def _na_kernel(q_ref, k_ref, v_ref, kc_ref, vc_ref, bias_ref, o_ref, vall_ref, kb_ref):
    n = k_ref.shape[1]
    rows = n // GRID_W
    n_tiles = rows // 2
    kr = min(NA_ROWS, rows)
    nkey = kr * GRID_W
    lo = lax.broadcasted_iota(jnp.int32, (2 * HEAD_DIM, 2 * GRID_W), 1) < GRID_W
    lo_q = lax.broadcasted_iota(jnp.int32, (HEAD_DIM, 2 * GRID_W), 1) < GRID_W

    def swap(x):
        return pltpu.roll(x, GRID_W, 1)

    def build(t, carry):
        a = v_ref[0, t]
        vall_ref[0, t] = a
        nxt = v_ref[0, jnp.minimum(t + 1, n_tiles - 1)]
        vall_ref[1, t] = jnp.where(lo, swap(a.astype(F32)), swap(nxt.astype(F32))).astype(BF16)
        return carry

    lax.fori_loop(0, n_tiles, build, 0)

    def row_start(r):
        return jnp.clip(r - kr // 2, 0, rows - kr)

    kmax = _key_norm_bounds(k_ref, kc_ref, kb_ref)
    bias_max = [jnp.maximum(jnp.max(jnp.max(bias_ref[h], axis=0), axis=0, keepdims=True), 0.0) for h in range(2)]
    tiles_per_iter = 4

    def run(exact):
        l_min = [jnp.full((1, 4 * GRID_W), jnp.inf, F32)]

        def make_item(t, par, h, done):
            r = 2 * t + par
            cache = {}

            def score():
                cols = []
                for c in range(2):
                    g0 = 4 * h + 2 * c
                    a0 = q_ref[0, t, g0 * HEAD_DIM:(g0 + 1) * HEAD_DIM, :].astype(F32)
                    a1 = q_ref[0, t, (g0 + 1) * HEAD_DIM:(g0 + 2) * HEAD_DIM, :].astype(F32)
                    cols.append(jnp.where(lo_q, a0, swap(a1)) if par == 0 else jnp.where(lo_q, swap(a0), a1))
                top = jnp.concatenate(cols, axis=1)
                if not exact:
                    qn = jnp.sqrt(jnp.sum(top * top, axis=0, keepdims=True))
                    cache["shift"] = qn * kmax[h] * SHIFT_SLACK + bias_max[h]
                k0 = pl.multiple_of(row_start(r) * GRID_W, GRID_W)
                return _dot(jnp.concatenate([k_ref[0, pl.ds(k0, nkey), :], kc_ref[0]], axis=0),
                            _padded_queries(top.astype(BF16), h))

            def finish(s):
                rs = row_start(r)
                ro0 = rs - r + (NA_ROWS - 1)
                bias = jnp.concatenate([bias_ref[h, ro0 + 2 * i] for i in range(kr // 2)], axis=0)
                p, l = _column_softmax([s[:nkey] + bias, s[nkey:]], None, cache.get("shift"))
                if not exact:
                    l_min[0] = jnp.minimum(l_min[0], l)
                vwin = [vall_ref[rs & 1, (rs >> 1) + i] for i in range(nkey // (2 * GRID_W))]
                o = _dot(jnp.concatenate(vwin + [vc_ref[0]], axis=1), p)[h * HEAD_DIM:(h + 1) * HEAD_DIM] / l
                done.append(o)
                if par == 1:
                    o_even, o_odd = done
                    ws = []
                    for c in range(2):
                        ce = o_even[:, c * 2 * GRID_W:(c + 1) * 2 * GRID_W]
                        co = o_odd[:, c * 2 * GRID_W:(c + 1) * 2 * GRID_W]
                        ws.append(jnp.where(lo_q, ce, swap(co)))
                        ws.append(jnp.where(lo_q, swap(ce), co))
                    w = jnp.concatenate(ws, axis=0)
                    q0 = pl.multiple_of(t * 2 * GRID_W, 2 * GRID_W)
                    o_ref[0, pl.ds(q0, 2 * GRID_W), 4 * h * HEAD_DIM:4 * (h + 1) * HEAD_DIM] = w.T.astype(BF16)

            return score, finish

        def body(j, carry):
            l_min[0] = carry
            items = []
            for u in range(tiles_per_iter):
                for h in range(2):
                    done = []
                    items += [make_item(tiles_per_iter * j + u, par, h, done) for par in range(2)]
            _pipelined(items, 2)
            return l_min[0]

        return lax.fori_loop(0, n_tiles // tiles_per_iter, body, l_min[0])

    l_fast = run(exact=False)

    @pl.when(jnp.min(l_fast) < L_FLOOR)
    def _():
        run(exact=True)


def _neighbourhood_attn(q_t, k, v_t, kc, vc_t, bias):
    b, n_tiles, hd, tw = q_t.shape
    n = k.shape[1]
    c = kc.shape[1]
    assert tw == 2 * GRID_W and n_tiles * tw == n
    return pl.pallas_call(
        _na_kernel,
        grid=(b,),
        in_specs=[
            pl.BlockSpec((1, n_tiles, hd, tw), lambda bi: (bi, 0, 0, 0)),
            pl.BlockSpec((1, n, 2 * HEAD_DIM), lambda bi: (bi, 0, 0)),
            pl.BlockSpec((1, n_tiles, 2 * HEAD_DIM, tw), lambda bi: (bi, 0, 0, 0)),
            pl.BlockSpec((1, c, 2 * HEAD_DIM), lambda bi: (bi, 0, 0)),
            pl.BlockSpec((1, 2 * HEAD_DIM, c), lambda bi: (bi, 0, 0)),
            _const_spec(bias.shape),
        ],
        out_specs=pl.BlockSpec((1, n, hd), lambda bi: (bi, 0, 0)),
        out_shape=jax.ShapeDtypeStruct((b, n, hd), BF16),
        scratch_shapes=[pltpu.VMEM((2, n_tiles, 2 * HEAD_DIM, tw), BF16),
                        pltpu.VMEM((n // c + 1, 8, 2 * HEAD_DIM), F32)],
        compiler_params=_vmem(52 * 1024 * 1024),
        name="neighbourhood_attn",
    )(q_t, k, v_t, kc, vc_t, bias)


def _diff_kernel(lq1_ref, lk1_ref, lq2_ref, lk2_ref, g_ref, qt_ref, kk_ref, kkc_ref, vt_ref, vtc_ref, o_ref,
                 qz_ref, kb_ref, kmax_ref, m_ref, l_ref, acc1_ref, acc2_ref, *, lam_init, n_sub):
    tq = qt_ref.shape[-1] // n_sub
    n_lt, dv, tk = vt_ref.shape[2:]
    n_kt = n_lt + 1
    assert kkc_ref.shape[2] == tk and vtc_ref.shape[2] == 1

    def k_tile(j):
        return kk_ref[0, 0, j * tk:(j + 1) * tk, :] if j < n_lt else kkc_ref[0, 0]

    def v_tile(j):
        return vt_ref[0, 0, j] if j < n_lt else vtc_ref[0, 0, 0]

    @pl.when(pl.program_id(2) == 0)
    def _():
        def group_max(kt):
            kf = kt.astype(F32)
            return jnp.max((kf * kf).reshape(tk // 8, 8, dv), axis=0)

        def kbody(j, carry):
            kb_ref[j] = group_max(kk_ref[0, 0, pl.ds(pl.multiple_of(j * tk, tk), tk), :])
            return carry

        lax.fori_loop(0, n_lt, kbody, 0)
        kb_ref[n_lt] = group_max(kkc_ref[0, 0])
        best = kb_ref[...].reshape(n_kt * 8, dv)
        half = lax.broadcasted_iota(jnp.int32, best.shape, 1) < HEAD_DIM
        k1 = jnp.max(jnp.sum(jnp.where(half, best, 0.0), axis=1, keepdims=True), axis=0, keepdims=True)
        k2 = jnp.max(jnp.sum(jnp.where(half, 0.0, best), axis=1, keepdims=True), axis=0, keepdims=True)
        kmax_ref[:, :tq] = jnp.broadcast_to(jnp.sqrt(k1), (1, tq))
        kmax_ref[:, tq:] = jnp.broadcast_to(jnp.sqrt(k2), (1, tq))

    zero = jnp.zeros((HEAD_DIM, tq), BF16)
    state = []
    for s in range(n_sub):
        qt = qt_ref[0, 0, :, s * tq:(s + 1) * tq]
        qz_ref[s, 0:HEAD_DIM, 0:tq] = qt[0:HEAD_DIM]
        qz_ref[s, 0:HEAD_DIM, tq:] = zero
        qz_ref[s, HEAD_DIM:, 0:tq] = zero
        qz_ref[s, HEAD_DIM:, tq:] = qt[HEAD_DIM:]
        qf = qt.astype(F32)
        qsq = qf * qf
        qn = jnp.concatenate([jnp.sum(qsq[:HEAD_DIM], axis=0, keepdims=True),
                              jnp.sum(qsq[HEAD_DIM:], axis=0, keepdims=True)], axis=1)
        state.append(dict(qz=qz_ref[s], shift=jnp.sqrt(qn) * kmax_ref[...] * (1.0 + 2.0 ** -10),
                          l8=jnp.zeros((8, 2 * tq), F32), acc1=jnp.zeros((dv, tq), F32), acc2=jnp.zeros((dv, tq), F32)))

    lam = (jnp.exp(jnp.sum(lq1_ref[...] * lk1_ref[...], axis=-1, keepdims=True))
           - jnp.exp(jnp.sum(lq2_ref[...] * lk2_ref[...], axis=-1, keepdims=True)) + lam_init)

    def finalize(s, l, acc1, acc2):
        o = acc1 / l[:, :tq] - lam * (acc2 / l[:, tq:])
        ms = jnp.mean(o * o, axis=0, keepdims=True)
        o = (o * lax.rsqrt(ms + EPS) * g_ref[...]) * (1.0 - lam_init)
        o_ref[0, s * tq:(s + 1) * tq, :] = o.T.astype(BF16)

    def make_item(s, j):
        st = state[s]

        def finish(scores):
            p = jnp.exp2(scores - st["shift"])
            st["l8"] = st["l8"] + jnp.sum(p.reshape(tk // 8, 8, 2 * tq), axis=0)
            pb = p.astype(BF16)
            vt = v_tile(j)
            st["acc1"] = st["acc1"] + _dot(vt, pb[:, :tq])
            st["acc2"] = st["acc2"] + _dot(vt, pb[:, tq:])
            if j == n_kt - 1:
                l_fast = jnp.sum(st["l8"], axis=0, keepdims=True)
                l_ref[s] = l_fast
                finalize(s, l_fast, st["acc1"], st["acc2"])

        return (lambda: _dot(k_tile(j), st["qz"])), finish

    _pipelined([make_item(s, j) for s in range(n_sub) for j in range(n_kt)], 2)
    l_min = functools.reduce(jnp.minimum, [l_ref[s] for s in range(n_sub)])

    @pl.when(jnp.min(l_min) < 2.0 ** -80)
    def _():
        for s in range(n_sub):
            m_ref[...] = jnp.full(m_ref.shape, NEG, F32)
            l_ref[s] = jnp.zeros((1, 2 * tq), F32)
            acc1_ref[s] = jnp.zeros((dv, tq), F32)
            acc2_ref[s] = jnp.zeros((dv, tq), F32)

            def step(kt, vt):
                st = _dot(kt, qz_ref[s])
                m_old = m_ref[...]
                m_new = jnp.maximum(m_old, jnp.max(st, axis=0, keepdims=True))
                alpha = jnp.exp2(m_old - m_new)
                p = jnp.exp2(st - m_new)
                l_ref[s] = alpha * l_ref[s] + jnp.sum(p, axis=0, keepdims=True)
                m_ref[...] = m_new
                pb = p.astype(BF16)
                acc1_ref[s] = alpha[:, :tq] * acc1_ref[s] + _dot(vt, pb[:, :tq])
                acc2_ref[s] = alpha[:, tq:] * acc2_ref[s] + _dot(vt, pb[:, tq:])

            def body(j, carry):
                step(kk_ref[0, 0, pl.ds(pl.multiple_of(j * tk, tk), tk), :], vt_ref[0, 0, j])
                return carry

            lax.fori_loop(0, n_lt, body, 0)
            step(kkc_ref[0, 0], vtc_ref[0, 0, 0])
            finalize(s, l_ref[s], acc1_ref[s], acc2_ref[s])


def _diff_attn(lam_vecs, sub_g, qt, kk, kkc, vt, vtc, lam_init, tq_block, n_sub):
    b, h, dv, n = qt.shape
    n_ctx = kkc.shape[2]
    n_lt, _, tk = vt.shape[2:]
    tq = tq_block // n_sub
    vec_spec = pl.BlockSpec((1, HEAD_DIM), lambda bi, hi, i: (0, 0))
    return pl.pallas_call(
        functools.partial(_diff_kernel, lam_init=lam_init, n_sub=n_sub),
        grid=(b, h, n // tq_block),
        in_specs=[vec_spec] * 4 + [
            pl.BlockSpec((dv, 1), lambda bi, hi, i: (0, 0)),
            pl.BlockSpec((1, 1, dv, tq_block), lambda bi, hi, i: (bi, hi, 0, i)),
            pl.BlockSpec((1, 1, n, dv), lambda bi, hi, i: (bi, hi, 0, 0)),
            pl.BlockSpec((1, 1, n_ctx, dv), lambda bi, hi, i: (bi, hi, 0, 0)),
            pl.BlockSpec((1, 1, n_lt, dv, tk), lambda bi, hi, i: (bi, hi, 0, 0, 0)),
            pl.BlockSpec((1, 1, 1, dv, tk), lambda bi, hi, i: (bi, hi, 0, 0, 0)),
        ],
        out_specs=pl.BlockSpec((1, tq_block, dv), lambda bi, hi, i: (bi, i, hi)),
        out_shape=jax.ShapeDtypeStruct((b, n, h * dv), BF16),
        scratch_shapes=[
            pltpu.VMEM((n_sub, dv, 2 * tq), BF16),
            pltpu.VMEM((n_lt + 1, 8, dv), F32),
            pltpu.VMEM((1, 2 * tq), F32),
            pltpu.VMEM((1, 2 * tq), F32),
            pltpu.VMEM((n_sub, 1, 2 * tq), F32),
            pltpu.VMEM((n_sub, dv, tq), F32),
            pltpu.VMEM((n_sub, dv, tq), F32),
        ],
        compiler_params=pltpu.CompilerParams(
            dimension_semantics=("arbitrary", "arbitrary", "arbitrary"), vmem_limit_bytes=32 * 1024 * 1024),
        name="diff_attn",
    )(*lam_vecs, sub_g.reshape(dv, 1), qt, kk, kkc, vt, vtc)


FF_CHUNK = 1024


def _residual_mlp(h, ys, mod_ref, g_ref, wo_ref, w1_ref, w2_ref):
    yw = wo_ref.shape[0] // len(ys)
    y = functools.reduce(jnp.add, [_dot(yj, wo_ref[j * yw:(j + 1) * yw, :]) for j, yj in enumerate(ys)])
    hx = h + mod_ref[0, 2:3, :] * y
    a = _modulated(hx, g_ref[...], mod_ref[0, 3:4, :], mod_ref[0, 4:5, :]).astype(BF16)
    acc = jnp.zeros(hx.shape, F32)
    for c0 in range(0, w1_ref.shape[1], FF_CHUNK):
        u = jnp.maximum(_dot(a, w1_ref[:, c0:c0 + FF_CHUNK]), 0.0)
        acc = acc + _dot((u * u).astype(BF16), w2_ref[c0:c0 + FF_CHUNK, :])
    return hx + mod_ref[0, 5:6, :] * acc


def _mlp_kernel(*refs, n_y, final):
    y_refs, refs = refs[:n_y], refs[n_y:]
    if final:
        h_ref, mod_ref, g_ref, wo_ref, w1_ref, w2_ref, fg_ref, o_ref = refs
    else:
        h_ref, mod_ref, g_ref, wo_ref, w1_ref, w2_ref, o_ref = refs
    out = _residual_mlp(h_ref[0], [y_ref[0] for y_ref in y_refs], mod_ref, g_ref, wo_ref, w1_ref, w2_ref)
    if final:
        ms = jnp.mean(out * out, axis=-1, keepdims=True)
        out = out * lax.rsqrt(ms + EPS) * fg_ref[...]
    o_ref[0] = out


def _out_mlp(ys, h, mod9, g2, wo, w1, w2, final_g, tm, name):
    b, n, d = h.shape
    dff = w1.shape[1]
    final = final_g is not None
    tile = pl.BlockSpec((1, tm, d), lambda bi, i: (bi, i, 0))
    in_specs = [pl.BlockSpec((1, tm, y.shape[-1]), lambda bi, i: (bi, i, 0)) for y in ys]
    in_specs += [tile, pl.BlockSpec((1, 6, d), lambda bi, i: (bi, 0, 0)), _const_spec((1, d)),
                 _const_spec((d, d)), _const_spec((d, dff)), _const_spec((dff, d))]
    args = list(ys) + [h, mod9, g2.reshape(1, d), wo, w1, w2]
    if final:
        in_specs.append(_const_spec((1, d)))
        args.append(final_g.reshape(1, d))
    return pl.pallas_call(
        functools.partial(_mlp_kernel, n_y=len(ys), final=final),
        grid=(b, n // tm),
        in_specs=in_specs,
        out_specs=tile,
        out_shape=jax.ShapeDtypeStruct((b, n, d), F32),
        compiler_params=_vmem(56 * 1024 * 1024),
        name=name,
    )(*args)


def _ctx_kernel(sink_ref, ctx_ref, mod0_ref, mod1_ref, g10_ref, g20_ref, g11_ref, wk0_ref, wt0_ref, wo_ref,
                w1_ref, w2_ref, wk1_ref, wv1t_ref, cak_ref, cavt_ref, cbk_ref, cbvt_ref, kkc_ref, vtc_ref):
    x = ctx_ref[0]
    c = x.shape[0]
    a = _modulated(x, g10_ref[...], mod0_ref[0, 0:1, :], mod0_ref[0, 1:2, :]).astype(BF16)
    kw = cak_ref.shape[-1]
    k_nat = _dot(a, wk0_ref[...]).astype(BF16)
    cak_ref[0] = k_nat[:, :kw]
    cbk_ref[0] = k_nat[:, kw:]
    rows = wt0_ref.shape[0] // 2
    qw = rows - kw
    lane_head = lax.broadcasted_iota(jnp.int32, (1, 4 * c), 1) // c
    ys = []
    for mixer, vt_ref in enumerate((cavt_ref, cbvt_ref)):
        xt = _dot_nt(wt0_ref[mixer * rows:(mixer + 1) * rows, :], a)
        vt = xt[qw:].astype(BF16)
        vt_ref[0] = vt
        km = k_nat[:, mixer * kw:(mixer + 1) * kw]
        cols = []
        for h in range(2):
            top = jnp.concatenate([xt[(4 * h + g) * HEAD_DIM:(4 * h + g + 1) * HEAD_DIM] for g in range(4)], axis=1)
            s = _dot(km, _padded_queries(top.astype(BF16), h))
            sink = None
            if mixer == 0:
                sink = jnp.full((1, 4 * c), sink_ref[4 * h + 3], F32)
                for g in range(3):
                    sink = jnp.where(lane_head == g, sink_ref[4 * h + g], sink)
            p, l = _column_softmax([s], sink)
            o = _dot(vt, p)[h * HEAD_DIM:(h + 1) * HEAD_DIM] / l
            cols.append(jnp.concatenate([o[:, g * c:(g + 1) * c] for g in range(4)], axis=0).T)
        ys.append(jnp.concatenate(cols, axis=1).astype(BF16))
    hc = _residual_mlp(x, ys, mod0_ref, g20_ref, wo_ref, w1_ref, w2_ref)
    a1 = _modulated(hc, g11_ref[...], mod1_ref[0, 0:1, :], mod1_ref[0, 1:2, :]).astype(BF16)
    dv = kkc_ref.shape[-1]
    kk = _dot(a1, wk1_ref[...])
    vt1 = _dot_nt(wv1t_ref[...], a1)
    for h in range(kkc_ref.shape[1]):
        kkc_ref[0, h] = kk[:, h * dv:(h + 1) * dv].astype(BF16)
        vtc_ref[0, h, 0] = vt1[h * dv:(h + 1) * dv].astype(BF16)


def _ctx_layer0(sink, ctx, mod0, mod1, g10, g20, g11, wk0, wt0, wo, w1, w2, wk1, wv1t):
    b, c, d = ctx.shape
    kw = wk0.shape[1] // 2
    dv = 2 * HEAD_DIM
    n_heads = wk1.shape[1] // dv
    row = lambda v: v.reshape(1, d)
    k_spec = pl.BlockSpec((1, c, kw), lambda bi: (bi, 0, 0))
    vt_spec = pl.BlockSpec((1, kw, c), lambda bi: (bi, 0, 0))
    k_shape = jax.ShapeDtypeStruct((b, c, kw), BF16)
    vt_shape = jax.ShapeDtypeStruct((b, kw, c), BF16)
    return pl.pallas_call(
        _ctx_kernel,
        grid=(b,),
        in_specs=[
            pl.BlockSpec(memory_space=pltpu.SMEM),
            pl.BlockSpec((1, c, d), lambda bi: (bi, 0, 0)),
            _const_spec((1, 6, d)), _const_spec((1, 6, d)),
            _const_spec((1, d)), _const_spec((1, d)), _const_spec((1, d)),
        ] + [_const_spec(w.shape) for w in (wk0, wt0, wo, w1, w2, wk1, wv1t)],
        out_specs=[k_spec, vt_spec, k_spec, vt_spec,
                   pl.BlockSpec((1, n_heads, c, dv), lambda bi: (bi, 0, 0, 0)),
                   pl.BlockSpec((1, n_heads, 1, dv, c), lambda bi: (bi, 0, 0, 0, 0))],
        out_shape=[k_shape, vt_shape, k_shape, vt_shape,
                   jax.ShapeDtypeStruct((b, n_heads, c, dv), BF16),
                   jax.ShapeDtypeStruct((b, n_heads, 1, dv, c), BF16)],
        compiler_params=_vmem(48 * 1024 * 1024),
        name="ctx_layer0",
    )(sink, ctx, mod0, mod1, row(g10), row(g20), row(g11), wk0, wt0, wo, w1, w2, wk1, wv1t)


def _rope_tables(n):
    t = jnp.arange(n, dtype=jnp.int32)
    row = (t // GRID_W).astype(F32)
    col = (t % GRID_W).astype(F32)
    quarter = HEAD_DIM // 4
    inv_freq = ROPE_THETA ** (-jnp.arange(quarter, dtype=F32) / quarter)
    ar = row[:, None] * inv_freq[None, :]
    ac = col[:, None] * inv_freq[None, :]
    ang = jnp.concatenate([ar, ar, ac, ac], axis=-1)
    cos, sin = jnp.cos(ang), jnp.sin(ang)
    even = (jnp.arange(HEAD_DIM) // quarter) % 2 == 0
    rep = LANES // HEAD_DIM
    sin_a = jnp.tile(jnp.where(even, -sin, 0.0), (1, rep))
    sin_b = jnp.tile(jnp.where(even, 0.0, sin), (1, rep))
    token_major = (jnp.tile(cos, (1, rep)), sin_a, sin_b)
    feature_major = (cos.T, jnp.where(even, -sin, sin).T)
    return token_major, feature_major


def kernel(x, c, ctx, c_ctx, ada_w, ada_b, norm1_g, norm2_g, even_w_in, even_w_out, a_sink, b_rpb,
           odd_w_in, odd_w_out, lam_q1, lam_k1, lam_q2, lam_k2, subln_g, mlp_w1, mlp_w2, final_g):
    b, n, d = x.shape
    n_ctx = ctx.shape[1]
    assert ada_w.shape[0] == 2 and d == 1024 and n % 512 == 0
    scale = HEAD_DIM ** -0.5
    a_qw, a_kvw = 8 * HEAD_DIM, 2 * HEAD_DIM
    b_q0 = a_qw + 2 * a_kvw

    cc = jnp.zeros((16, d), F32).at[:b].set(c).at[b].set(c_ctx)
    mod = _ada_call(cc, ada_w, ada_b)[:, :b + 1].reshape(2, b + 1, 6, d)
    rope, rope_t = _rope_tables(n)

    qscale = scale * LOG2E
    cols0 = jnp.arange(even_w_in.shape[2])
    qcols0 = (cols0 < a_qw) | ((cols0 >= b_q0) & (cols0 < b_q0 + a_qw))
    w_in0 = (even_w_in[0] * jnp.where(qcols0, qscale, 1.0)).astype(BF16)
    cols1 = jnp.arange(odd_w_in.shape[2])
    w_in1 = (odd_w_in[0] * jnp.where(cols1 < 1024, qscale, 1.0)).astype(BF16)
    w_out0, w_out1 = even_w_out[0].astype(BF16), odd_w_out[0].astype(BF16)
    w1, w2 = mlp_w1.astype(BF16), mlp_w2.astype(BF16)

    ak0, av0, bq0, bk0, bv0 = a_qw, a_qw + a_kvw, b_q0, b_q0 + a_qw, b_q0 + a_qw + a_kvw
    wk0 = jnp.concatenate([w_in0[:, ak0:av0], w_in0[:, bk0:bv0]], axis=1)
    wt0 = jnp.concatenate([w_in0[:, :ak0], w_in0[:, av0:bq0], w_in0[:, bq0:bk0], w_in0[:, bv0:]], axis=1).T
    wq1, wk1, wv1 = w_in1[:, :1024], w_in1[:, 1024:2048], w_in1[:, 2048:]
    wt1 = jnp.concatenate([wq1, wv1], axis=1).T
    sink = a_sink[0].astype(F32) * LOG2E

    cak, cav_t, cbk, cbv_t, kkc, vtc = _ctx_layer0(
        sink, ctx, mod[0, b:b + 1], mod[1, b:b + 1], norm1_g[0], norm2_g[0], norm1_g[1],
        wk0, wt0, w_out0, w1[0], w2[0], wk1, wt1[1024:])

    aq_t, ak, av_t, bq_t, bk, bv_t = _project0(x, mod[0], norm1_g[0], wk0, wt0, rope, rope_t, 1024, WINDOW)
    ya = _window_attn(sink, aq_t, ak, av_t, cak, cav_t)
    yb = _neighbourhood_attn(bq_t, bk, bv_t, cbk, cbv_t, _na_bias(b_rpb[0].astype(F32)))
    hx = _out_mlp([ya, yb], x, mod[0], norm2_g[0], w_out0, w1[0], w2[0], None, 1024, "mlp0_x")

    lam_init = 0.8 - 0.6 * math.exp(-0.3 * 1)
    qt, kk, vt = _project1(hx, mod[1], norm1_g[1], wk1, wt1, rope, rope_t, 1024, n_ctx)
    lam_vecs = [v[0].reshape(1, HEAD_DIM).astype(F32) for v in (lam_q1, lam_k1, lam_q2, lam_k2)]
    yx = _diff_attn(lam_vecs, subln_g[0].astype(F32), qt, kk, kkc, vt, vtc, lam_init, 2048, 8)
    return _out_mlp([yx], hx, mod[1], norm2_g[1], w_out1, w1[1], w2[1], final_g, 1024, "mlp1_x")
```

```python
import functools
import math

import jax
import jax.numpy as jnp
from jax import lax
from jax.experimental import pallas as pl
from jax.experimental.pallas import tpu as pltpu

F32 = jnp.float32
BF16 = jnp.bfloat16

GRID_W = 64
HEAD_DIM = 64
WINDOW = 128
NA_ROWS = 8
NA_COLS = 16
ROPE_THETA = 10000.0
EPS = 1e-6
NEG = -1e30
LOG2E = math.log2(math.e)

MIB = 1024 * 1024
V7X_VMEM_BYTES = 64 * MIB
VMEM_RESERVE_BYTES = 8 * MIB
LANES = 128


def _vmem(nbytes, **kwargs):
    limit = int(min(nbytes, V7X_VMEM_BYTES - VMEM_RESERVE_BYTES))
    return pltpu.CompilerParams(vmem_limit_bytes=limit, **kwargs)


def _const_spec(shape):
    n = len(shape)
    return pl.BlockSpec(shape, lambda *_: (0,) * n, pipeline_mode=pl.Buffered(1))


def _dot(a, b):
    return jnp.dot(a, b, preferred_element_type=F32)


def _dot_nt(a, b):
    return lax.dot_general(a, b, (((1,), (1,)), ((), ())), preferred_element_type=F32)


def _ada_kernel(c_ref, w_ref, b_ref, o_ref):
    cc = c_ref[...]
    s = cc * jax.nn.sigmoid(cc)
    o_ref[0] = jnp.dot(s, w_ref[0], preferred_element_type=F32, precision=lax.Precision.HIGHEST) + b_ref[0]


def _ada_call(cc, ada_w, ada_b):
    depth, d, six_d = ada_w.shape
    tn = 1536
    return pl.pallas_call(
        _ada_kernel,
        grid=(depth, six_d // tn),
        in_specs=[
            pl.BlockSpec((16, d), lambda l, j: (0, 0)),
            pl.BlockSpec((1, d, tn), lambda l, j: (l, 0, j)),
            pl.BlockSpec((1, 1, tn), lambda l, j: (l, 0, j)),
        ],
        out_specs=pl.BlockSpec((1, 16, tn), lambda l, j: (l, 0, j)),
        out_shape=jax.ShapeDtypeStruct((depth, 16, six_d), F32),
        compiler_params=_vmem(32 * MIB),
        name="ada_mod",
    )(cc, ada_w, ada_b.reshape(depth, 1, six_d))


def _modulated(x, g, shift, scale):
    ms = jnp.mean(x * x, axis=-1, keepdims=True)
    return (x * lax.rsqrt(ms + EPS) * g) * (1.0 + scale) + shift


def _rope_lanes(x, cos, sin_a, sin_b):
    return x * cos + pltpu.roll(x, LANES - 16, 1) * sin_a + pltpu.roll(x, 16, 1) * sin_b


def _rope_rows(x, cos_t, sin_t):
    q = HEAD_DIM // 4
    rot = jnp.concatenate([x[q:2 * q], x[0:q], x[3 * q:4 * q], x[2 * q:3 * q]], axis=0)
    return x * cos_t + rot * sin_t


def _proj1_kernel(h_ref, mod_ref, g_ref, wk_ref, wt_ref, cos_ref, sa_ref, sb_ref, cost_ref, sint_ref,
                  qt_ref, kk_ref, vt_ref, *, tk):
    a = _modulated(h_ref[0], g_ref[...], mod_ref[0, 0:1, :], mod_ref[0, 1:2, :]).astype(BF16)
    tm = a.shape[0]
    dv = 2 * HEAD_DIM
    n_heads = kk_ref.shape[1]
    chunk = 4 * dv
    for c0 in range(0, n_heads * dv, chunk):
        p = _dot(a, wk_ref[:, c0:c0 + chunk])
        for j in range(chunk // dv):
            xc = _rope_lanes(p[:, j * dv:(j + 1) * dv], cos_ref[...], sa_ref[...], sb_ref[...])
            kk_ref[0, c0 // dv + j] = xc.astype(BF16)
    n_q = n_heads * dv
    for f0 in range(0, wt_ref.shape[0], chunk):
        xt = _dot_nt(wt_ref[f0:f0 + chunk, :], a)
        for j in range(chunk // dv):
            row = f0 + j * dv
            hd = xt[j * dv:(j + 1) * dv]
            if row < n_q:
                hd = jnp.concatenate([_rope_rows(hd[m * HEAD_DIM:(m + 1) * HEAD_DIM], cost_ref[...], sint_ref[...])
                                      for m in range(2)], axis=0)
                qt_ref[0, row // dv] = hd.astype(BF16)
            else:
                for t in range(tm // tk):
                    vt_ref[0, (row - n_q) // dv, t] = hd[:, t * tk:(t + 1) * tk].astype(BF16)


def _project1(h, mod9, g, wk, wt, rope, rope_t, tm, tk):
    b, n, d = h.shape
    dv = 2 * HEAD_DIM
    n_heads = wk.shape[1] // dv
    return pl.pallas_call(
        functools.partial(_proj1_kernel, tk=tk),
        grid=(b, n // tm),
        in_specs=[
            pl.BlockSpec((1, tm, d), lambda bi, i: (bi, i, 0)),
            pl.BlockSpec((1, 6, d), lambda bi, i: (bi, 0, 0)),
            _const_spec((1, d)),
            _const_spec(wk.shape),
            _const_spec(wt.shape),
        ] + [pl.BlockSpec((tm, LANES), lambda bi, i: (i, 0))] * 3
          + [pl.BlockSpec((HEAD_DIM, tm), lambda bi, i: (0, i))] * 2,
        out_specs=[
            pl.BlockSpec((1, n_heads, dv, tm), lambda bi, i: (bi, 0, 0, i)),
            pl.BlockSpec((1, n_heads, tm, dv), lambda bi, i: (bi, 0, i, 0)),
            pl.BlockSpec((1, n_heads, tm // tk, dv, tk), lambda bi, i: (bi, 0, i, 0, 0)),
        ],
        out_shape=[
            jax.ShapeDtypeStruct((b, n_heads, dv, n), BF16),
            jax.ShapeDtypeStruct((b, n_heads, n, dv), BF16),
            jax.ShapeDtypeStruct((b, n_heads, n // tk, dv, tk), BF16),
        ],
        compiler_params=_vmem(48 * MIB),
        name="proj1_x",
    )(h, mod9, g.reshape(1, d), wk, wt, *rope, *rope_t)


def _proj0_kernel(h_ref, mod_ref, g_ref, wk_ref, wt_ref, cos_ref, sa_ref, sb_ref, cost_ref, sint_ref,
                  qa_ref, ka_ref, va_ref, qb_ref, kb_ref, vb_ref, *, tw):
    a = _modulated(h_ref[0], g_ref[...], mod_ref[0, 0:1, :], mod_ref[0, 1:2, :]).astype(BF16)
    tm = a.shape[0]
    kw = ka_ref.shape[-1]
    p = _dot(a, wk_ref[...])
    ka_ref[0] = _rope_lanes(p[:, :kw], cos_ref[...], sa_ref[...], sb_ref[...]).astype(BF16)
    kb_ref[0] = p[:, kw:].astype(BF16)
    qw = qa_ref.shape[2]
    rows = qw + va_ref.shape[2]
    for mixer, (q_ref, v_ref) in enumerate(((qa_ref, va_ref), (qb_ref, vb_ref))):
        xt = _dot_nt(wt_ref[mixer * rows:(mixer + 1) * rows, :], a)
        q = xt[:qw]
        if mixer == 0:
            q = jnp.concatenate([_rope_rows(q[h * HEAD_DIM:(h + 1) * HEAD_DIM], cost_ref[...], sint_ref[...])
                                 for h in range(qw // HEAD_DIM)], axis=0)
        for t in range(tm // tw):
            q_ref[0, t] = q[:, t * tw:(t + 1) * tw].astype(BF16)
            v_ref[0, t] = xt[qw:, t * tw:(t + 1) * tw].astype(BF16)


def _project0(h, mod9, g, wk, wt, rope, rope_t, tm, tw):
    b, n, d = h.shape
    kw = wk.shape[1] // 2
    rows = wt.shape[0] // 2
    qw = rows - kw
    q_spec = pl.BlockSpec((1, tm // tw, qw, tw), lambda bi, i: (bi, i, 0, 0))
    k_spec = pl.BlockSpec((1, tm, kw), lambda bi, i: (bi, i, 0))
    v_spec = pl.BlockSpec((1, tm // tw, kw, tw), lambda bi, i: (bi, i, 0, 0))
    q_shape = jax.ShapeDtypeStruct((b, n // tw, qw, tw), BF16)
    k_shape = jax.ShapeDtypeStruct((b, n, kw), BF16)
    v_shape = jax.ShapeDtypeStruct((b, n // tw, kw, tw), BF16)
    return pl.pallas_call(
        functools.partial(_proj0_kernel, tw=tw),
        grid=(b, n // tm),
        in_specs=[
            pl.BlockSpec((1, tm, d), lambda bi, i: (bi, i, 0)),
            pl.BlockSpec((1, 6, d), lambda bi, i: (bi, 0, 0)),
            _const_spec((1, d)),
            _const_spec(wk.shape),
            _const_spec(wt.shape),
        ] + [pl.BlockSpec((tm, LANES), lambda bi, i: (i, 0))] * 3
          + [pl.BlockSpec((HEAD_DIM, tm), lambda bi, i: (0, i))] * 2,
        out_specs=[q_spec, k_spec, v_spec, q_spec, k_spec, v_spec],
        out_shape=[q_shape, k_shape, v_shape, q_shape, k_shape, v_shape],
        compiler_params=_vmem(48 * MIB),
        name="proj0_x",
    )(h, mod9, g.reshape(1, d), wk, wt, *rope, *rope_t)


def _pipelined(items, ahead):
    scores = [items[k][0]() for k in range(min(ahead, len(items)))]
    for k in range(len(items)):
        if k + ahead < len(items):
            scores.append(items[k + ahead][0]())
        items[k][1](scores[k])
        scores[k] = None


SHIFT_SLACK = 1.0 + 2.0 ** -10
L_FLOOR = 2.0 ** -80


def _key_norm_bounds(k_ref, kc_ref, kb_ref):
    tile = kc_ref.shape[1]
    nt = k_ref.shape[1] // tile
    width = k_ref.shape[2]

    def group_max(kt):
        kf = kt.astype(F32)
        return jnp.max((kf * kf).reshape(tile // 8, 8, width), axis=0)

    def body(j, carry):
        kb_ref[j] = group_max(k_ref[0, pl.ds(pl.multiple_of(j * tile, tile), tile), :])
        return carry

    lax.fori_loop(0, nt, body, 0)
    kb_ref[nt] = group_max(kc_ref[0])
    best = kb_ref[...].reshape((nt + 1) * 8, width)
    half = lax.broadcasted_iota(jnp.int32, best.shape, 1) < width // 2
    k0 = jnp.max(jnp.sum(jnp.where(half, best, 0.0), axis=1, keepdims=True), axis=0, keepdims=True)
    k1 = jnp.max(jnp.sum(jnp.where(half, 0.0, best), axis=1, keepdims=True), axis=0, keepdims=True)
    return jnp.sqrt(k0), jnp.sqrt(k1)


def _column_softmax(pieces, extra_row=None, shift=None):
    if shift is None:
        m = functools.reduce(jnp.maximum, [jnp.max(s, axis=0, keepdims=True) for s in pieces])
        if extra_row is not None:
            m = jnp.maximum(m, extra_row)
    else:
        m = shift
    ps = [jnp.exp2(s - m) for s in pieces]
    l = functools.reduce(jnp.add, [jnp.sum(p, axis=0, keepdims=True) for p in ps])
    if extra_row is not None:
        l = l + jnp.exp2(extra_row - m)
    return jnp.concatenate([p.astype(BF16) for p in ps], axis=0), l


def _padded_queries(top, h):
    z = jnp.zeros_like(top)
    return jnp.concatenate([top, z] if h == 0 else [z, top], axis=0)


def _win_kernel(sink_ref, q_ref, k_ref, v_ref, kc_ref, vc_ref, o_ref, kb_ref, *, blk, per_iter):
    n = k_ref.shape[1]
    nb = n // blk
    c_idx = lax.broadcasted_iota(jnp.int32, (blk, blk), 0)
    r_idx = lax.broadcasted_iota(jnp.int32, (blk, blk), 1)
    lower = jnp.tile(jnp.where(c_idx >= r_idx, 0.0, NEG), (1, 4))
    upper = jnp.tile(jnp.where(c_idx <= r_idx, 0.0, NEG), (1, 4))
    lane_head = lax.broadcasted_iota(jnp.int32, (1, 4 * blk), 1) // blk

    def sink_row(h):
        row = jnp.full((1, 4 * blk), sink_ref[4 * h + 3], F32)
        for g in range(3):
            row = jnp.where(lane_head == g, sink_ref[4 * h + g], row)
        return row

    sinks = [sink_row(0), sink_row(1)]
    kmax = _key_norm_bounds(k_ref, kc_ref, kb_ref)

    def run(exact):
        l_min = [jnp.full((1, 4 * blk), jnp.inf, F32)]

        def make_item(i, h, k_rows, v_tiles, biases):
            cache = {}

            def score():
                top = jnp.concatenate([q_ref[0, i, (4 * h + g) * HEAD_DIM:(4 * h + g + 1) * HEAD_DIM, :]
                                       for g in range(4)], axis=1)
                if not exact:
                    tf = top.astype(F32)
                    qn = jnp.sqrt(jnp.sum(tf * tf, axis=0, keepdims=True))
                    cache["shift"] = jnp.maximum(qn * kmax[h] * SHIFT_SLACK, sinks[h])
                return _dot(jnp.concatenate([k_rows(), kc_ref[0]], axis=0), _padded_queries(top, h))

            def finish(s):
                pieces = [s[t * blk:(t + 1) * blk] if bias is None else s[t * blk:(t + 1) * blk] + bias
                          for t, bias in enumerate(biases)]
                pieces.append(s[len(biases) * blk:])
                p, l = _column_softmax(pieces, sinks[h], cache.get("shift"))
                if not exact:
                    l_min[0] = jnp.minimum(l_min[0], l)
                vall = jnp.concatenate(v_tiles() + [vc_ref[0]], axis=1)
                o = _dot(vall, p)[h * HEAD_DIM:(h + 1) * HEAD_DIM] / l
                o4 = jnp.concatenate([o[:, g * blk:(g + 1) * blk] for g in range(4)], axis=0)
                q0 = i * blk if isinstance(i, int) else pl.multiple_of(i * blk, blk)
                o_ref[0, pl.ds(q0, blk), 4 * h * HEAD_DIM:4 * (h + 1) * HEAD_DIM] = o4.T.astype(BF16)

            return score, finish

        def edge_items(i, t0, biases):
            return [make_item(i, h, lambda: k_ref[0, t0 * blk:(t0 + 2) * blk, :],
                              lambda: [v_ref[0, t0], v_ref[0, t0 + 1]], biases) for h in range(2)]

        def inner_items(i):
            k0 = pl.multiple_of((i - 1) * blk, blk)
            return [make_item(i, h, lambda: k_ref[0, pl.ds(k0, 3 * blk), :],
                              lambda: [v_ref[0, i - 1], v_ref[0, i], v_ref[0, i + 1]], [lower, None, upper])
                    for h in range(2)]

        _pipelined(edge_items(0, 0, [None, upper]), 1)

        def body(j, carry):
            l_min[0] = carry
            _pipelined([it for u in range(per_iter) for it in inner_items(1 + per_iter * j + u)], 2)
            return l_min[0]

        l_min[0] = lax.fori_loop(0, (nb - 2) // per_iter, body, l_min[0])
        _pipelined(edge_items(nb - 1, nb - 2, [lower, None]), 1)
        return l_min[0]

    l_fast = run(exact=False)

    @pl.when(jnp.min(l_fast) < L_FLOOR)
    def _():
        run(exact=True)


def _window_attn(sink, q_t, k, v_t, kc, vc_t):
    b, nb, hd, blk = q_t.shape
    n = k.shape[1]
    c = kc.shape[1]
    per_iter = 5
    assert blk == WINDOW and (nb - 2) % per_iter == 0
    return pl.pallas_call(
        functools.partial(_win_kernel, blk=blk, per_iter=per_iter),
        grid=(b,),
        in_specs=[
            pl.BlockSpec(memory_space=pltpu.SMEM),
            pl.BlockSpec((1, nb, hd, blk), lambda bi: (bi, 0, 0, 0)),
            pl.BlockSpec((1, n, 2 * HEAD_DIM), lambda bi: (bi, 0, 0)),
            pl.BlockSpec((1, nb, 2 * HEAD_DIM, blk), lambda bi: (bi, 0, 0, 0)),
            pl.BlockSpec((1, c, 2 * HEAD_DIM), lambda bi: (bi, 0, 0)),
            pl.BlockSpec((1, 2 * HEAD_DIM, c), lambda bi: (bi, 0, 0)),
        ],
        out_specs=pl.BlockSpec((1, n, hd), lambda bi: (bi, 0, 0)),
        out_shape=jax.ShapeDtypeStruct((b, n, hd), BF16),
        scratch_shapes=[pltpu.VMEM((n // c + 1, 8, 2 * HEAD_DIM), F32)],
        compiler_params=_vmem(48 * MIB),
        name="window_attn",
    )(sink, q_t, k, v_t, kc, vc_t)


def _na_bias_kernel(rpb_ref, o_ref):
    h = pl.program_id(0)
    n_roff = 2 * NA_ROWS - 1
    n_coff = 2 * NA_COLS - 1
    shape = (GRID_W, 4 * GRID_W)
    ck = lax.broadcasted_iota(jnp.int32, shape, 0)
    lane = lax.broadcasted_iota(jnp.int32, shape, 1)
    cq = lane % GRID_W
    head = lane // GRID_W
    cstart = jnp.clip(cq - NA_COLS // 2, 0, GRID_W - NA_COLS)
    valid = (ck >= cstart) & (ck < cstart + NA_COLS)
    diff = ck - cq + (NA_COLS - 1)
    hits = [[valid & (diff == o) & (head == g) for o in range(n_coff)] for g in range(4)]
    tiles = []
    for roff in range(n_roff):
        t = jnp.full(shape, NEG, F32)
        for g in range(4):
            for o in range(n_coff):
                t = jnp.where(hits[g][o], rpb_ref[((4 * h + g) * n_roff + roff) * n_coff + o] * LOG2E, t)
        tiles.append(t)
    for ro in range(n_roff - 1):
        o_ref[0, ro] = jnp.concatenate([tiles[ro], tiles[ro + 1]], axis=0)


def _na_bias(rpb):
    hq, n_roff, n_coff = rpb.shape
    return pl.pallas_call(
        _na_bias_kernel,
        grid=(hq // 4,),
        in_specs=[pl.BlockSpec(memory_space=pltpu.SMEM)],
        out_specs=pl.BlockSpec((1, n_roff - 1, 2 * GRID_W, 4 * GRID_W), lambda h: (h, 0, 0, 0)),
        out_shape=jax.ShapeDtypeStruct((hq // 4, n_roff - 1, 2 * GRID_W, 4 * GRID_W), F32),
        name="na_bias",
    )(rpb.reshape(-1))


def _na_kernel(q_ref, k_ref, v_ref, kc_ref, vc_ref, bias_ref, o_ref, vall_ref, kb_ref):
    n = k_ref.shape[1]
    rows = n // GRID_W
    n_tiles = rows // 2
    kr = min(NA_ROWS, rows)
    nkey = kr * GRID_W
    lo = lax.broadcasted_iota(jnp.int32, (2 * HEAD_DIM, 2 * GRID_W), 1) < GRID_W
    lo_q = lax.broadcasted_iota(jnp.int32, (HEAD_DIM, 2 * GRID_W), 1) < GRID_W

    def swap(x):
        return pltpu.roll(x, GRID_W, 1)

    def build(t, carry):
        a = v_ref[0, t]
        vall_ref[0, t] = a
        nxt = v_ref[0, jnp.minimum(t + 1, n_tiles - 1)]
        vall_ref[1, t] = jnp.where(lo, swap(a.astype(F32)), swap(nxt.astype(F32))).astype(BF16)
        return carry

    lax.fori_loop(0, n_tiles, build, 0)

    def row_start(r):
        return jnp.clip(r - kr // 2, 0, rows - kr)

    kmax = _key_norm_bounds(k_ref, kc_ref, kb_ref)
    bias_max = [jnp.maximum(jnp.max(jnp.max(bias_ref[h], axis=0), axis=0, keepdims=True), 0.0) for h in range(2)]
    tiles_per_iter = 4

    def run(exact):
        l_min = [jnp.full((1, 4 * GRID_W), jnp.inf, F32)]

        def make_item(t, par, h, done):
            r = 2 * t + par
            cache = {}

            def score():
                cols = []
                for c in range(2):
                    g0 = 4 * h + 2 * c
                    a0 = q_ref[0, t, g0 * HEAD_DIM:(g0 + 1) * HEAD_DIM, :].astype(F32)
                    a1 = q_ref[0, t, (g0 + 1) * HEAD_DIM:(g0 + 2) * HEAD_DIM, :].astype(F32)
                    cols.append(jnp.where(lo_q, a0, swap(a1)) if par == 0 else jnp.where(lo_q, swap(a0), a1))
                top = jnp.concatenate(cols, axis=1)
                if not exact:
                    qn = jnp.sqrt(jnp.sum(top * top, axis=0, keepdims=True))
                    cache["shift"] = qn * kmax[h] * SHIFT_SLACK + bias_max[h]
                k0 = pl.multiple_of(row_start(r) * GRID_W, GRID_W)
                return _dot(jnp.concatenate([k_ref[0, pl.ds(k0, nkey), :], kc_ref[0]], axis=0),
                            _padded_queries(top.astype(BF16), h))

            def finish(s):
                rs = row_start(r)
                ro0 = rs - r + (NA_ROWS - 1)
                bias = jnp.concatenate([bias_ref[h, ro0 + 2 * i] for i in range(kr // 2)], axis=0)
                p, l = _column_softmax([s[:nkey] + bias, s[nkey:]], None, cache.get("shift"))
                if not exact:
                    l_min[0] = jnp.minimum(l_min[0], l)
                vwin = [vall_ref[rs & 1, (rs >> 1) + i] for i in range(nkey // (2 * GRID_W))]
                o = _dot(jnp.concatenate(vwin + [vc_ref[0]], axis=1), p)[h * HEAD_DIM:(h + 1) * HEAD_DIM] / l
                done.append(o)
                if par == 1:
                    o_even, o_odd = done
                    ws = []
                    for c in range(2):
                        ce = o_even[:, c * 2 * GRID_W:(c + 1) * 2 * GRID_W]
                        co = o_odd[:, c * 2 * GRID_W:(c + 1) * 2 * GRID_W]
                        ws.append(jnp.where(lo_q, ce, swap(co)))
                        ws.append(jnp.where(lo_q, swap(ce), co))
                    w = jnp.concatenate(ws, axis=0)
                    q0 = pl.multiple_of(t * 2 * GRID_W, 2 * GRID_W)
                    o_ref[0, pl.ds(q0, 2 * GRID_W), 4 * h * HEAD_DIM:4 * (h + 1) * HEAD_DIM] = w.T.astype(BF16)

            return score, finish

        def body(j, carry):
            l_min[0] = carry
            items = []
            for u in range(tiles_per_iter):
                for h in range(2):
                    done = []
                    items += [make_item(tiles_per_iter * j + u, par, h, done) for par in range(2)]
            _pipelined(items, 2)
            return l_min[0]

        return lax.fori_loop(0, n_tiles // tiles_per_iter, body, l_min[0])

    l_fast = run(exact=False)

    @pl.when(jnp.min(l_fast) < L_FLOOR)
    def _():
        run(exact=True)


def _neighbourhood_attn(q_t, k, v_t, kc, vc_t, bias):
    b, n_tiles, hd, tw = q_t.shape
    n = k.shape[1]
    c = kc.shape[1]
    assert tw == 2 * GRID_W and n_tiles * tw == n
    return pl.pallas_call(
        _na_kernel,
        grid=(b,),
        in_specs=[
            pl.BlockSpec((1, n_tiles, hd, tw), lambda bi: (bi, 0, 0, 0)),
            pl.BlockSpec((1, n, 2 * HEAD_DIM), lambda bi: (bi, 0, 0)),
            pl.BlockSpec((1, n_tiles, 2 * HEAD_DIM, tw), lambda bi: (bi, 0, 0, 0)),
            pl.BlockSpec((1, c, 2 * HEAD_DIM), lambda bi: (bi, 0, 0)),
            pl.BlockSpec((1, 2 * HEAD_DIM, c), lambda bi: (bi, 0, 0)),
            _const_spec(bias.shape),
        ],
        out_specs=pl.BlockSpec((1, n, hd), lambda bi: (bi, 0, 0)),
        out_shape=jax.ShapeDtypeStruct((b, n, hd), BF16),
        scratch_shapes=[pltpu.VMEM((2, n_tiles, 2 * HEAD_DIM, tw), BF16),
                        pltpu.VMEM((n // c + 1, 8, 2 * HEAD_DIM), F32)],
        compiler_params=_vmem(52 * MIB),
        name="neighbourhood_attn",
    )(q_t, k, v_t, kc, vc_t, bias)


def _diff_kernel(lq1_ref, lk1_ref, lq2_ref, lk2_ref, g_ref, qt_ref, kk_ref, kkc_ref, vt_ref, vtc_ref, o_ref,
                 qz_ref, kb_ref, kmax_ref, m_ref, l_ref, acc1_ref, acc2_ref, *, lam_init, n_sub):
    tq = qt_ref.shape[-1] // n_sub
    n_lt, dv, tk = vt_ref.shape[2:]
    n_kt = n_lt + 1
    assert kkc_ref.shape[2] == tk and vtc_ref.shape[2] == 1

    def k_tile(j):
        return kk_ref[0, 0, j * tk:(j + 1) * tk, :] if j < n_lt else kkc_ref[0, 0]

    def v_tile(j):
        return vt_ref[0, 0, j] if j < n_lt else vtc_ref[0, 0, 0]

    @pl.when(pl.program_id(2) == 0)
    def _():
        def group_max(kt):
            kf = kt.astype(F32)
            return jnp.max((kf * kf).reshape(tk // 8, 8, dv), axis=0)

        def kbody(j, carry):
            kb_ref[j] = group_max(kk_ref[0, 0, pl.ds(pl.multiple_of(j * tk, tk), tk), :])
            return carry

        lax.fori_loop(0, n_lt, kbody, 0)
        kb_ref[n_lt] = group_max(kkc_ref[0, 0])
        best = kb_ref[...].reshape(n_kt * 8, dv)
        half = lax.broadcasted_iota(jnp.int32, best.shape, 1) < HEAD_DIM
        k1 = jnp.max(jnp.sum(jnp.where(half, best, 0.0), axis=1, keepdims=True), axis=0, keepdims=True)
        k2 = jnp.max(jnp.sum(jnp.where(half, 0.0, best), axis=1, keepdims=True), axis=0, keepdims=True)
        kmax_ref[:, :tq] = jnp.broadcast_to(jnp.sqrt(k1), (1, tq))
        kmax_ref[:, tq:] = jnp.broadcast_to(jnp.sqrt(k2), (1, tq))

    zero = jnp.zeros((HEAD_DIM, tq), BF16)
    state = []
    for s in range(n_sub):
        qt = qt_ref[0, 0, :, s * tq:(s + 1) * tq]
        qz_ref[s, 0:HEAD_DIM, 0:tq] = qt[0:HEAD_DIM]
        qz_ref[s, 0:HEAD_DIM, tq:] = zero
        qz_ref[s, HEAD_DIM:, 0:tq] = zero
        qz_ref[s, HEAD_DIM:, tq:] = qt[HEAD_DIM:]
        qf = qt.astype(F32)
        qsq = qf * qf
        qn = jnp.concatenate([jnp.sum(qsq[:HEAD_DIM], axis=0, keepdims=True),
                              jnp.sum(qsq[HEAD_DIM:], axis=0, keepdims=True)], axis=1)
        state.append(dict(qz=qz_ref[s], shift=jnp.sqrt(qn) * kmax_ref[...] * SHIFT_SLACK,
                          l8=jnp.zeros((8, 2 * tq), F32), acc1=jnp.zeros((dv, tq), F32), acc2=jnp.zeros((dv, tq), F32)))

    lam = (jnp.exp(jnp.sum(lq1_ref[...] * lk1_ref[...], axis=-1, keepdims=True))
           - jnp.exp(jnp.sum(lq2_ref[...] * lk2_ref[...], axis=-1, keepdims=True)) + lam_init)

    def finalize(s, l, acc1, acc2):
        o = acc1 / l[:, :tq] - lam * (acc2 / l[:, tq:])
        ms = jnp.mean(o * o, axis=0, keepdims=True)
        o = (o * lax.rsqrt(ms + EPS) * g_ref[...]) * (1.0 - lam_init)
        o_ref[0, 0, s * tq:(s + 1) * tq, :] = o.T.astype(BF16)

    def make_item(s, j):
        st = state[s]

        def finish(scores):
            p = jnp.exp2(scores - st["shift"])
            st["l8"] = st["l8"] + jnp.sum(p.reshape(tk // 8, 8, 2 * tq), axis=0)
            pb = p.astype(BF16)
            vt = v_tile(j)
            st["acc1"] = st["acc1"] + _dot(vt, pb[:, :tq])
            st["acc2"] = st["acc2"] + _dot(vt, pb[:, tq:])
            if j == n_kt - 1:
                l_fast = jnp.sum(st["l8"], axis=0, keepdims=True)
                l_ref[s] = l_fast
                finalize(s, l_fast, st["acc1"], st["acc2"])

        return (lambda: _dot(k_tile(j), st["qz"])), finish

    _pipelined([make_item(s, j) for s in range(n_sub) for j in range(n_kt)], 2)
    l_min = functools.reduce(jnp.minimum, [l_ref[s] for s in range(n_sub)])

    @pl.when(jnp.min(l_min) < L_FLOOR)
    def _():
        for s in range(n_sub):
            m_ref[...] = jnp.full(m_ref.shape, NEG, F32)
            l_ref[s] = jnp.zeros((1, 2 * tq), F32)
            acc1_ref[s] = jnp.zeros((dv, tq), F32)
            acc2_ref[s] = jnp.zeros((dv, tq), F32)

            def step(kt, vt):
                st = _dot(kt, qz_ref[s])
                m_old = m_ref[...]
                m_new = jnp.maximum(m_old, jnp.max(st, axis=0, keepdims=True))
                alpha = jnp.exp2(m_old - m_new)
                p = jnp.exp2(st - m_new)
                l_ref[s] = alpha * l_ref[s] + jnp.sum(p, axis=0, keepdims=True)
                m_ref[...] = m_new
                pb = p.astype(BF16)
                acc1_ref[s] = alpha[:, :tq] * acc1_ref[s] + _dot(vt, pb[:, :tq])
                acc2_ref[s] = alpha[:, tq:] * acc2_ref[s] + _dot(vt, pb[:, tq:])

            def body(j, carry):
                step(kk_ref[0, 0, pl.ds(pl.multiple_of(j * tk, tk), tk), :], vt_ref[0, 0, j])
                return carry

            lax.fori_loop(0, n_lt, body, 0)
            step(kkc_ref[0, 0], vtc_ref[0, 0, 0])
            finalize(s, l_ref[s], acc1_ref[s], acc2_ref[s])


def _diff_attn(lam_vecs, sub_g, qt, kk, kkc, vt, vtc, lam_init, tq_block, n_sub):
    b, h, dv, n = qt.shape
    n_ctx = kkc.shape[2]
    n_lt, _, tk = vt.shape[2:]
    tq = tq_block // n_sub
    vec_spec = pl.BlockSpec((1, HEAD_DIM), lambda bi, hi, i: (0, 0))
    return pl.pallas_call(
        functools.partial(_diff_kernel, lam_init=lam_init, n_sub=n_sub),
        grid=(b, h, n // tq_block),
        in_specs=[vec_spec] * 4 + [
            pl.BlockSpec((dv, 1), lambda bi, hi, i: (0, 0)),
            pl.BlockSpec((1, 1, dv, tq_block), lambda bi, hi, i: (bi, hi, 0, i)),
            pl.BlockSpec((1, 1, n, dv), lambda bi, hi, i: (bi, hi, 0, 0)),
            pl.BlockSpec((1, 1, n_ctx, dv), lambda bi, hi, i: (bi, hi, 0, 0)),
            pl.BlockSpec((1, 1, n_lt, dv, tk), lambda bi, hi, i: (bi, hi, 0, 0, 0)),
            pl.BlockSpec((1, 1, 1, dv, tk), lambda bi, hi, i: (bi, hi, 0, 0, 0)),
        ],
        out_specs=pl.BlockSpec((1, 1, tq_block, dv), lambda bi, hi, i: (bi, hi, i, 0)),
        out_shape=jax.ShapeDtypeStruct((b, h, n, dv), BF16),
        scratch_shapes=[
            pltpu.VMEM((n_sub, dv, 2 * tq), BF16),
            pltpu.VMEM((n_lt + 1, 8, dv), F32),
            pltpu.VMEM((1, 2 * tq), F32),
            pltpu.VMEM((1, 2 * tq), F32),
            pltpu.VMEM((n_sub, 1, 2 * tq), F32),
            pltpu.VMEM((n_sub, dv, tq), F32),
            pltpu.VMEM((n_sub, dv, tq), F32),
        ],
        compiler_params=_vmem(32 * MIB, dimension_semantics=("arbitrary", "arbitrary", "arbitrary")),
        name="diff_attn",
    )(*lam_vecs, sub_g.reshape(dv, 1), qt, kk, kkc, vt, vtc)


FF_CHUNK = 1024


def _residual_mlp(h, ys, mod_ref, g_ref, wo_ref, w1_ref, w2_ref):
    yw = wo_ref.shape[0] // len(ys)
    y = functools.reduce(jnp.add, [_dot(yj, wo_ref[j * yw:(j + 1) * yw, :]) for j, yj in enumerate(ys)])
    hx = h + mod_ref[0, 2:3, :] * y
    a = _modulated(hx, g_ref[...], mod_ref[0, 3:4, :], mod_ref[0, 4:5, :]).astype(BF16)
    acc = jnp.zeros(hx.shape, F32)
    for c0 in range(0, w1_ref.shape[1], FF_CHUNK):
        u = jnp.maximum(_dot(a, w1_ref[:, c0:c0 + FF_CHUNK]), 0.0)
        acc = acc + _dot((u * u).astype(BF16), w2_ref[c0:c0 + FF_CHUNK, :])
    return hx + mod_ref[0, 5:6, :] * acc


def _mlp_kernel(*refs, n_y, final):
    y_refs, refs = refs[:n_y], refs[n_y:]
    if final:
        h_ref, mod_ref, g_ref, wo_ref, w1_ref, w2_ref, fg_ref, o_ref = refs
    else:
        h_ref, mod_ref, g_ref, wo_ref, w1_ref, w2_ref, o_ref = refs
    ys = [y_ref[0] if len(y_ref.shape) == 3 else jnp.concatenate([y_ref[0, h] for h in range(y_ref.shape[1])], axis=1)
          for y_ref in y_refs]
    out = _residual_mlp(h_ref[0], ys, mod_ref, g_ref, wo_ref, w1_ref, w2_ref)
    if final:
        ms = jnp.mean(out * out, axis=-1, keepdims=True)
        out = out * lax.rsqrt(ms + EPS) * fg_ref[...]
    o_ref[0] = out


def _out_mlp(ys, h, mod9, g2, wo, w1, w2, final_g, tm, name):
    b, n, d = h.shape
    dff = w1.shape[1]
    final = final_g is not None
    tile = pl.BlockSpec((1, tm, d), lambda bi, i: (bi, i, 0))
    in_specs = [pl.BlockSpec((1, tm, y.shape[-1]), lambda bi, i: (bi, i, 0)) if y.ndim == 3 else
                pl.BlockSpec((1, y.shape[1], tm, y.shape[-1]), lambda bi, i: (bi, 0, i, 0)) for y in ys]
    in_specs += [tile, pl.BlockSpec((1, 6, d), lambda bi, i: (bi, 0, 0)), _const_spec((1, d)),
                 _const_spec((d, d)), _const_spec((d, dff)), _const_spec((dff, d))]
    args = list(ys) + [h, mod9, g2.reshape(1, d), wo, w1, w2]
    if final:
        in_specs.append(_const_spec((1, d)))
        args.append(final_g.reshape(1, d))
    return pl.pallas_call(
        functools.partial(_mlp_kernel, n_y=len(ys), final=final),
        grid=(b, n // tm),
        in_specs=in_specs,
        out_specs=tile,
        out_shape=jax.ShapeDtypeStruct((b, n, d), F32),
        compiler_params=_vmem(56 * MIB),
        name=name,
    )(*args)


def _ctx_kernel(sink_ref, ctx_ref, mod0_ref, mod1_ref, g10_ref, g20_ref, g11_ref, wk0_ref, wt0_ref, wo_ref,
                w1_ref, w2_ref, wk1_ref, wv1t_ref, cak_ref, cavt_ref, cbk_ref, cbvt_ref, kkc_ref, vtc_ref):
    x = ctx_ref[0]
    c = x.shape[0]
    a = _modulated(x, g10_ref[...], mod0_ref[0, 0:1, :], mod0_ref[0, 1:2, :]).astype(BF16)
    kw = cak_ref.shape[-1]
    k_nat = _dot(a, wk0_ref[...]).astype(BF16)
    cak_ref[0] = k_nat[:, :kw]
    cbk_ref[0] = k_nat[:, kw:]
    rows = wt0_ref.shape[0] // 2
    qw = rows - kw
    lane_head = lax.broadcasted_iota(jnp.int32, (1, 4 * c), 1) // c
    ys = []
    for mixer, vt_ref in enumerate((cavt_ref, cbvt_ref)):
        xt = _dot_nt(wt0_ref[mixer * rows:(mixer + 1) * rows, :], a)
        vt = xt[qw:].astype(BF16)
        vt_ref[0] = vt
        km = k_nat[:, mixer * kw:(mixer + 1) * kw]
        cols = []
        for h in range(2):
            top = jnp.concatenate([xt[(4 * h + g) * HEAD_DIM:(4 * h + g + 1) * HEAD_DIM] for g in range(4)], axis=1)
            s = _dot(km, _padded_queries(top.astype(BF16), h))
            sink = None
            if mixer == 0:
                sink = jnp.full((1, 4 * c), sink_ref[4 * h + 3], F32)
                for g in range(3):
                    sink = jnp.where(lane_head == g, sink_ref[4 * h + g], sink)
            p, l = _column_softmax([s], sink)
            o = _dot(vt, p)[h * HEAD_DIM:(h + 1) * HEAD_DIM] / l
            cols.append(jnp.concatenate([o[:, g * c:(g + 1) * c] for g in range(4)], axis=0).T)
        ys.append(jnp.concatenate(cols, axis=1).astype(BF16))
    hc = _residual_mlp(x, ys, mod0_ref, g20_ref, wo_ref, w1_ref, w2_ref)
    a1 = _modulated(hc, g11_ref[...], mod1_ref[0, 0:1, :], mod1_ref[0, 1:2, :]).astype(BF16)
    dv = kkc_ref.shape[-1]
    kk = _dot(a1, wk1_ref[...])
    vt1 = _dot_nt(wv1t_ref[...], a1)
    for h in range(kkc_ref.shape[1]):
        kkc_ref[0, h] = kk[:, h * dv:(h + 1) * dv].astype(BF16)
        vtc_ref[0, h, 0] = vt1[h * dv:(h + 1) * dv].astype(BF16)


def _ctx_layer0(sink, ctx, mod0, mod1, g10, g20, g11, wk0, wt0, wo, w1, w2, wk1, wv1t):
    b, c, d = ctx.shape
    kw = wk0.shape[1] // 2
    dv = 2 * HEAD_DIM
    n_heads = wk1.shape[1] // dv
    row = lambda v: v.reshape(1, d)
    k_spec = pl.BlockSpec((1, c, kw), lambda bi: (bi, 0, 0))
    vt_spec = pl.BlockSpec((1, kw, c), lambda bi: (bi, 0, 0))
    k_shape = jax.ShapeDtypeStruct((b, c, kw), BF16)
    vt_shape = jax.ShapeDtypeStruct((b, kw, c), BF16)
    return pl.pallas_call(
        _ctx_kernel,
        grid=(b,),
        in_specs=[
            pl.BlockSpec(memory_space=pltpu.SMEM),
            pl.BlockSpec((1, c, d), lambda bi: (bi, 0, 0)),
            _const_spec((1, 6, d)), _const_spec((1, 6, d)),
            _const_spec((1, d)), _const_spec((1, d)), _const_spec((1, d)),
        ] + [_const_spec(w.shape) for w in (wk0, wt0, wo, w1, w2, wk1, wv1t)],
        out_specs=[k_spec, vt_spec, k_spec, vt_spec,
                   pl.BlockSpec((1, n_heads, c, dv), lambda bi: (bi, 0, 0, 0)),
                   pl.BlockSpec((1, n_heads, 1, dv, c), lambda bi: (bi, 0, 0, 0, 0))],
        out_shape=[k_shape, vt_shape, k_shape, vt_shape,
                   jax.ShapeDtypeStruct((b, n_heads, c, dv), BF16),
                   jax.ShapeDtypeStruct((b, n_heads, 1, dv, c), BF16)],
        compiler_params=_vmem(48 * MIB),
        name="ctx_layer0",
    )(sink, ctx, mod0, mod1, row(g10), row(g20), row(g11), wk0, wt0, wo, w1, w2, wk1, wv1t)


def _rope_tables(n):
    t = jnp.arange(n, dtype=jnp.int32)
    row = (t // GRID_W).astype(F32)
    col = (t % GRID_W).astype(F32)
    quarter = HEAD_DIM // 4
    inv_freq = ROPE_THETA ** (-jnp.arange(quarter, dtype=F32) / quarter)
    ar = row[:, None] * inv_freq[None, :]
    ac = col[:, None] * inv_freq[None, :]
    ang = jnp.concatenate([ar, ar, ac, ac], axis=-1)
    cos, sin = jnp.cos(ang), jnp.sin(ang)
    even = (jnp.arange(HEAD_DIM) // quarter) % 2 == 0
    rep = LANES // HEAD_DIM
    sin_a = jnp.tile(jnp.where(even, -sin, 0.0), (1, rep))
    sin_b = jnp.tile(jnp.where(even, 0.0, sin), (1, rep))
    token_major = (jnp.tile(cos, (1, rep)), sin_a, sin_b)
    feature_major = (cos.T, jnp.where(even, -sin, sin).T)
    return token_major, feature_major


def kernel(x, c, ctx, c_ctx, ada_w, ada_b, norm1_g, norm2_g, even_w_in, even_w_out, a_sink, b_rpb,
           odd_w_in, odd_w_out, lam_q1, lam_k1, lam_q2, lam_k2, subln_g, mlp_w1, mlp_w2, final_g):
    b, n, d = x.shape
    n_ctx = ctx.shape[1]
    assert ada_w.shape[0] == 2 and d == 1024 and n % 512 == 0
    scale = HEAD_DIM ** -0.5
    a_qw, a_kvw = 8 * HEAD_DIM, 2 * HEAD_DIM
    b_q0 = a_qw + 2 * a_kvw

    cc = jnp.zeros((16, d), F32).at[:b].set(c).at[b].set(c_ctx)
    mod = _ada_call(cc, ada_w, ada_b)[:, :b + 1].reshape(2, b + 1, 6, d)
    rope, rope_t = _rope_tables(n)

    qscale = scale * LOG2E
    cols0 = jnp.arange(even_w_in.shape[2])
    qcols0 = (cols0 < a_qw) | ((cols0 >= b_q0) & (cols0 < b_q0 + a_qw))
    w_in0 = (even_w_in[0] * jnp.where(qcols0, qscale, 1.0)).astype(BF16)
    cols1 = jnp.arange(odd_w_in.shape[2])
    w_in1 = (odd_w_in[0] * jnp.where(cols1 < 1024, qscale, 1.0)).astype(BF16)
    w_out0, w_out1 = even_w_out[0].astype(BF16), odd_w_out[0].astype(BF16)
    w1, w2 = mlp_w1.astype(BF16), mlp_w2.astype(BF16)

    ak0, av0, bq0, bk0, bv0 = a_qw, a_qw + a_kvw, b_q0, b_q0 + a_qw, b_q0 + a_qw + a_kvw
    wk0 = jnp.concatenate([w_in0[:, ak0:av0], w_in0[:, bk0:bv0]], axis=1)
    wt0 = jnp.concatenate([w_in0[:, :ak0], w_in0[:, av0:bq0], w_in0[:, bq0:bk0], w_in0[:, bv0:]], axis=1).T
    wq1, wk1, wv1 = w_in1[:, :1024], w_in1[:, 1024:2048], w_in1[:, 2048:]
    wt1 = jnp.concatenate([wq1, wv1], axis=1).T
    sink = a_sink[0].astype(F32) * LOG2E

    cak, cav_t, cbk, cbv_t, kkc, vtc = _ctx_layer0(
        sink, ctx, mod[0, b:b + 1], mod[1, b:b + 1], norm1_g[0], norm2_g[0], norm1_g[1],
        wk0, wt0, w_out0, w1[0], w2[0], wk1, wt1[1024:])

    aq_t, ak, av_t, bq_t, bk, bv_t = _project0(x, mod[0], norm1_g[0], wk0, wt0, rope, rope_t, 1024, WINDOW)
    ya = _window_attn(sink, aq_t, ak, av_t, cak, cav_t)
    yb = _neighbourhood_attn(bq_t, bk, bv_t, cbk, cbv_t, _na_bias(b_rpb[0].astype(F32)))
    hx = _out_mlp([ya, yb], x, mod[0], norm2_g[0], w_out0, w1[0], w2[0], None, 1024, "mlp0_x")

    lam_init = 0.8 - 0.6 * math.exp(-0.3 * 1)
    qt, kk, vt = _project1(hx, mod[1], norm1_g[1], wk1, wt1, rope, rope_t, 1024, n_ctx)
    lam_vecs = [v[0].reshape(1, HEAD_DIM).astype(F32) for v in (lam_q1, lam_k1, lam_q2, lam_k2)]
    yx = _diff_attn(lam_vecs, subln_g[0].astype(F32), qt, kk, kkc, vt, vtc, lam_init, 2048, 8)
    return _out_mlp([yx], hx, mod[1], norm2_g[1], w_out1, w1[1], w2[1], final_g, 1024, "mlp1_x")
```

```python
import functools
import math

import jax
import jax.numpy as jnp
from jax import lax
from jax.experimental import pallas as pl
from jax.experimental.pallas import tpu as pltpu

F32 = jnp.float32
BF16 = jnp.bfloat16

GRID_W = 64
HEAD_DIM = 64
WINDOW = 128
NA_ROWS = 8
NA_COLS = 16
ROPE_THETA = 10000.0
EPS = 1e-6
NEG = -1e30
LOG2E = math.log2(math.e)

MIB = 1024 * 1024
V7X_VMEM_BYTES = 64 * MIB
VMEM_RESERVE_BYTES = 8 * MIB
LANES = 128


def _vmem(nbytes, **kwargs):
    limit = int(min(nbytes, V7X_VMEM_BYTES - VMEM_RESERVE_BYTES))
    return pltpu.CompilerParams(vmem_limit_bytes=limit, **kwargs)


def _const_spec(shape):
    n = len(shape)
    return pl.BlockSpec(shape, lambda *_: (0,) * n, pipeline_mode=pl.Buffered(1))


def _dot(a, b):
    return jnp.dot(a, b, preferred_element_type=F32)


def _dot_nt(a, b):
    return lax.dot_general(a, b, (((1,), (1,)), ((), ())), preferred_element_type=F32)


def _ada_kernel(c_ref, w_ref, b_ref, o_ref):
    cc = c_ref[...]
    s = cc * jax.nn.sigmoid(cc)
    o_ref[0] = jnp.dot(s, w_ref[0], preferred_element_type=F32, precision=lax.Precision.HIGHEST) + b_ref[0]


def _ada_call(cc, ada_w, ada_b):
    depth, d, six_d = ada_w.shape
    tn = 1536
    return pl.pallas_call(
        _ada_kernel,
        grid=(depth, six_d // tn),
        in_specs=[
            pl.BlockSpec((16, d), lambda l, j: (0, 0)),
            pl.BlockSpec((1, d, tn), lambda l, j: (l, 0, j)),
            pl.BlockSpec((1, 1, tn), lambda l, j: (l, 0, j)),
        ],
        out_specs=pl.BlockSpec((1, 16, tn), lambda l, j: (l, 0, j)),
        out_shape=jax.ShapeDtypeStruct((depth, 16, six_d), F32),
        compiler_params=_vmem(32 * MIB),
        name="ada_mod",
    )(cc, ada_w, ada_b.reshape(depth, 1, six_d))


def _modulated(x, g, shift, scale):
    ms = jnp.mean(x * x, axis=-1, keepdims=True)
    return (x * lax.rsqrt(ms + EPS) * g) * (1.0 + scale) + shift


def _rope_lanes(x, cos, sin_a, sin_b):
    return x * cos + pltpu.roll(x, LANES - 16, 1) * sin_a + pltpu.roll(x, 16, 1) * sin_b


def _rope_rows(x, cos_t, sin_t):
    q = HEAD_DIM // 4
    rot = jnp.concatenate([x[q:2 * q], x[0:q], x[3 * q:4 * q], x[2 * q:3 * q]], axis=0)
    return x * cos_t + rot * sin_t


def _proj1_kernel(h_ref, mod_ref, g_ref, wk_ref, wt_ref, cos_ref, sa_ref, sb_ref, cost_ref, sint_ref,
                  qt_ref, kk_ref, vt_ref, *, tk):
    a = _modulated(h_ref[0], g_ref[...], mod_ref[0, 0:1, :], mod_ref[0, 1:2, :]).astype(BF16)
    tm = a.shape[0]
    dv = 2 * HEAD_DIM
    n_heads = kk_ref.shape[1]
    chunk = 4 * dv
    for c0 in range(0, n_heads * dv, chunk):
        p = _dot(a, wk_ref[:, c0:c0 + chunk])
        for j in range(chunk // dv):
            xc = _rope_lanes(p[:, j * dv:(j + 1) * dv], cos_ref[...], sa_ref[...], sb_ref[...])
            kk_ref[0, c0 // dv + j] = xc.astype(BF16)
    n_q = n_heads * dv
    for f0 in range(0, wt_ref.shape[0], chunk):
        xt = _dot_nt(wt_ref[f0:f0 + chunk, :], a)
        for j in range(chunk // dv):
            row = f0 + j * dv
            hd = xt[j * dv:(j + 1) * dv]
            if row < n_q:
                hd = jnp.concatenate([_rope_rows(hd[m * HEAD_DIM:(m + 1) * HEAD_DIM], cost_ref[...], sint_ref[...])
                                      for m in range(2)], axis=0)
                qt_ref[0, row // dv] = hd.astype(BF16)
            else:
                for t in range(tm // tk):
                    vt_ref[0, (row - n_q) // dv, t] = hd[:, t * tk:(t + 1) * tk].astype(BF16)


def _project1(h, mod9, g, wk, wt, rope, rope_t, tm, tk):
    b, n, d = h.shape
    dv = 2 * HEAD_DIM
    n_heads = wk.shape[1] // dv
    return pl.pallas_call(
        functools.partial(_proj1_kernel, tk=tk),
        grid=(b, n // tm),
        in_specs=[
            pl.BlockSpec((1, tm, d), lambda bi, i: (bi, i, 0)),
            pl.BlockSpec((1, 6, d), lambda bi, i: (bi, 0, 0)),
            _const_spec((1, d)),
            _const_spec(wk.shape),
            _const_spec(wt.shape),
        ] + [pl.BlockSpec((tm, LANES), lambda bi, i: (i, 0))] * 3
          + [pl.BlockSpec((HEAD_DIM, tm), lambda bi, i: (0, i))] * 2,
        out_specs=[
            pl.BlockSpec((1, n_heads, dv, tm), lambda bi, i: (bi, 0, 0, i)),
            pl.BlockSpec((1, n_heads, tm, dv), lambda bi, i: (bi, 0, i, 0)),
            pl.BlockSpec((1, n_heads, tm // tk, dv, tk), lambda bi, i: (bi, 0, i, 0, 0)),
        ],
        out_shape=[
            jax.ShapeDtypeStruct((b, n_heads, dv, n), BF16),
            jax.ShapeDtypeStruct((b, n_heads, n, dv), BF16),
            jax.ShapeDtypeStruct((b, n_heads, n // tk, dv, tk), BF16),
        ],
        compiler_params=_vmem(48 * MIB),
        name="proj1_x",
    )(h, mod9, g.reshape(1, d), wk, wt, *rope, *rope_t)


def _proj0_kernel(h_ref, mod_ref, g_ref, wk_ref, wt_ref, cos_ref, sa_ref, sb_ref, cost_ref, sint_ref,
                  qa_ref, ka_ref, va_ref, qb_ref, kb_ref, vb_ref, *, tw):
    a = _modulated(h_ref[0], g_ref[...], mod_ref[0, 0:1, :], mod_ref[0, 1:2, :]).astype(BF16)
    tm = a.shape[0]
    kw = ka_ref.shape[-1]
    p = _dot(a, wk_ref[...])
    ka_ref[0] = _rope_lanes(p[:, :kw], cos_ref[...], sa_ref[...], sb_ref[...]).astype(BF16)
    kb_ref[0] = p[:, kw:].astype(BF16)
    qw = qa_ref.shape[2]
    rows = qw + va_ref.shape[2]
    for mixer, (q_ref, v_ref) in enumerate(((qa_ref, va_ref), (qb_ref, vb_ref))):
        xt = _dot_nt(wt_ref[mixer * rows:(mixer + 1) * rows, :], a)
        q = xt[:qw]
        if mixer == 0:
            q = jnp.concatenate([_rope_rows(q[h * HEAD_DIM:(h + 1) * HEAD_DIM], cost_ref[...], sint_ref[...])
                                 for h in range(qw // HEAD_DIM)], axis=0)
        for t in range(tm // tw):
            q_ref[0, t] = q[:, t * tw:(t + 1) * tw].astype(BF16)
            v_ref[0, t] = xt[qw:, t * tw:(t + 1) * tw].astype(BF16)


def _project0(h, mod9, g, wk, wt, rope, rope_t, tm, tw):
    b, n, d = h.shape
    kw = wk.shape[1] // 2
    rows = wt.shape[0] // 2
    qw = rows - kw
    q_spec = pl.BlockSpec((1, tm // tw, qw, tw), lambda bi, i: (bi, i, 0, 0))
    k_spec = pl.BlockSpec((1, tm, kw), lambda bi, i: (bi, i, 0))
    v_spec = pl.BlockSpec((1, tm // tw, kw, tw), lambda bi, i: (bi, i, 0, 0))
    q_shape = jax.ShapeDtypeStruct((b, n // tw, qw, tw), BF16)
    k_shape = jax.ShapeDtypeStruct((b, n, kw), BF16)
    v_shape = jax.ShapeDtypeStruct((b, n // tw, kw, tw), BF16)
    return pl.pallas_call(
        functools.partial(_proj0_kernel, tw=tw),
        grid=(b, n // tm),
        in_specs=[
            pl.BlockSpec((1, tm, d), lambda bi, i: (bi, i, 0)),
            pl.BlockSpec((1, 6, d), lambda bi, i: (bi, 0, 0)),
            _const_spec((1, d)),
            _const_spec(wk.shape),
            _const_spec(wt.shape),
        ] + [pl.BlockSpec((tm, LANES), lambda bi, i: (i, 0))] * 3
          + [pl.BlockSpec((HEAD_DIM, tm), lambda bi, i: (0, i))] * 2,
        out_specs=[q_spec, k_spec, v_spec, q_spec, k_spec, v_spec],
        out_shape=[q_shape, k_shape, v_shape, q_shape, k_shape, v_shape],
        compiler_params=_vmem(48 * MIB),
        name="proj0_x",
    )(h, mod9, g.reshape(1, d), wk, wt, *rope, *rope_t)


def _pipelined(items, ahead):
    scores = [items[k][0]() for k in range(min(ahead, len(items)))]
    for k in range(len(items)):
        if k + ahead < len(items):
            scores.append(items[k + ahead][0]())
        items[k][1](scores[k])
        scores[k] = None


SHIFT_SLACK = 1.0 + 2.0 ** -10
L_FLOOR = 2.0 ** -80


def _key_norm_bounds(k_ref, kc_ref, kb_ref):
    tile = kc_ref.shape[1]
    nt = k_ref.shape[1] // tile
    width = k_ref.shape[2]

    def group_max(kt):
        kf = kt.astype(F32)
        return jnp.max((kf * kf).reshape(tile // 8, 8, width), axis=0)

    def body(j, carry):
        kb_ref[j] = group_max(k_ref[0, pl.ds(pl.multiple_of(j * tile, tile), tile), :])
        return carry

    lax.fori_loop(0, nt, body, 0)
    kb_ref[nt] = group_max(kc_ref[0])
    best = kb_ref[...].reshape((nt + 1) * 8, width)
    half = lax.broadcasted_iota(jnp.int32, best.shape, 1) < width // 2
    k0 = jnp.max(jnp.sum(jnp.where(half, best, 0.0), axis=1, keepdims=True), axis=0, keepdims=True)
    k1 = jnp.max(jnp.sum(jnp.where(half, 0.0, best), axis=1, keepdims=True), axis=0, keepdims=True)
    return jnp.sqrt(k0), jnp.sqrt(k1)


def _column_softmax(pieces, extra_row=None, shift=None):
    if shift is None:
        m = functools.reduce(jnp.maximum, [jnp.max(s, axis=0, keepdims=True) for s in pieces])
        if extra_row is not None:
            m = jnp.maximum(m, extra_row)
    else:
        m = shift
    ps = [jnp.exp2(s - m) for s in pieces]
    l = functools.reduce(jnp.add, [jnp.sum(p, axis=0, keepdims=True) for p in ps])
    if extra_row is not None:
        l = l + jnp.exp2(extra_row - m)
    return jnp.concatenate([p.astype(BF16) for p in ps], axis=0), l


def _padded_queries(top, h):
    z = jnp.zeros_like(top)
    return jnp.concatenate([top, z] if h == 0 else [z, top], axis=0)


def _win_kernel(sink_ref, q_ref, k_ref, v_ref, kc_ref, vc_ref, o_ref, kb_ref, *, blk, per_iter):
    n = k_ref.shape[1]
    nb = n // blk
    c_idx = lax.broadcasted_iota(jnp.int32, (blk, blk), 0)
    r_idx = lax.broadcasted_iota(jnp.int32, (blk, blk), 1)
    lower = jnp.tile(jnp.where(c_idx >= r_idx, 0.0, NEG), (1, 4))
    upper = jnp.tile(jnp.where(c_idx <= r_idx, 0.0, NEG), (1, 4))
    lane_head = lax.broadcasted_iota(jnp.int32, (1, 4 * blk), 1) // blk

    def sink_row(h):
        row = jnp.full((1, 4 * blk), sink_ref[4 * h + 3], F32)
        for g in range(3):
            row = jnp.where(lane_head == g, sink_ref[4 * h + g], row)
        return row

    sinks = [sink_row(0), sink_row(1)]
    kmax = _key_norm_bounds(k_ref, kc_ref, kb_ref)

    def run(exact):
        l_min = [jnp.full((1, 4 * blk), jnp.inf, F32)]

        def make_item(i, h, k_rows, v_tiles, biases):
            cache = {}

            def score():
                top = jnp.concatenate([q_ref[0, i, (4 * h + g) * HEAD_DIM:(4 * h + g + 1) * HEAD_DIM, :]
                                       for g in range(4)], axis=1)
                if not exact:
                    tf = top.astype(F32)
                    qn = jnp.sqrt(jnp.sum(tf * tf, axis=0, keepdims=True))
                    cache["shift"] = jnp.maximum(qn * kmax[h] * SHIFT_SLACK, sinks[h])
                return _dot(jnp.concatenate([k_rows(), kc_ref[0]], axis=0), _padded_queries(top, h))

            def finish(s):
                pieces = [s[t * blk:(t + 1) * blk] if bias is None else s[t * blk:(t + 1) * blk] + bias
                          for t, bias in enumerate(biases)]
                pieces.append(s[len(biases) * blk:])
                p, l = _column_softmax(pieces, sinks[h], cache.get("shift"))
                if not exact:
                    l_min[0] = jnp.minimum(l_min[0], l)
                vall = jnp.concatenate(v_tiles() + [vc_ref[0]], axis=1)
                o = _dot(vall, p)[h * HEAD_DIM:(h + 1) * HEAD_DIM] / l
                o4 = jnp.concatenate([o[:, g * blk:(g + 1) * blk] for g in range(4)], axis=0)
                q0 = i * blk if isinstance(i, int) else pl.multiple_of(i * blk, blk)
                o_ref[0, pl.ds(q0, blk), 4 * h * HEAD_DIM:4 * (h + 1) * HEAD_DIM] = o4.T.astype(BF16)

            return score, finish

        def edge_items(i, t0, biases):
            return [make_item(i, h, lambda: k_ref[0, t0 * blk:(t0 + 2) * blk, :],
                              lambda: [v_ref[0, t0], v_ref[0, t0 + 1]], biases) for h in range(2)]

        def inner_items(i):
            k0 = pl.multiple_of((i - 1) * blk, blk)
            return [make_item(i, h, lambda: k_ref[0, pl.ds(k0, 3 * blk), :],
                              lambda: [v_ref[0, i - 1], v_ref[0, i], v_ref[0, i + 1]], [lower, None, upper])
                    for h in range(2)]

        _pipelined(edge_items(0, 0, [None, upper]), 1)

        def body(j, carry):
            l_min[0] = carry
            _pipelined([it for u in range(per_iter) for it in inner_items(1 + per_iter * j + u)], 2)
            return l_min[0]

        l_min[0] = lax.fori_loop(0, (nb - 2) // per_iter, body, l_min[0])
        _pipelined(edge_items(nb - 1, nb - 2, [lower, None]), 1)
        return l_min[0]

    l_fast = run(exact=False)

    @pl.when(jnp.min(l_fast) < L_FLOOR)
    def _():
        run(exact=True)


def _window_attn(sink, q_t, k, v_t, kc, vc_t):
    b, nb, hd, blk = q_t.shape
    n = k.shape[1]
    c = kc.shape[1]
    per_iter = 5
    assert blk == WINDOW and (nb - 2) % per_iter == 0
    return pl.pallas_call(
        functools.partial(_win_kernel, blk=blk, per_iter=per_iter),
        grid=(b,),
        in_specs=[
            pl.BlockSpec(memory_space=pltpu.SMEM),
            pl.BlockSpec((1, nb, hd, blk), lambda bi: (bi, 0, 0, 0)),
            pl.BlockSpec((1, n, 2 * HEAD_DIM), lambda bi: (bi, 0, 0)),
            pl.BlockSpec((1, nb, 2 * HEAD_DIM, blk), lambda bi: (bi, 0, 0, 0)),
            pl.BlockSpec((1, c, 2 * HEAD_DIM), lambda bi: (bi, 0, 0)),
            pl.BlockSpec((1, 2 * HEAD_DIM, c), lambda bi: (bi, 0, 0)),
        ],
        out_specs=pl.BlockSpec((1, n, hd), lambda bi: (bi, 0, 0)),
        out_shape=jax.ShapeDtypeStruct((b, n, hd), BF16),
        scratch_shapes=[pltpu.VMEM((n // c + 1, 8, 2 * HEAD_DIM), F32)],
        compiler_params=_vmem(48 * MIB),
        name="window_attn",
    )(sink, q_t, k, v_t, kc, vc_t)


def _na_bias_kernel(rpb_ref, o_ref):
    h = pl.program_id(0)
    n_roff = 2 * NA_ROWS - 1
    n_coff = 2 * NA_COLS - 1
    shape = (GRID_W, 4 * GRID_W)
    ck = lax.broadcasted_iota(jnp.int32, shape, 0)
    lane = lax.broadcasted_iota(jnp.int32, shape, 1)
    cq = lane % GRID_W
    head = lane // GRID_W
    cstart = jnp.clip(cq - NA_COLS // 2, 0, GRID_W - NA_COLS)
    valid = (ck >= cstart) & (ck < cstart + NA_COLS)
    diff = ck - cq + (NA_COLS - 1)
    hits = [[valid & (diff == o) & (head == g) for o in range(n_coff)] for g in range(4)]
    tiles = []
    for roff in range(n_roff):
        t = jnp.full(shape, NEG, F32)
        for g in range(4):
            for o in range(n_coff):
                t = jnp.where(hits[g][o], rpb_ref[((4 * h + g) * n_roff + roff) * n_coff + o] * LOG2E, t)
        tiles.append(t)
    for ro in range(n_roff - 1):
        o_ref[0, ro] = jnp.concatenate([tiles[ro], tiles[ro + 1]], axis=0)


def _na_bias(rpb):
    hq, n_roff, n_coff = rpb.shape
    return pl.pallas_call(
        _na_bias_kernel,
        grid=(hq // 4,),
        in_specs=[pl.BlockSpec(memory_space=pltpu.SMEM)],
        out_specs=pl.BlockSpec((1, n_roff - 1, 2 * GRID_W, 4 * GRID_W), lambda h: (h, 0, 0, 0)),
        out_shape=jax.ShapeDtypeStruct((hq // 4, n_roff - 1, 2 * GRID_W, 4 * GRID_W), F32),
        name="na_bias",
    )(rpb.reshape(-1))


def _na_kernel(q_ref, k_ref, v_ref, kc_ref, vc_ref, bias_ref, o_ref, vall_ref, kb_ref):
    n = k_ref.shape[1]
    rows = n // GRID_W
    n_tiles = rows // 2
    kr = min(NA_ROWS, rows)
    nkey = kr * GRID_W
    lo = lax.broadcasted_iota(jnp.int32, (2 * HEAD_DIM, 2 * GRID_W), 1) < GRID_W
    lo_q = lax.broadcasted_iota(jnp.int32, (HEAD_DIM, 2 * GRID_W), 1) < GRID_W

    def swap(x):
        return pltpu.roll(x, GRID_W, 1)

    def build(t, carry):
        a = v_ref[0, t]
        vall_ref[0, t] = a
        nxt = v_ref[0, jnp.minimum(t + 1, n_tiles - 1)]
        vall_ref[1, t] = jnp.where(lo, swap(a.astype(F32)), swap(nxt.astype(F32))).astype(BF16)
        return carry

    lax.fori_loop(0, n_tiles, build, 0)

    def row_start(r):
        return jnp.clip(r - kr // 2, 0, rows - kr)

    kmax = _key_norm_bounds(k_ref, kc_ref, kb_ref)
    bias_max = [jnp.maximum(jnp.max(jnp.max(bias_ref[h], axis=0), axis=0, keepdims=True), 0.0) for h in range(2)]
    tiles_per_iter = 4

    def run(exact):
        l_min = [jnp.full((1, 4 * GRID_W), jnp.inf, F32)]

        def make_item(t, par, h, done):
            r = 2 * t + par
            cache = {}

            def score():
                cols = []
                for c in range(2):
                    g0 = 4 * h + 2 * c
                    a0 = q_ref[0, t, g0 * HEAD_DIM:(g0 + 1) * HEAD_DIM, :].astype(F32)
                    a1 = q_ref[0, t, (g0 + 1) * HEAD_DIM:(g0 + 2) * HEAD_DIM, :].astype(F32)
                    cols.append(jnp.where(lo_q, a0, swap(a1)) if par == 0 else jnp.where(lo_q, swap(a0), a1))
                top = jnp.concatenate(cols, axis=1)
                if not exact:
                    qn = jnp.sqrt(jnp.sum(top * top, axis=0, keepdims=True))
                    cache["shift"] = qn * kmax[h] * SHIFT_SLACK + bias_max[h]
                k0 = pl.multiple_of(row_start(r) * GRID_W, GRID_W)
                return _dot(jnp.concatenate([k_ref[0, pl.ds(k0, nkey), :], kc_ref[0]], axis=0),
                            _padded_queries(top.astype(BF16), h))

            def finish(s):
                rs = row_start(r)
                ro0 = rs - r + (NA_ROWS - 1)
                bias = jnp.concatenate([bias_ref[h, ro0 + 2 * i] for i in range(kr // 2)], axis=0)
                p, l = _column_softmax([s[:nkey] + bias, s[nkey:]], None, cache.get("shift"))
                if not exact:
                    l_min[0] = jnp.minimum(l_min[0], l)
                vwin = [vall_ref[rs & 1, (rs >> 1) + i] for i in range(nkey // (2 * GRID_W))]
                o = _dot(jnp.concatenate(vwin + [vc_ref[0]], axis=1), p)[h * HEAD_DIM:(h + 1) * HEAD_DIM] / l
                done.append(o)
                if par == 1:
                    o_even, o_odd = done
                    ws = []
                    for c in range(2):
                        ce = o_even[:, c * 2 * GRID_W:(c + 1) * 2 * GRID_W]
                        co = o_odd[:, c * 2 * GRID_W:(c + 1) * 2 * GRID_W]
                        ws.append(jnp.where(lo_q, ce, swap(co)))
                        ws.append(jnp.where(lo_q, swap(ce), co))
                    w = jnp.concatenate(ws, axis=0)
                    q0 = pl.multiple_of(t * 2 * GRID_W, 2 * GRID_W)
                    o_ref[0, pl.ds(q0, 2 * GRID_W), 4 * h * HEAD_DIM:4 * (h + 1) * HEAD_DIM] = w.T.astype(BF16)

            return score, finish

        def body(j, carry):
            l_min[0] = carry
            items = []
            for u in range(tiles_per_iter):
                for h in range(2):
                    done = []
                    items += [make_item(tiles_per_iter * j + u, par, h, done) for par in range(2)]
            _pipelined(items, 2)
            return l_min[0]

        return lax.fori_loop(0, n_tiles // tiles_per_iter, body, l_min[0])

    l_fast = run(exact=False)

    @pl.when(jnp.min(l_fast) < L_FLOOR)
    def _():
        run(exact=True)


def _neighbourhood_attn(q_t, k, v_t, kc, vc_t, bias):
    b, n_tiles, hd, tw = q_t.shape
    n = k.shape[1]
    c = kc.shape[1]
    assert tw == 2 * GRID_W and n_tiles * tw == n
    return pl.pallas_call(
        _na_kernel,
        grid=(b,),
        in_specs=[
            pl.BlockSpec((1, n_tiles, hd, tw), lambda bi: (bi, 0, 0, 0)),
            pl.BlockSpec((1, n, 2 * HEAD_DIM), lambda bi: (bi, 0, 0)),
            pl.BlockSpec((1, n_tiles, 2 * HEAD_DIM, tw), lambda bi: (bi, 0, 0, 0)),
            pl.BlockSpec((1, c, 2 * HEAD_DIM), lambda bi: (bi, 0, 0)),
            pl.BlockSpec((1, 2 * HEAD_DIM, c), lambda bi: (bi, 0, 0)),
            _const_spec(bias.shape),
        ],
        out_specs=pl.BlockSpec((1, n, hd), lambda bi: (bi, 0, 0)),
        out_shape=jax.ShapeDtypeStruct((b, n, hd), BF16),
        scratch_shapes=[pltpu.VMEM((2, n_tiles, 2 * HEAD_DIM, tw), BF16),
                        pltpu.VMEM((n // c + 1, 8, 2 * HEAD_DIM), F32)],
        compiler_params=_vmem(52 * MIB),
        name="neighbourhood_attn",
    )(q_t, k, v_t, kc, vc_t, bias)


def _diff_kernel(lq1_ref, lk1_ref, lq2_ref, lk2_ref, g_ref, qt_ref, kk_ref, kkc_ref, vt_ref, vtc_ref, o_ref,
                 qz_ref, kb_ref, kmax_ref, m_ref, l_ref, acc1_ref, acc2_ref, *, lam_init, n_sub):
    tq = qt_ref.shape[-1] // n_sub
    n_lt, dv, tk = vt_ref.shape[2:]
    n_kt = n_lt + 1
    assert kkc_ref.shape[2] == tk and vtc_ref.shape[2] == 1

    def k_tile(j):
        return kk_ref[0, 0, j * tk:(j + 1) * tk, :] if j < n_lt else kkc_ref[0, 0]

    def v_tile(j):
        return vt_ref[0, 0, j] if j < n_lt else vtc_ref[0, 0, 0]

    @pl.when(pl.program_id(2) == 0)
    def _():
        def group_max(kt):
            kf = kt.astype(F32)
            return jnp.max((kf * kf).reshape(tk // 8, 8, dv), axis=0)

        def kbody(j, carry):
            kb_ref[j] = group_max(kk_ref[0, 0, pl.ds(pl.multiple_of(j * tk, tk), tk), :])
            return carry

        lax.fori_loop(0, n_lt, kbody, 0)
        kb_ref[n_lt] = group_max(kkc_ref[0, 0])
        best = kb_ref[...].reshape(n_kt * 8, dv)
        half = lax.broadcasted_iota(jnp.int32, best.shape, 1) < HEAD_DIM
        k1 = jnp.max(jnp.sum(jnp.where(half, best, 0.0), axis=1, keepdims=True), axis=0, keepdims=True)
        k2 = jnp.max(jnp.sum(jnp.where(half, 0.0, best), axis=1, keepdims=True), axis=0, keepdims=True)
        kmax_ref[:, :tq] = jnp.broadcast_to(jnp.sqrt(k1), (1, tq))
        kmax_ref[:, tq:] = jnp.broadcast_to(jnp.sqrt(k2), (1, tq))

    zero = jnp.zeros((HEAD_DIM, tq), BF16)
    state = []
    for s in range(n_sub):
        qt = qt_ref[0, 0, :, s * tq:(s + 1) * tq]
        qz_ref[s, 0:HEAD_DIM, 0:tq] = qt[0:HEAD_DIM]
        qz_ref[s, 0:HEAD_DIM, tq:] = zero
        qz_ref[s, HEAD_DIM:, 0:tq] = zero
        qz_ref[s, HEAD_DIM:, tq:] = qt[HEAD_DIM:]
        qf = qt.astype(F32)
        qsq = qf * qf
        qn = jnp.concatenate([jnp.sum(qsq[:HEAD_DIM], axis=0, keepdims=True),
                              jnp.sum(qsq[HEAD_DIM:], axis=0, keepdims=True)], axis=1)
        state.append(dict(qz=qz_ref[s], shift=jnp.sqrt(qn) * kmax_ref[...] * SHIFT_SLACK,
                          l8=jnp.zeros((8, 2 * tq), F32), acc1=jnp.zeros((dv, tq), F32), acc2=jnp.zeros((dv, tq), F32)))

    lam = (jnp.exp(jnp.sum(lq1_ref[...] * lk1_ref[...], axis=-1, keepdims=True))
           - jnp.exp(jnp.sum(lq2_ref[...] * lk2_ref[...], axis=-1, keepdims=True)) + lam_init)

    def finalize(s, l, acc1, acc2):
        o = acc1 / l[:, :tq] - lam * (acc2 / l[:, tq:])
        ms = jnp.mean(o * o, axis=0, keepdims=True)
        o = (o * lax.rsqrt(ms + EPS) * g_ref[...]) * (1.0 - lam_init)
        o_ref[0, 0, s * tq:(s + 1) * tq, :] = o.T.astype(BF16)

    def make_item(s, j):
        st = state[s]

        def finish(scores):
            p = jnp.exp2(scores - st["shift"])
            st["l8"] = st["l8"] + jnp.sum(p.reshape(tk // 8, 8, 2 * tq), axis=0)
            pb = p.astype(BF16)
            vt = v_tile(j)
            st["acc1"] = st["acc1"] + _dot(vt, pb[:, :tq])
            st["acc2"] = st["acc2"] + _dot(vt, pb[:, tq:])
            if j == n_kt - 1:
                l_fast = jnp.sum(st["l8"], axis=0, keepdims=True)
                l_ref[s] = l_fast
                finalize(s, l_fast, st["acc1"], st["acc2"])

        return (lambda: _dot(k_tile(j), st["qz"])), finish

    _pipelined([make_item(s, j) for s in range(n_sub) for j in range(n_kt)], 2)
    l_min = functools.reduce(jnp.minimum, [l_ref[s] for s in range(n_sub)])

    @pl.when(jnp.min(l_min) < L_FLOOR)
    def _():
        for s in range(n_sub):
            m_ref[...] = jnp.full(m_ref.shape, NEG, F32)
            l_ref[s] = jnp.zeros((1, 2 * tq), F32)
            acc1_ref[s] = jnp.zeros((dv, tq), F32)
            acc2_ref[s] = jnp.zeros((dv, tq), F32)

            def step(kt, vt):
                st = _dot(kt, qz_ref[s])
                m_old = m_ref[...]
                m_new = jnp.maximum(m_old, jnp.max(st, axis=0, keepdims=True))
                alpha = jnp.exp2(m_old - m_new)
                p = jnp.exp2(st - m_new)
                l_ref[s] = alpha * l_ref[s] + jnp.sum(p, axis=0, keepdims=True)
                m_ref[...] = m_new
                pb = p.astype(BF16)
                acc1_ref[s] = alpha[:, :tq] * acc1_ref[s] + _dot(vt, pb[:, :tq])
                acc2_ref[s] = alpha[:, tq:] * acc2_ref[s] + _dot(vt, pb[:, tq:])

            def body(j, carry):
                step(kk_ref[0, 0, pl.ds(pl.multiple_of(j * tk, tk), tk), :], vt_ref[0, 0, j])
                return carry

            lax.fori_loop(0, n_lt, body, 0)
            step(kkc_ref[0, 0], vtc_ref[0, 0, 0])
            finalize(s, l_ref[s], acc1_ref[s], acc2_ref[s])


def _diff_attn(lam_vecs, sub_g, qt, kk, kkc, vt, vtc, lam_init, tq_block, n_sub):
    b, h, dv, n = qt.shape
    n_ctx = kkc.shape[2]
    n_lt, _, tk = vt.shape[2:]
    tq = tq_block // n_sub
    vec_spec = pl.BlockSpec((1, HEAD_DIM), lambda bi, hi, i: (0, 0))
    return pl.pallas_call(
        functools.partial(_diff_kernel, lam_init=lam_init, n_sub=n_sub),
        grid=(b, h, n // tq_block),
        in_specs=[vec_spec] * 4 + [
            pl.BlockSpec((dv, 1), lambda bi, hi, i: (0, 0)),
            pl.BlockSpec((1, 1, dv, tq_block), lambda bi, hi, i: (bi, hi, 0, i)),
            pl.BlockSpec((1, 1, n, dv), lambda bi, hi, i: (bi, hi, 0, 0)),
            pl.BlockSpec((1, 1, n_ctx, dv), lambda bi, hi, i: (bi, hi, 0, 0)),
            pl.BlockSpec((1, 1, n_lt, dv, tk), lambda bi, hi, i: (bi, hi, 0, 0, 0)),
            pl.BlockSpec((1, 1, 1, dv, tk), lambda bi, hi, i: (bi, hi, 0, 0, 0)),
        ],
        out_specs=pl.BlockSpec((1, 1, tq_block, dv), lambda bi, hi, i: (bi, hi, i, 0)),
        out_shape=jax.ShapeDtypeStruct((b, h, n, dv), BF16),
        scratch_shapes=[
            pltpu.VMEM((n_sub, dv, 2 * tq), BF16),
            pltpu.VMEM((n_lt + 1, 8, dv), F32),
            pltpu.VMEM((1, 2 * tq), F32),
            pltpu.VMEM((1, 2 * tq), F32),
            pltpu.VMEM((n_sub, 1, 2 * tq), F32),
            pltpu.VMEM((n_sub, dv, tq), F32),
            pltpu.VMEM((n_sub, dv, tq), F32),
        ],
        compiler_params=_vmem(32 * MIB, dimension_semantics=("arbitrary", "arbitrary", "arbitrary")),
        name="diff_attn",
    )(*lam_vecs, sub_g.reshape(dv, 1), qt, kk, kkc, vt, vtc)


FF_CHUNK = 1024


def _residual_mlp(h, ys, mod_ref, g_ref, wo_ref, w1_ref, w2_ref):
    yw = wo_ref.shape[0] // len(ys)
    y = functools.reduce(jnp.add, [_dot(yj, wo_ref[j * yw:(j + 1) * yw, :]) for j, yj in enumerate(ys)])
    hx = h + mod_ref[0, 2:3, :] * y
    a = _modulated(hx, g_ref[...], mod_ref[0, 3:4, :], mod_ref[0, 4:5, :]).astype(BF16)
    acc = jnp.zeros(hx.shape, F32)
    for c0 in range(0, w1_ref.shape[1], FF_CHUNK):
        u = jnp.maximum(_dot(a, w1_ref[:, c0:c0 + FF_CHUNK]), 0.0)
        acc = acc + _dot((u * u).astype(BF16), w2_ref[c0:c0 + FF_CHUNK, :])
    return hx + mod_ref[0, 5:6, :] * acc


def _mlp_kernel(*refs, n_y, final):
    y_refs, refs = refs[:n_y], refs[n_y:]
    if final:
        h_ref, mod_ref, g_ref, wo_ref, w1_ref, w2_ref, fg_ref, o_ref = refs
    else:
        h_ref, mod_ref, g_ref, wo_ref, w1_ref, w2_ref, o_ref = refs
    ys = [y_ref[0] if len(y_ref.shape) == 3 else jnp.concatenate([y_ref[0, h] for h in range(y_ref.shape[1])], axis=1)
          for y_ref in y_refs]
    out = _residual_mlp(h_ref[0], ys, mod_ref, g_ref, wo_ref, w1_ref, w2_ref)
    if final:
        ms = jnp.mean(out * out, axis=-1, keepdims=True)
        out = out * lax.rsqrt(ms + EPS) * fg_ref[...]
    o_ref[0] = out


def _out_mlp(ys, h, mod9, g2, wo, w1, w2, final_g, tm, name):
    b, n, d = h.shape
    dff = w1.shape[1]
    final = final_g is not None
    tile = pl.BlockSpec((1, tm, d), lambda bi, i: (bi, i, 0))
    in_specs = [pl.BlockSpec((1, tm, y.shape[-1]), lambda bi, i: (bi, i, 0)) if y.ndim == 3 else
                pl.BlockSpec((1, y.shape[1], tm, y.shape[-1]), lambda bi, i: (bi, 0, i, 0)) for y in ys]
    in_specs += [tile, pl.BlockSpec((1, 6, d), lambda bi, i: (bi, 0, 0)), _const_spec((1, d)),
                 _const_spec((d, d)), _const_spec((d, dff)), _const_spec((dff, d))]
    args = list(ys) + [h, mod9, g2.reshape(1, d), wo, w1, w2]
    if final:
        in_specs.append(_const_spec((1, d)))
        args.append(final_g.reshape(1, d))
    return pl.pallas_call(
        functools.partial(_mlp_kernel, n_y=len(ys), final=final),
        grid=(b, n // tm),
        in_specs=in_specs,
        out_specs=tile,
        out_shape=jax.ShapeDtypeStruct((b, n, d), F32),
        compiler_params=_vmem(56 * MIB),
        name=name,
    )(*args)


def _ctx_kernel(sink_ref, ctx_ref, mod0_ref, mod1_ref, g10_ref, g20_ref, g11_ref, wk0_ref, wt0_ref, wo_ref,
                w1_ref, w2_ref, wk1_ref, wv1t_ref, cak_ref, cavt_ref, cbk_ref, cbvt_ref, kkc_ref, vtc_ref):
    x = ctx_ref[0]
    c = x.shape[0]
    a = _modulated(x, g10_ref[...], mod0_ref[0, 0:1, :], mod0_ref[0, 1:2, :]).astype(BF16)
    kw = cak_ref.shape[-1]
    k_nat = _dot(a, wk0_ref[...]).astype(BF16)
    cak_ref[0] = k_nat[:, :kw]
    cbk_ref[0] = k_nat[:, kw:]
    rows = wt0_ref.shape[0] // 2
    qw = rows - kw
    lane_head = lax.broadcasted_iota(jnp.int32, (1, 4 * c), 1) // c
    ys = []
    for mixer, vt_ref in enumerate((cavt_ref, cbvt_ref)):
        xt = _dot_nt(wt0_ref[mixer * rows:(mixer + 1) * rows, :], a)
        vt = xt[qw:].astype(BF16)
        vt_ref[0] = vt
        km = k_nat[:, mixer * kw:(mixer + 1) * kw]
        cols = []
        for h in range(2):
            top = jnp.concatenate([xt[(4 * h + g) * HEAD_DIM:(4 * h + g + 1) * HEAD_DIM] for g in range(4)], axis=1)
            s = _dot(km, _padded_queries(top.astype(BF16), h))
            sink = None
            if mixer == 0:
                sink = jnp.full((1, 4 * c), sink_ref[4 * h + 3], F32)
                for g in range(3):
                    sink = jnp.where(lane_head == g, sink_ref[4 * h + g], sink)
            p, l = _column_softmax([s], sink)
            o = _dot(vt, p)[h * HEAD_DIM:(h + 1) * HEAD_DIM] / l
            cols.append(jnp.concatenate([o[:, g * c:(g + 1) * c] for g in range(4)], axis=0).T)
        ys.append(jnp.concatenate(cols, axis=1).astype(BF16))
    hc = _residual_mlp(x, ys, mod0_ref, g20_ref, wo_ref, w1_ref, w2_ref)
    a1 = _modulated(hc, g11_ref[...], mod1_ref[0, 0:1, :], mod1_ref[0, 1:2, :]).astype(BF16)
    dv = kkc_ref.shape[-1]
    kk = _dot(a1, wk1_ref[...])
    vt1 = _dot_nt(wv1t_ref[...], a1)
    for h in range(kkc_ref.shape[1]):
        kkc_ref[0, h] = kk[:, h * dv:(h + 1) * dv].astype(BF16)
        vtc_ref[0, h, 0] = vt1[h * dv:(h + 1) * dv].astype(BF16)


def _ctx_layer0(sink, ctx, mod0, mod1, g10, g20, g11, wk0, wt0, wo, w1, w2, wk1, wv1t):
    b, c, d = ctx.shape
    kw = wk0.shape[1] // 2
    dv = 2 * HEAD_DIM
    n_heads = wk1.shape[1] // dv
    row = lambda v: v.reshape(1, d)
    k_spec = pl.BlockSpec((1, c, kw), lambda bi: (bi, 0, 0))
    vt_spec = pl.BlockSpec((1, kw, c), lambda bi: (bi, 0, 0))
    k_shape = jax.ShapeDtypeStruct((b, c, kw), BF16)
    vt_shape = jax.ShapeDtypeStruct((b, kw, c), BF16)
    return pl.pallas_call(
        _ctx_kernel,
        grid=(b,),
        in_specs=[
            pl.BlockSpec(memory_space=pltpu.SMEM),
            pl.BlockSpec((1, c, d), lambda bi: (bi, 0, 0)),
            _const_spec((1, 6, d)), _const_spec((1, 6, d)),
            _const_spec((1, d)), _const_spec((1, d)), _const_spec((1, d)),
        ] + [_const_spec(w.shape) for w in (wk0, wt0, wo, w1, w2, wk1, wv1t)],
        out_specs=[k_spec, vt_spec, k_spec, vt_spec,
                   pl.BlockSpec((1, n_heads, c, dv), lambda bi: (bi, 0, 0, 0)),
                   pl.BlockSpec((1, n_heads, 1, dv, c), lambda bi: (bi, 0, 0, 0, 0))],
        out_shape=[k_shape, vt_shape, k_shape, vt_shape,
                   jax.ShapeDtypeStruct((b, n_heads, c, dv), BF16),
                   jax.ShapeDtypeStruct((b, n_heads, 1, dv, c), BF16)],
        compiler_params=_vmem(48 * MIB),
        name="ctx_layer0",
    )(sink, ctx, mod0, mod1, row(g10), row(g20), row(g11), wk0, wt0, wo, w1, w2, wk1, wv1t)


def _rope_tables(n):
    t = jnp.arange(n, dtype=jnp.int32)
    row = (t // GRID_W).astype(F32)
    col = (t % GRID_W).astype(F32)
    quarter = HEAD_DIM // 4
    inv_freq = ROPE_THETA ** (-jnp.arange(quarter, dtype=F32) / quarter)
    ar = row[:, None] * inv_freq[None, :]
    ac = col[:, None] * inv_freq[None, :]
    ang = jnp.concatenate([ar, ar, ac, ac], axis=-1)
    cos, sin = jnp.cos(ang), jnp.sin(ang)
    even = (jnp.arange(HEAD_DIM) // quarter) % 2 == 0
    rep = LANES // HEAD_DIM
    sin_a = jnp.tile(jnp.where(even, -sin, 0.0), (1, rep))
    sin_b = jnp.tile(jnp.where(even, 0.0, sin), (1, rep))
    token_major = (jnp.tile(cos, (1, rep)), sin_a, sin_b)
    feature_major = (cos.T, jnp.where(even, -sin, sin).T)
    return token_major, feature_major


def kernel(x, c, ctx, c_ctx, ada_w, ada_b, norm1_g, norm2_g, even_w_in, even_w_out, a_sink, b_rpb,
           odd_w_in, odd_w_out, lam_q1, lam_k1, lam_q2, lam_k2, subln_g, mlp_w1, mlp_w2, final_g):
    b, n, d = x.shape
    n_ctx = ctx.shape[1]
    assert ada_w.shape[0] == 2 and d == 1024 and n % 512 == 0
    scale = HEAD_DIM ** -0.5
    a_qw, a_kvw = 8 * HEAD_DIM, 2 * HEAD_DIM
    b_q0 = a_qw + 2 * a_kvw

    cc = jnp.zeros((16, d), F32).at[:b].set(c).at[b].set(c_ctx)
    mod = _ada_call(cc, ada_w, ada_b)[:, :b + 1].reshape(2, b + 1, 6, d)
    rope, rope_t = _rope_tables(n)

    qscale = scale * LOG2E
    cols0 = jnp.arange(even_w_in.shape[2])
    qcols0 = (cols0 < a_qw) | ((cols0 >= b_q0) & (cols0 < b_q0 + a_qw))
    w_in0 = (even_w_in[0] * jnp.where(qcols0, qscale, 1.0)).astype(BF16)
    cols1 = jnp.arange(odd_w_in.shape[2])
    w_in1 = (odd_w_in[0] * jnp.where(cols1 < 1024, qscale, 1.0)).astype(BF16)
    w_out0, w_out1 = even_w_out[0].astype(BF16), odd_w_out[0].astype(BF16)
    w1, w2 = mlp_w1.astype(BF16), mlp_w2.astype(BF16)

    ak0, av0, bq0, bk0, bv0 = a_qw, a_qw + a_kvw, b_q0, b_q0 + a_qw, b_q0 + a_qw + a_kvw
    wk0 = jnp.concatenate([w_in0[:, ak0:av0], w_in0[:, bk0:bv0]], axis=1)
    wt0 = jnp.concatenate([w_in0[:, :ak0], w_in0[:, av0:bq0], w_in0[:, bq0:bk0], w_in0[:, bv0:]], axis=1).T
    wq1, wk1, wv1 = w_in1[:, :1024], w_in1[:, 1024:2048], w_in1[:, 2048:]
    wt1 = jnp.concatenate([wq1, wv1], axis=1).T
    sink = a_sink[0].astype(F32) * LOG2E

    cak, cav_t, cbk, cbv_t, kkc, vtc = _ctx_layer0(
        sink, ctx, mod[0, b:b + 1], mod[1, b:b + 1], norm1_g[0], norm2_g[0], norm1_g[1],
        wk0, wt0, w_out0, w1[0], w2[0], wk1, wt1[1024:])

    aq_t, ak, av_t, bq_t, bk, bv_t = _project0(x, mod[0], norm1_g[0], wk0, wt0, rope, rope_t, 1024, WINDOW)
    ya = _window_attn(sink, aq_t, ak, av_t, cak, cav_t)
    yb = _neighbourhood_attn(bq_t, bk, bv_t, cbk, cbv_t, _na_bias(b_rpb[0].astype(F32)))
    hx = _out_mlp([ya, yb], x, mod[0], norm2_g[0], w_out0, w1[0], w2[0], None, 1024, "mlp0_x")

    lam_init = 0.8 - 0.6 * math.exp(-0.3 * 1)
    qt, kk, vt = _project1(hx, mod[1], norm1_g[1], wk1, wt1, rope, rope_t, 1024, n_ctx)
    lam_vecs = [v[0].reshape(1, HEAD_DIM).astype(F32) for v in (lam_q1, lam_k1, lam_q2, lam_k2)]
    yx = _diff_attn(lam_vecs, subln_g[0].astype(F32), qt, kk, kkc, vt, vtc, lam_init, 4096, 16)
    return _out_mlp([yx], hx, mod[1], norm2_g[1], w_out1, w1[1], w2[1], final_g, 1024, "mlp1_x")
```

```python
import functools
import math

import jax
import jax.numpy as jnp
from jax import lax
from jax.experimental import pallas as pl
from jax.experimental.pallas import tpu as pltpu

F32 = jnp.float32
BF16 = jnp.bfloat16

GRID_W = 64
HEAD_DIM = 64
WINDOW = 128
NA_ROWS = 8
NA_COLS = 16
ROPE_THETA = 10000.0
EPS = 1e-6
NEG = -1e30
LOG2E = math.log2(math.e)

MIB = 1024 * 1024
V7X_VMEM_BYTES = 64 * MIB
VMEM_RESERVE_BYTES = 8 * MIB
LANES = 128


def _vmem(nbytes, **kwargs):
    limit = int(min(nbytes, V7X_VMEM_BYTES - VMEM_RESERVE_BYTES))
    return pltpu.CompilerParams(vmem_limit_bytes=limit, **kwargs)


def _const_spec(shape):
    n = len(shape)
    return pl.BlockSpec(shape, lambda *_: (0,) * n, pipeline_mode=pl.Buffered(1))


def _dot(a, b):
    return jnp.dot(a, b, preferred_element_type=F32)


def _dot_nt(a, b):
    return lax.dot_general(a, b, (((1,), (1,)), ((), ())), preferred_element_type=F32)


def _ada_kernel(c_ref, w_ref, b_ref, o_ref):
    cc = c_ref[...]
    s = cc * jax.nn.sigmoid(cc)
    o_ref[0] = jnp.dot(s, w_ref[0], preferred_element_type=F32, precision=lax.Precision.HIGHEST) + b_ref[0]


def _ada_call(cc, ada_w, ada_b):
    depth, d, six_d = ada_w.shape
    tn = 1536
    return pl.pallas_call(
        _ada_kernel,
        grid=(depth, six_d // tn),
        in_specs=[
            pl.BlockSpec((16, d), lambda l, j: (0, 0)),
            pl.BlockSpec((1, d, tn), lambda l, j: (l, 0, j)),
            pl.BlockSpec((1, 1, tn), lambda l, j: (l, 0, j)),
        ],
        out_specs=pl.BlockSpec((1, 16, tn), lambda l, j: (l, 0, j)),
        out_shape=jax.ShapeDtypeStruct((depth, 16, six_d), F32),
        compiler_params=_vmem(32 * MIB),
        name="ada_mod",
    )(cc, ada_w, ada_b.reshape(depth, 1, six_d))


def _modulated(x, g, shift, scale):
    ms = jnp.mean(x * x, axis=-1, keepdims=True)
    return (x * lax.rsqrt(ms + EPS) * g) * (1.0 + scale) + shift


def _rope_lanes(x, cos, sin_a, sin_b):
    return x * cos + pltpu.roll(x, LANES - 16, 1) * sin_a + pltpu.roll(x, 16, 1) * sin_b


def _rope_rows(x, cos_t, sin_t):
    q = HEAD_DIM // 4
    rot = jnp.concatenate([x[q:2 * q], x[0:q], x[3 * q:4 * q], x[2 * q:3 * q]], axis=0)
    return x * cos_t + rot * sin_t


def _proj1_kernel(h_ref, mod_ref, g_ref, wk_ref, wt_ref, cos_ref, sa_ref, sb_ref, cost_ref, sint_ref,
                  qt_ref, kk_ref, vt_ref, *, tk):
    a = _modulated(h_ref[0], g_ref[...], mod_ref[0, 0:1, :], mod_ref[0, 1:2, :]).astype(BF16)
    tm = a.shape[0]
    dv = 2 * HEAD_DIM
    n_heads = kk_ref.shape[1]
    chunk = 4 * dv
    for c0 in range(0, n_heads * dv, chunk):
        p = _dot(a, wk_ref[:, c0:c0 + chunk])
        for j in range(chunk // dv):
            xc = _rope_lanes(p[:, j * dv:(j + 1) * dv], cos_ref[...], sa_ref[...], sb_ref[...])
            kk_ref[0, c0 // dv + j] = xc.astype(BF16)
    n_q = n_heads * dv
    for f0 in range(0, wt_ref.shape[0], chunk):
        xt = _dot_nt(wt_ref[f0:f0 + chunk, :], a)
        for j in range(chunk // dv):
            row = f0 + j * dv
            hd = xt[j * dv:(j + 1) * dv]
            if row < n_q:
                hd = jnp.concatenate([_rope_rows(hd[m * HEAD_DIM:(m + 1) * HEAD_DIM], cost_ref[...], sint_ref[...])
                                      for m in range(2)], axis=0)
                qt_ref[0, row // dv] = hd.astype(BF16)
            else:
                for t in range(tm // tk):
                    vt_ref[0, (row - n_q) // dv, t] = hd[:, t * tk:(t + 1) * tk].astype(BF16)


def _project1(h, mod9, g, wk, wt, rope, rope_t, tm, tk):
    b, n, d = h.shape
    dv = 2 * HEAD_DIM
    n_heads = wk.shape[1] // dv
    return pl.pallas_call(
        functools.partial(_proj1_kernel, tk=tk),
        grid=(b, n // tm),
        in_specs=[
            pl.BlockSpec((1, tm, d), lambda bi, i: (bi, i, 0)),
            pl.BlockSpec((1, 6, d), lambda bi, i: (bi, 0, 0)),
            _const_spec((1, d)),
            _const_spec(wk.shape),
            _const_spec(wt.shape),
        ] + [pl.BlockSpec((tm, LANES), lambda bi, i: (i, 0))] * 3
          + [pl.BlockSpec((HEAD_DIM, tm), lambda bi, i: (0, i))] * 2,
        out_specs=[
            pl.BlockSpec((1, n_heads, dv, tm), lambda bi, i: (bi, 0, 0, i)),
            pl.BlockSpec((1, n_heads, tm, dv), lambda bi, i: (bi, 0, i, 0)),
            pl.BlockSpec((1, n_heads, tm // tk, dv, tk), lambda bi, i: (bi, 0, i, 0, 0)),
        ],
        out_shape=[
            jax.ShapeDtypeStruct((b, n_heads, dv, n), BF16),
            jax.ShapeDtypeStruct((b, n_heads, n, dv), BF16),
            jax.ShapeDtypeStruct((b, n_heads, n // tk, dv, tk), BF16),
        ],
        compiler_params=_vmem(48 * MIB),
        name="proj1_x",
    )(h, mod9, g.reshape(1, d), wk, wt, *rope, *rope_t)


def _proj0_kernel(h_ref, mod_ref, g_ref, wk_ref, wt_ref, cos_ref, sa_ref, sb_ref, cost_ref, sint_ref,
                  qa_ref, ka_ref, va_ref, qb_ref, kb_ref, vb_ref, *, tw):
    a = _modulated(h_ref[0], g_ref[...], mod_ref[0, 0:1, :], mod_ref[0, 1:2, :]).astype(BF16)
    tm = a.shape[0]
    kw = ka_ref.shape[-1]
    p = _dot(a, wk_ref[...])
    ka_ref[0] = _rope_lanes(p[:, :kw], cos_ref[...], sa_ref[...], sb_ref[...]).astype(BF16)
    kb_ref[0] = p[:, kw:].astype(BF16)
    qw = qa_ref.shape[2]
    rows = qw + va_ref.shape[2]
    for mixer, (q_ref, v_ref) in enumerate(((qa_ref, va_ref), (qb_ref, vb_ref))):
        xt = _dot_nt(wt_ref[mixer * rows:(mixer + 1) * rows, :], a)
        q = xt[:qw]
        if mixer == 0:
            q = jnp.concatenate([_rope_rows(q[h * HEAD_DIM:(h + 1) * HEAD_DIM], cost_ref[...], sint_ref[...])
                                 for h in range(qw // HEAD_DIM)], axis=0)
        for t in range(tm // tw):
            q_ref[0, t] = q[:, t * tw:(t + 1) * tw].astype(BF16)
            v_ref[0, t] = xt[qw:, t * tw:(t + 1) * tw].astype(BF16)


def _project0(h, mod9, g, wk, wt, rope, rope_t, tm, tw):
    b, n, d = h.shape
    kw = wk.shape[1] // 2
    rows = wt.shape[0] // 2
    qw = rows - kw
    q_spec = pl.BlockSpec((1, tm // tw, qw, tw), lambda bi, i: (bi, i, 0, 0))
    k_spec = pl.BlockSpec((1, tm, kw), lambda bi, i: (bi, i, 0))
    v_spec = pl.BlockSpec((1, tm // tw, kw, tw), lambda bi, i: (bi, i, 0, 0))
    q_shape = jax.ShapeDtypeStruct((b, n // tw, qw, tw), BF16)
    k_shape = jax.ShapeDtypeStruct((b, n, kw), BF16)
    v_shape = jax.ShapeDtypeStruct((b, n // tw, kw, tw), BF16)
    return pl.pallas_call(
        functools.partial(_proj0_kernel, tw=tw),
        grid=(b, n // tm),
        in_specs=[
            pl.BlockSpec((1, tm, d), lambda bi, i: (bi, i, 0)),
            pl.BlockSpec((1, 6, d), lambda bi, i: (bi, 0, 0)),
            _const_spec((1, d)),
            _const_spec(wk.shape),
            _const_spec(wt.shape),
        ] + [pl.BlockSpec((tm, LANES), lambda bi, i: (i, 0))] * 3
          + [pl.BlockSpec((HEAD_DIM, tm), lambda bi, i: (0, i))] * 2,
        out_specs=[q_spec, k_spec, v_spec, q_spec, k_spec, v_spec],
        out_shape=[q_shape, k_shape, v_shape, q_shape, k_shape, v_shape],
        compiler_params=_vmem(48 * MIB),
        name="proj0_x",
    )(h, mod9, g.reshape(1, d), wk, wt, *rope, *rope_t)


def _pipelined(items, ahead):
    scores = [items[k][0]() for k in range(min(ahead, len(items)))]
    for k in range(len(items)):
        if k + ahead < len(items):
            scores.append(items[k + ahead][0]())
        items[k][1](scores[k])
        scores[k] = None


SHIFT_SLACK = 1.0 + 2.0 ** -10
L_FLOOR = 2.0 ** -80


def _key_norm_bounds(k_ref, kc_ref, kb_ref):
    tile = kc_ref.shape[1]
    nt = k_ref.shape[1] // tile
    width = k_ref.shape[2]

    def group_max(kt):
        kf = kt.astype(F32)
        return jnp.max((kf * kf).reshape(tile // 8, 8, width), axis=0)

    def body(j, carry):
        kb_ref[j] = group_max(k_ref[0, pl.ds(pl.multiple_of(j * tile, tile), tile), :])
        return carry

    lax.fori_loop(0, nt, body, 0)
    kb_ref[nt] = group_max(kc_ref[0])
    best = kb_ref[...].reshape((nt + 1) * 8, width)
    half = lax.broadcasted_iota(jnp.int32, best.shape, 1) < width // 2
    k0 = jnp.max(jnp.sum(jnp.where(half, best, 0.0), axis=1, keepdims=True), axis=0, keepdims=True)
    k1 = jnp.max(jnp.sum(jnp.where(half, 0.0, best), axis=1, keepdims=True), axis=0, keepdims=True)
    return jnp.sqrt(k0), jnp.sqrt(k1)


def _column_softmax(pieces, extra_row=None, shift=None):
    if shift is None:
        m = functools.reduce(jnp.maximum, [jnp.max(s, axis=0, keepdims=True) for s in pieces])
        if extra_row is not None:
            m = jnp.maximum(m, extra_row)
    else:
        m = shift
    ps = [jnp.exp2(s - m) for s in pieces]
    l = functools.reduce(jnp.add, [jnp.sum(p, axis=0, keepdims=True) for p in ps])
    if extra_row is not None:
        l = l + jnp.exp2(extra_row - m)
    return jnp.concatenate([p.astype(BF16) for p in ps], axis=0), l


def _padded_queries(top, h):
    z = jnp.zeros_like(top)
    return jnp.concatenate([top, z] if h == 0 else [z, top], axis=0)


def _win_kernel(sink_ref, q_ref, k_ref, v_ref, kc_ref, vc_ref, o_ref, kb_ref, *, blk, per_iter):
    n = k_ref.shape[1]
    nb = n // blk
    c_idx = lax.broadcasted_iota(jnp.int32, (blk, blk), 0)
    r_idx = lax.broadcasted_iota(jnp.int32, (blk, blk), 1)
    lower = jnp.tile(jnp.where(c_idx >= r_idx, 0.0, NEG), (1, 4))
    upper = jnp.tile(jnp.where(c_idx <= r_idx, 0.0, NEG), (1, 4))
    lane_head = lax.broadcasted_iota(jnp.int32, (1, 4 * blk), 1) // blk

    def sink_row(h):
        row = jnp.full((1, 4 * blk), sink_ref[4 * h + 3], F32)
        for g in range(3):
            row = jnp.where(lane_head == g, sink_ref[4 * h + g], row)
        return row

    sinks = [sink_row(0), sink_row(1)]
    kmax = _key_norm_bounds(k_ref, kc_ref, kb_ref)

    def run(exact):
        l_min = [jnp.full((1, 4 * blk), jnp.inf, F32)]

        def make_item(i, h, k_rows, v_tiles, biases):
            cache = {}

            def score():
                top = jnp.concatenate([q_ref[0, i, (4 * h + g) * HEAD_DIM:(4 * h + g + 1) * HEAD_DIM, :]
                                       for g in range(4)], axis=1)
                if not exact:
                    tf = top.astype(F32)
                    qn = jnp.sqrt(jnp.sum(tf * tf, axis=0, keepdims=True))
                    cache["shift"] = jnp.maximum(qn * kmax[h] * SHIFT_SLACK, sinks[h])
                return _dot(jnp.concatenate([k_rows(), kc_ref[0]], axis=0), _padded_queries(top, h))

            def finish(s):
                pieces = [s[t * blk:(t + 1) * blk] if bias is None else s[t * blk:(t + 1) * blk] + bias
                          for t, bias in enumerate(biases)]
                pieces.append(s[len(biases) * blk:])
                p, l = _column_softmax(pieces, sinks[h], cache.get("shift"))
                if not exact:
                    l_min[0] = jnp.minimum(l_min[0], l)
                vall = jnp.concatenate(v_tiles() + [vc_ref[0]], axis=1)
                o = _dot(vall, p)[h * HEAD_DIM:(h + 1) * HEAD_DIM] / l
                o4 = jnp.concatenate([o[:, g * blk:(g + 1) * blk] for g in range(4)], axis=0)
                q0 = i * blk if isinstance(i, int) else pl.multiple_of(i * blk, blk)
                o_ref[0, pl.ds(q0, blk), 4 * h * HEAD_DIM:4 * (h + 1) * HEAD_DIM] = o4.T.astype(BF16)

            return score, finish

        def edge_items(i, t0, biases):
            return [make_item(i, h, lambda: k_ref[0, t0 * blk:(t0 + 2) * blk, :],
                              lambda: [v_ref[0, t0], v_ref[0, t0 + 1]], biases) for h in range(2)]

        def inner_items(i):
            k0 = pl.multiple_of((i - 1) * blk, blk)
            return [make_item(i, h, lambda: k_ref[0, pl.ds(k0, 3 * blk), :],
                              lambda: [v_ref[0, i - 1], v_ref[0, i], v_ref[0, i + 1]], [lower, None, upper])
                    for h in range(2)]

        _pipelined(edge_items(0, 0, [None, upper]), 1)

        def body(j, carry):
            l_min[0] = carry
            _pipelined([it for u in range(per_iter) for it in inner_items(1 + per_iter * j + u)], 2)
            return l_min[0]

        l_min[0] = lax.fori_loop(0, (nb - 2) // per_iter, body, l_min[0])
        _pipelined(edge_items(nb - 1, nb - 2, [lower, None]), 1)
        return l_min[0]

    l_fast = run(exact=False)

    @pl.when(jnp.min(l_fast) < L_FLOOR)
    def _():
        run(exact=True)


def _window_attn(sink, q_t, k, v_t, kc, vc_t):
    b, nb, hd, blk = q_t.shape
    n = k.shape[1]
    c = kc.shape[1]
    per_iter = 5
    assert blk == WINDOW and (nb - 2) % per_iter == 0
    return pl.pallas_call(
        functools.partial(_win_kernel, blk=blk, per_iter=per_iter),
        grid=(b,),
        in_specs=[
            pl.BlockSpec(memory_space=pltpu.SMEM),
            pl.BlockSpec((1, nb, hd, blk), lambda bi: (bi, 0, 0, 0)),
            pl.BlockSpec((1, n, 2 * HEAD_DIM), lambda bi: (bi, 0, 0)),
            pl.BlockSpec((1, nb, 2 * HEAD_DIM, blk), lambda bi: (bi, 0, 0, 0)),
            pl.BlockSpec((1, c, 2 * HEAD_DIM), lambda bi: (bi, 0, 0)),
            pl.BlockSpec((1, 2 * HEAD_DIM, c), lambda bi: (bi, 0, 0)),
        ],
        out_specs=pl.BlockSpec((1, n, hd), lambda bi: (bi, 0, 0)),
        out_shape=jax.ShapeDtypeStruct((b, n, hd), BF16),
        scratch_shapes=[pltpu.VMEM((n // c + 1, 8, 2 * HEAD_DIM), F32)],
        compiler_params=_vmem(48 * MIB),
        name="window_attn",
    )(sink, q_t, k, v_t, kc, vc_t)


def _na_bias_kernel(rpb_ref, o_ref):
    h = pl.program_id(0)
    n_roff = 2 * NA_ROWS - 1
    n_coff = 2 * NA_COLS - 1
    shape = (GRID_W, 4 * GRID_W)
    ck = lax.broadcasted_iota(jnp.int32, shape, 0)
    lane = lax.broadcasted_iota(jnp.int32, shape, 1)
    cq = lane % GRID_W
    head = lane // GRID_W
    cstart = jnp.clip(cq - NA_COLS // 2, 0, GRID_W - NA_COLS)
    valid = (ck >= cstart) & (ck < cstart + NA_COLS)
    diff = ck - cq + (NA_COLS - 1)
    hits = [[valid & (diff == o) & (head == g) for o in range(n_coff)] for g in range(4)]
    tiles = []
    for roff in range(n_roff):
        t = jnp.full(shape, NEG, F32)
        for g in range(4):
            for o in range(n_coff):
                t = jnp.where(hits[g][o], rpb_ref[((4 * h + g) * n_roff + roff) * n_coff + o] * LOG2E, t)
        tiles.append(t)
    for ro in range(n_roff - 1):
        o_ref[0, ro] = jnp.concatenate([tiles[ro], tiles[ro + 1]], axis=0)


def _na_bias(rpb):
    hq, n_roff, n_coff = rpb.shape
    return pl.pallas_call(
        _na_bias_kernel,
        grid=(hq // 4,),
        in_specs=[pl.BlockSpec(memory_space=pltpu.SMEM)],
        out_specs=pl.BlockSpec((1, n_roff - 1, 2 * GRID_W, 4 * GRID_W), lambda h: (h, 0, 0, 0)),
        out_shape=jax.ShapeDtypeStruct((hq // 4, n_roff - 1, 2 * GRID_W, 4 * GRID_W), F32),
        name="na_bias",
    )(rpb.reshape(-1))


def _na_kernel(q_ref, k_ref, v_ref, kc_ref, vc_ref, bias_ref, o_ref, vall_ref, kb_ref):
    n = k_ref.shape[1]
    rows = n // GRID_W
    n_tiles = rows // 2
    kr = min(NA_ROWS, rows)
    nkey = kr * GRID_W
    lo = lax.broadcasted_iota(jnp.int32, (2 * HEAD_DIM, 2 * GRID_W), 1) < GRID_W
    lo_q = lax.broadcasted_iota(jnp.int32, (HEAD_DIM, 2 * GRID_W), 1) < GRID_W

    def swap(x):
        return pltpu.roll(x, GRID_W, 1)

    def build(t, carry):
        a = v_ref[0, t]
        vall_ref[0, t] = a
        nxt = v_ref[0, jnp.minimum(t + 1, n_tiles - 1)]
        vall_ref[1, t] = jnp.where(lo, swap(a.astype(F32)), swap(nxt.astype(F32))).astype(BF16)
        return carry

    lax.fori_loop(0, n_tiles, build, 0)

    def row_start(r):
        return jnp.clip(r - kr // 2, 0, rows - kr)

    kmax = _key_norm_bounds(k_ref, kc_ref, kb_ref)
    bias_max = [jnp.maximum(jnp.max(jnp.max(bias_ref[h], axis=0), axis=0, keepdims=True), 0.0) for h in range(2)]
    tiles_per_iter = 4

    def run(exact):
        l_min = [jnp.full((1, 4 * GRID_W), jnp.inf, F32)]

        def make_item(t, par, h, done):
            r = 2 * t + par
            cache = {}

            def score():
                cols = []
                for c in range(2):
                    g0 = 4 * h + 2 * c
                    a0 = q_ref[0, t, g0 * HEAD_DIM:(g0 + 1) * HEAD_DIM, :].astype(F32)
                    a1 = q_ref[0, t, (g0 + 1) * HEAD_DIM:(g0 + 2) * HEAD_DIM, :].astype(F32)
                    cols.append(jnp.where(lo_q, a0, swap(a1)) if par == 0 else jnp.where(lo_q, swap(a0), a1))
                top = jnp.concatenate(cols, axis=1)
                if not exact:
                    qn = jnp.sqrt(jnp.sum(top * top, axis=0, keepdims=True))
                    cache["shift"] = qn * kmax[h] * SHIFT_SLACK + bias_max[h]
                k0 = pl.multiple_of(row_start(r) * GRID_W, GRID_W)
                return _dot(jnp.concatenate([k_ref[0, pl.ds(k0, nkey), :], kc_ref[0]], axis=0),
                            _padded_queries(top.astype(BF16), h))

            def finish(s):
                rs = row_start(r)
                ro0 = rs - r + (NA_ROWS - 1)
                bias = jnp.concatenate([bias_ref[h, ro0 + 2 * i] for i in range(kr // 2)], axis=0)
                p, l = _column_softmax([s[:nkey] + bias, s[nkey:]], None, cache.get("shift"))
                if not exact:
                    l_min[0] = jnp.minimum(l_min[0], l)
                vwin = [vall_ref[rs & 1, (rs >> 1) + i] for i in range(nkey // (2 * GRID_W))]
                o = _dot(jnp.concatenate(vwin + [vc_ref[0]], axis=1), p)[h * HEAD_DIM:(h + 1) * HEAD_DIM] / l
                done.append(o)
                if par == 1:
                    o_even, o_odd = done
                    ws = []
                    for c in range(2):
                        ce = o_even[:, c * 2 * GRID_W:(c + 1) * 2 * GRID_W]
                        co = o_odd[:, c * 2 * GRID_W:(c + 1) * 2 * GRID_W]
                        ws.append(jnp.where(lo_q, ce, swap(co)))
                        ws.append(jnp.where(lo_q, swap(ce), co))
                    w = jnp.concatenate(ws, axis=0)
                    q0 = pl.multiple_of(t * 2 * GRID_W, 2 * GRID_W)
                    o_ref[0, pl.ds(q0, 2 * GRID_W), 4 * h * HEAD_DIM:4 * (h + 1) * HEAD_DIM] = w.T.astype(BF16)

            return score, finish

        def body(j, carry):
            l_min[0] = carry
            items = []
            for u in range(tiles_per_iter):
                for h in range(2):
                    done = []
                    items += [make_item(tiles_per_iter * j + u, par, h, done) for par in range(2)]
            _pipelined(items, 2)
            return l_min[0]

        return lax.fori_loop(0, n_tiles // tiles_per_iter, body, l_min[0])

    l_fast = run(exact=False)

    @pl.when(jnp.min(l_fast) < L_FLOOR)
    def _():
        run(exact=True)


def _neighbourhood_attn(q_t, k, v_t, kc, vc_t, bias):
    b, n_tiles, hd, tw = q_t.shape
    n = k.shape[1]
    c = kc.shape[1]
    assert tw == 2 * GRID_W and n_tiles * tw == n
    return pl.pallas_call(
        _na_kernel,
        grid=(b,),
        in_specs=[
            pl.BlockSpec((1, n_tiles, hd, tw), lambda bi: (bi, 0, 0, 0)),
            pl.BlockSpec((1, n, 2 * HEAD_DIM), lambda bi: (bi, 0, 0)),
            pl.BlockSpec((1, n_tiles, 2 * HEAD_DIM, tw), lambda bi: (bi, 0, 0, 0)),
            pl.BlockSpec((1, c, 2 * HEAD_DIM), lambda bi: (bi, 0, 0)),
            pl.BlockSpec((1, 2 * HEAD_DIM, c), lambda bi: (bi, 0, 0)),
            _const_spec(bias.shape),
        ],
        out_specs=pl.BlockSpec((1, n, hd), lambda bi: (bi, 0, 0)),
        out_shape=jax.ShapeDtypeStruct((b, n, hd), BF16),
        scratch_shapes=[pltpu.VMEM((2, n_tiles, 2 * HEAD_DIM, tw), BF16),
                        pltpu.VMEM((n // c + 1, 8, 2 * HEAD_DIM), F32)],
        compiler_params=_vmem(52 * MIB),
        name="neighbourhood_attn",
    )(q_t, k, v_t, kc, vc_t, bias)


def _diff_kernel(lq1_ref, lk1_ref, lq2_ref, lk2_ref, g_ref, qt_ref, kk_ref, kkc_ref, vt_ref, vtc_ref, o_ref,
                 qz_ref, kb_ref, kmax_ref, m_ref, l_ref, acc1_ref, acc2_ref, *, lam_init, n_sub):
    tq = qt_ref.shape[-1] // n_sub
    n_lt, dv, tk = vt_ref.shape[2:]
    n_kt = n_lt + 1
    assert kkc_ref.shape[2] == tk and vtc_ref.shape[2] == 1

    def k_tile(j):
        return kk_ref[0, 0, j * tk:(j + 1) * tk, :] if j < n_lt else kkc_ref[0, 0]

    def v_tile(j):
        return vt_ref[0, 0, j] if j < n_lt else vtc_ref[0, 0, 0]

    @pl.when(pl.program_id(2) == 0)
    def _():
        def group_max(kt):
            kf = kt.astype(F32)
            return jnp.max((kf * kf).reshape(tk // 8, 8, dv), axis=0)

        def kbody(j, carry):
            kb_ref[j] = group_max(kk_ref[0, 0, pl.ds(pl.multiple_of(j * tk, tk), tk), :])
            return carry

        lax.fori_loop(0, n_lt, kbody, 0)
        kb_ref[n_lt] = group_max(kkc_ref[0, 0])
        best = kb_ref[...].reshape(n_kt * 8, dv)
        half = lax.broadcasted_iota(jnp.int32, best.shape, 1) < HEAD_DIM
        k1 = jnp.max(jnp.sum(jnp.where(half, best, 0.0), axis=1, keepdims=True), axis=0, keepdims=True)
        k2 = jnp.max(jnp.sum(jnp.where(half, 0.0, best), axis=1, keepdims=True), axis=0, keepdims=True)
        kmax_ref[:, :tq] = jnp.broadcast_to(jnp.sqrt(k1), (1, tq))
        kmax_ref[:, tq:] = jnp.broadcast_to(jnp.sqrt(k2), (1, tq))

    zero = jnp.zeros((HEAD_DIM, tq), BF16)
    state = []
    for s in range(n_sub):
        qt = qt_ref[0, 0, :, s * tq:(s + 1) * tq]
        qz_ref[s, 0:HEAD_DIM, 0:tq] = qt[0:HEAD_DIM]
        qz_ref[s, 0:HEAD_DIM, tq:] = zero
        qz_ref[s, HEAD_DIM:, 0:tq] = zero
        qz_ref[s, HEAD_DIM:, tq:] = qt[HEAD_DIM:]
        qf = qt.astype(F32)
        qsq = qf * qf
        qn = jnp.concatenate([jnp.sum(qsq[:HEAD_DIM], axis=0, keepdims=True),
                              jnp.sum(qsq[HEAD_DIM:], axis=0, keepdims=True)], axis=1)
        state.append(dict(qz=qz_ref[s], shift=jnp.sqrt(qn) * kmax_ref[...] * SHIFT_SLACK,
                          l8=jnp.zeros((8, 2 * tq), F32), acc1=jnp.zeros((dv, tq), F32), acc2=jnp.zeros((dv, tq), F32)))

    lam = (jnp.exp(jnp.sum(lq1_ref[...] * lk1_ref[...], axis=-1, keepdims=True))
           - jnp.exp(jnp.sum(lq2_ref[...] * lk2_ref[...], axis=-1, keepdims=True)) + lam_init)

    def finalize(s, l, acc1, acc2):
        o = acc1 / l[:, :tq] - lam * (acc2 / l[:, tq:])
        ms = jnp.mean(o * o, axis=0, keepdims=True)
        o = (o * lax.rsqrt(ms + EPS) * g_ref[...]) * (1.0 - lam_init)
        start = s * tq if isinstance(s, int) else pl.multiple_of(s * tq, tq)
        o_ref[0, 0, pl.ds(start, tq), :] = o.T.astype(BF16)

    def make_item(s, j):
        st = state[s]

        def finish(scores):
            p = jnp.exp2(scores - st["shift"])
            st["l8"] = st["l8"] + jnp.sum(p.reshape(tk // 8, 8, 2 * tq), axis=0)
            pb = p.astype(BF16)
            vt = v_tile(j)
            st["acc1"] = st["acc1"] + _dot(vt, pb[:, :tq])
            st["acc2"] = st["acc2"] + _dot(vt, pb[:, tq:])
            if j == n_kt - 1:
                l_fast = jnp.sum(st["l8"], axis=0, keepdims=True)
                l_ref[s] = l_fast
                finalize(s, l_fast, st["acc1"], st["acc2"])

        return (lambda: _dot(k_tile(j), st["qz"])), finish

    _pipelined([make_item(s, j) for s in range(n_sub) for j in range(n_kt)], 2)
    l_min = functools.reduce(jnp.minimum, [l_ref[s] for s in range(n_sub)])

    @pl.when(jnp.min(l_min) < L_FLOOR)
    def _():
        def redo(s, outer):
            m_ref[...] = jnp.full(m_ref.shape, NEG, F32)
            l_ref[s] = jnp.zeros((1, 2 * tq), F32)
            acc1_ref[s] = jnp.zeros((dv, tq), F32)
            acc2_ref[s] = jnp.zeros((dv, tq), F32)

            def step(kt, vt):
                st = _dot(kt, qz_ref[s])
                m_old = m_ref[...]
                m_new = jnp.maximum(m_old, jnp.max(st, axis=0, keepdims=True))
                alpha = jnp.exp2(m_old - m_new)
                p = jnp.exp2(st - m_new)
                l_ref[s] = alpha * l_ref[s] + jnp.sum(p, axis=0, keepdims=True)
                m_ref[...] = m_new
                pb = p.astype(BF16)
                acc1_ref[s] = alpha[:, :tq] * acc1_ref[s] + _dot(vt, pb[:, :tq])
                acc2_ref[s] = alpha[:, tq:] * acc2_ref[s] + _dot(vt, pb[:, tq:])

            def body(j, carry):
                step(kk_ref[0, 0, pl.ds(pl.multiple_of(j * tk, tk), tk), :], vt_ref[0, 0, j])
                return carry

            lax.fori_loop(0, n_lt, body, 0)
            step(kkc_ref[0, 0], vtc_ref[0, 0, 0])
            finalize(s, l_ref[s], acc1_ref[s], acc2_ref[s])
            return outer

        lax.fori_loop(0, n_sub, redo, 0)


def _diff_attn(lam_vecs, sub_g, qt, kk, kkc, vt, vtc, lam_init, tq_block, n_sub):
    b, h, dv, n = qt.shape
    n_ctx = kkc.shape[2]
    n_lt, _, tk = vt.shape[2:]
    tq = tq_block // n_sub
    vec_spec = pl.BlockSpec((1, HEAD_DIM), lambda bi, hi, i: (0, 0))
    return pl.pallas_call(
        functools.partial(_diff_kernel, lam_init=lam_init, n_sub=n_sub),
        grid=(b, h, n // tq_block),
        in_specs=[vec_spec] * 4 + [
            pl.BlockSpec((dv, 1), lambda bi, hi, i: (0, 0)),
            pl.BlockSpec((1, 1, dv, tq_block), lambda bi, hi, i: (bi, hi, 0, i)),
            pl.BlockSpec((1, 1, n, dv), lambda bi, hi, i: (bi, hi, 0, 0)),
            pl.BlockSpec((1, 1, n_ctx, dv), lambda bi, hi, i: (bi, hi, 0, 0)),
            pl.BlockSpec((1, 1, n_lt, dv, tk), lambda bi, hi, i: (bi, hi, 0, 0, 0)),
            pl.BlockSpec((1, 1, 1, dv, tk), lambda bi, hi, i: (bi, hi, 0, 0, 0)),
        ],
        out_specs=pl.BlockSpec((1, 1, tq_block, dv), lambda bi, hi, i: (bi, hi, i, 0)),
        out_shape=jax.ShapeDtypeStruct((b, h, n, dv), BF16),
        scratch_shapes=[
            pltpu.VMEM((n_sub, dv, 2 * tq), BF16),
            pltpu.VMEM((n_lt + 1, 8, dv), F32),
            pltpu.VMEM((1, 2 * tq), F32),
            pltpu.VMEM((1, 2 * tq), F32),
            pltpu.VMEM((n_sub, 1, 2 * tq), F32),
            pltpu.VMEM((n_sub, dv, tq), F32),
            pltpu.VMEM((n_sub, dv, tq), F32),
        ],
        compiler_params=_vmem(32 * MIB, dimension_semantics=("arbitrary", "arbitrary", "arbitrary")),
        name="diff_attn",
    )(*lam_vecs, sub_g.reshape(dv, 1), qt, kk, kkc, vt, vtc)


FF_CHUNK = 1024


def _residual_mlp(h, ys, mod_ref, g_ref, wo_ref, w1_ref, w2_ref):
    yw = wo_ref.shape[0] // len(ys)
    y = functools.reduce(jnp.add, [_dot(yj, wo_ref[j * yw:(j + 1) * yw, :]) for j, yj in enumerate(ys)])
    hx = h + mod_ref[0, 2:3, :] * y
    a = _modulated(hx, g_ref[...], mod_ref[0, 3:4, :], mod_ref[0, 4:5, :]).astype(BF16)
    acc = jnp.zeros(hx.shape, F32)
    for c0 in range(0, w1_ref.shape[1], FF_CHUNK):
        u = jnp.maximum(_dot(a, w1_ref[:, c0:c0 + FF_CHUNK]), 0.0)
        acc = acc + _dot((u * u).astype(BF16), w2_ref[c0:c0 + FF_CHUNK, :])
    return hx + mod_ref[0, 5:6, :] * acc


def _mlp_kernel(*refs, n_y, final):
    y_refs, refs = refs[:n_y], refs[n_y:]
    if final:
        h_ref, mod_ref, g_ref, wo_ref, w1_ref, w2_ref, fg_ref, o_ref = refs
    else:
        h_ref, mod_ref, g_ref, wo_ref, w1_ref, w2_ref, o_ref = refs
    ys = [y_ref[0] if len(y_ref.shape) == 3 else jnp.concatenate([y_ref[0, h] for h in range(y_ref.shape[1])], axis=1)
          for y_ref in y_refs]
    out = _residual_mlp(h_ref[0], ys, mod_ref, g_ref, wo_ref, w1_ref, w2_ref)
    if final:
        ms = jnp.mean(out * out, axis=-1, keepdims=True)
        out = out * lax.rsqrt(ms + EPS) * fg_ref[...]
    o_ref[0] = out


def _out_mlp(ys, h, mod9, g2, wo, w1, w2, final_g, tm, name):
    b, n, d = h.shape
    dff = w1.shape[1]
    final = final_g is not None
    tile = pl.BlockSpec((1, tm, d), lambda bi, i: (bi, i, 0))
    in_specs = [pl.BlockSpec((1, tm, y.shape[-1]), lambda bi, i: (bi, i, 0)) if y.ndim == 3 else
                pl.BlockSpec((1, y.shape[1], tm, y.shape[-1]), lambda bi, i: (bi, 0, i, 0)) for y in ys]
    in_specs += [tile, pl.BlockSpec((1, 6, d), lambda bi, i: (bi, 0, 0)), _const_spec((1, d)),
                 _const_spec((d, d)), _const_spec((d, dff)), _const_spec((dff, d))]
    args = list(ys) + [h, mod9, g2.reshape(1, d), wo, w1, w2]
    if final:
        in_specs.append(_const_spec((1, d)))
        args.append(final_g.reshape(1, d))
    return pl.pallas_call(
        functools.partial(_mlp_kernel, n_y=len(ys), final=final),
        grid=(b, n // tm),
        in_specs=in_specs,
        out_specs=tile,
        out_shape=jax.ShapeDtypeStruct((b, n, d), F32),
        compiler_params=_vmem(56 * MIB),
        name=name,
    )(*args)


def _ctx_kernel(sink_ref, ctx_ref, mod0_ref, mod1_ref, g10_ref, g20_ref, g11_ref, wk0_ref, wt0_ref, wo_ref,
                w1_ref, w2_ref, wk1_ref, wv1t_ref, cak_ref, cavt_ref, cbk_ref, cbvt_ref, kkc_ref, vtc_ref):
    x = ctx_ref[0]
    c = x.shape[0]
    a = _modulated(x, g10_ref[...], mod0_ref[0, 0:1, :], mod0_ref[0, 1:2, :]).astype(BF16)
    kw = cak_ref.shape[-1]
    k_nat = _dot(a, wk0_ref[...]).astype(BF16)
    cak_ref[0] = k_nat[:, :kw]
    cbk_ref[0] = k_nat[:, kw:]
    rows = wt0_ref.shape[0] // 2
    qw = rows - kw
    lane_head = lax.broadcasted_iota(jnp.int32, (1, 4 * c), 1) // c
    ys = []
    for mixer, vt_ref in enumerate((cavt_ref, cbvt_ref)):
        xt = _dot_nt(wt0_ref[mixer * rows:(mixer + 1) * rows, :], a)
        vt = xt[qw:].astype(BF16)
        vt_ref[0] = vt
        km = k_nat[:, mixer * kw:(mixer + 1) * kw]
        cols = []
        for h in range(2):
            top = jnp.concatenate([xt[(4 * h + g) * HEAD_DIM:(4 * h + g + 1) * HEAD_DIM] for g in range(4)], axis=1)
            s = _dot(km, _padded_queries(top.astype(BF16), h))
            sink = None
            if mixer == 0:
                sink = jnp.full((1, 4 * c), sink_ref[4 * h + 3], F32)
                for g in range(3):
                    sink = jnp.where(lane_head == g, sink_ref[4 * h + g], sink)
            p, l = _column_softmax([s], sink)
            o = _dot(vt, p)[h * HEAD_DIM:(h + 1) * HEAD_DIM] / l
            cols.append(jnp.concatenate([o[:, g * c:(g + 1) * c] for g in range(4)], axis=0).T)
        ys.append(jnp.concatenate(cols, axis=1).astype(BF16))
    hc = _residual_mlp(x, ys, mod0_ref, g20_ref, wo_ref, w1_ref, w2_ref)
    a1 = _modulated(hc, g11_ref[...], mod1_ref[0, 0:1, :], mod1_ref[0, 1:2, :]).astype(BF16)
    dv = kkc_ref.shape[-1]
    kk = _dot(a1, wk1_ref[...])
    vt1 = _dot_nt(wv1t_ref[...], a1)
    for h in range(kkc_ref.shape[1]):
        kkc_ref[0, h] = kk[:, h * dv:(h + 1) * dv].astype(BF16)
        vtc_ref[0, h, 0] = vt1[h * dv:(h + 1) * dv].astype(BF16)


def _ctx_layer0(sink, ctx, mod0, mod1, g10, g20, g11, wk0, wt0, wo, w1, w2, wk1, wv1t):
    b, c, d = ctx.shape
    kw = wk0.shape[1] // 2
    dv = 2 * HEAD_DIM
    n_heads = wk1.shape[1] // dv
    row = lambda v: v.reshape(1, d)
    k_spec = pl.BlockSpec((1, c, kw), lambda bi: (bi, 0, 0))
    vt_spec = pl.BlockSpec((1, kw, c), lambda bi: (bi, 0, 0))
    k_shape = jax.ShapeDtypeStruct((b, c, kw), BF16)
    vt_shape = jax.ShapeDtypeStruct((b, kw, c), BF16)
    return pl.pallas_call(
        _ctx_kernel,
        grid=(b,),
        in_specs=[
            pl.BlockSpec(memory_space=pltpu.SMEM),
            pl.BlockSpec((1, c, d), lambda bi: (bi, 0, 0)),
            _const_spec((1, 6, d)), _const_spec((1, 6, d)),
            _const_spec((1, d)), _const_spec((1, d)), _const_spec((1, d)),
        ] + [_const_spec(w.shape) for w in (wk0, wt0, wo, w1, w2, wk1, wv1t)],
        out_specs=[k_spec, vt_spec, k_spec, vt_spec,
                   pl.BlockSpec((1, n_heads, c, dv), lambda bi: (bi, 0, 0, 0)),
                   pl.BlockSpec((1, n_heads, 1, dv, c), lambda bi: (bi, 0, 0, 0, 0))],
        out_shape=[k_shape, vt_shape, k_shape, vt_shape,
                   jax.ShapeDtypeStruct((b, n_heads, c, dv), BF16),
                   jax.ShapeDtypeStruct((b, n_heads, 1, dv, c), BF16)],
        compiler_params=_vmem(48 * MIB),
        name="ctx_layer0",
    )(sink, ctx, mod0, mod1, row(g10), row(g20), row(g11), wk0, wt0, wo, w1, w2, wk1, wv1t)


def _rope_tables(n):
    t = jnp.arange(n, dtype=jnp.int32)
    row = (t // GRID_W).astype(F32)
    col = (t % GRID_W).astype(F32)
    quarter = HEAD_DIM // 4
    inv_freq = ROPE_THETA ** (-jnp.arange(quarter, dtype=F32) / quarter)
    ar = row[:, None] * inv_freq[None, :]
    ac = col[:, None] * inv_freq[None, :]
    ang = jnp.concatenate([ar, ar, ac, ac], axis=-1)
    cos, sin = jnp.cos(ang), jnp.sin(ang)
    even = (jnp.arange(HEAD_DIM) // quarter) % 2 == 0
    rep = LANES // HEAD_DIM
    sin_a = jnp.tile(jnp.where(even, -sin, 0.0), (1, rep))
    sin_b = jnp.tile(jnp.where(even, 0.0, sin), (1, rep))
    token_major = (jnp.tile(cos, (1, rep)), sin_a, sin_b)
    feature_major = (cos.T, jnp.where(even, -sin, sin).T)
    return token_major, feature_major


def kernel(x, c, ctx, c_ctx, ada_w, ada_b, norm1_g, norm2_g, even_w_in, even_w_out, a_sink, b_rpb,
           odd_w_in, odd_w_out, lam_q1, lam_k1, lam_q2, lam_k2, subln_g, mlp_w1, mlp_w2, final_g):
    b, n, d = x.shape
    n_ctx = ctx.shape[1]
    assert ada_w.shape[0] == 2 and d == 1024 and n % 512 == 0
    scale = HEAD_DIM ** -0.5
    a_qw, a_kvw = 8 * HEAD_DIM, 2 * HEAD_DIM
    b_q0 = a_qw + 2 * a_kvw

    cc = jnp.zeros((16, d), F32).at[:b].set(c).at[b].set(c_ctx)
    mod = _ada_call(cc, ada_w, ada_b)[:, :b + 1].reshape(2, b + 1, 6, d)
    rope, rope_t = _rope_tables(n)

    qscale = scale * LOG2E
    cols0 = jnp.arange(even_w_in.shape[2])
    qcols0 = (cols0 < a_qw) | ((cols0 >= b_q0) & (cols0 < b_q0 + a_qw))
    w_in0 = (even_w_in[0] * jnp.where(qcols0, qscale, 1.0)).astype(BF16)
    cols1 = jnp.arange(odd_w_in.shape[2])
    w_in1 = (odd_w_in[0] * jnp.where(cols1 < 1024, qscale, 1.0)).astype(BF16)
    w_out0, w_out1 = even_w_out[0].astype(BF16), odd_w_out[0].astype(BF16)
    w1, w2 = mlp_w1.astype(BF16), mlp_w2.astype(BF16)

    ak0, av0, bq0, bk0, bv0 = a_qw, a_qw + a_kvw, b_q0, b_q0 + a_qw, b_q0 + a_qw + a_kvw
    wk0 = jnp.concatenate([w_in0[:, ak0:av0], w_in0[:, bk0:bv0]], axis=1)
    wt0 = jnp.concatenate([w_in0[:, :ak0], w_in0[:, av0:bq0], w_in0[:, bq0:bk0], w_in0[:, bv0:]], axis=1).T
    wq1, wk1, wv1 = w_in1[:, :1024], w_in1[:, 1024:2048], w_in1[:, 2048:]
    wt1 = jnp.concatenate([wq1, wv1], axis=1).T
    sink = a_sink[0].astype(F32) * LOG2E

    cak, cav_t, cbk, cbv_t, kkc, vtc = _ctx_layer0(
        sink, ctx, mod[0, b:b + 1], mod[1, b:b + 1], norm1_g[0], norm2_g[0], norm1_g[1],
        wk0, wt0, w_out0, w1[0], w2[0], wk1, wt1[1024:])

    aq_t, ak, av_t, bq_t, bk, bv_t = _project0(x, mod[0], norm1_g[0], wk0, wt0, rope, rope_t, 1024, WINDOW)
    ya = _window_attn(sink, aq_t, ak, av_t, cak, cav_t)
    yb = _neighbourhood_attn(bq_t, bk, bv_t, cbk, cbv_t, _na_bias(b_rpb[0].astype(F32)))
    hx = _out_mlp([ya, yb], x, mod[0], norm2_g[0], w_out0, w1[0], w2[0], None, 1024, "mlp0_x")

    lam_init = 0.8 - 0.6 * math.exp(-0.3 * 1)
    qt, kk, vt = _project1(hx, mod[1], norm1_g[1], wk1, wt1, rope, rope_t, 1024, n_ctx)
    lam_vecs = [v[0].reshape(1, HEAD_DIM).astype(F32) for v in (lam_q1, lam_k1, lam_q2, lam_k2)]
    yx = _diff_attn(lam_vecs, subln_g[0].astype(F32), qt, kk, kkc, vt, vtc, lam_init, 2048, 8)
    return _out_mlp([yx], hx, mod[1], norm2_g[1], w_out1, w1[1], w2[1], final_g, 1024, "mlp1_x")
```
